```python
import math
import jax, jax.numpy as jnp
from jax import lax
import numpy as np

D_MODEL = 1024
BATCH = 4
SEQ = 8192
DEPTH = 4

GRID_W = 64
CTX_LEN = 256
N_MIXERS = 4
N_SSD_LAYERS = (DEPTH + 3) // 4
N_LRU_LAYERS = (DEPTH + 2) // 4
N_S5_LAYERS = (DEPTH + 1) // 4
N_MLA_LAYERS = DEPTH // 4
NORM_EPS = 1e-6
CONV_WIDTH = 4
CONV_PAD_LEFT = 2
FFN_HIDDEN = ((8 * D_MODEL // 3 + 255) // 256) * 256
M2_INNER = 2 * D_MODEL
M2_HEADDIM = 64
M2_HEADS = M2_INNER // M2_HEADDIM
M2_GROUPS = 4
M2_HPG = M2_HEADS // M2_GROUPS
M2_STATE = 128
M2_CONV_CH = M2_INNER + 2 * M2_GROUPS * M2_STATE
M2_PROJ = M2_INNER + M2_CONV_CH + 2 * M2_HEADS
SSD_CHUNK = 128
LRU_WIDTH = 5 * D_MODEL // 4
LRU_BLOCK = 128
LRU_BLOCKS = LRU_WIDTH // LRU_BLOCK
LRU_C = 8.0
S5_GROUP = 16
S5_GROUPS = D_MODEL // S5_GROUP
S5_STATE = 64
MLA_HEADS = 16
MLA_NOPE = 64
MLA_ROPE = 32
MLA_V = 64
MLA_Q_RANK = 384
MLA_KV_RANK = 256
MLA_IN = MLA_Q_RANK + MLA_KV_RANK + MLA_ROPE
ROPE_FREQ = MLA_ROPE // 4
ROPE_BASE = 10000.0
Q_BLOCK = 128

kernel_name = 'hybrid_interleaved_flow_trunk'


def rms_norm(x, w):
    xf = x.astype(jnp.float32)
    y = xf * lax.rsqrt(jnp.mean(xf * xf, axis=-1, keepdims=True) + NORM_EPS)
    return (y * w.astype(jnp.float32)).astype(x.dtype)


def dwconv(x, w, b):
    y = lax.conv_general_dilated(x, w[:, None, :], window_strides=(1,),
                                 padding=[(CONV_PAD_LEFT, CONV_WIDTH - 1 - CONV_PAD_LEFT)],
                                 dimension_numbers=('NWC', 'WIO', 'NWC'),
                                 feature_group_count=x.shape[-1])
    return y + b


def swiglu(h, w13, w2):
    a, g = jnp.split(h @ w13, 2, axis=-1)
    return (jax.nn.silu(a) * g) @ w2


def rope2d(x, cos, sin):
    xs = x.reshape(x.shape[:-1] + (2, 2, ROPE_FREQ))
    x1, x2 = xs[..., 0, :], xs[..., 1, :]
    out = jnp.stack([x1 * cos - x2 * sin, x2 * cos + x1 * sin], axis=-2)
    return out.reshape(x.shape)


def attend_blocks(q, k, v):
    b, lq, h, dq = q.shape
    nb = lq // Q_BLOCK
    qb = jnp.moveaxis(q.reshape(b, nb, Q_BLOCK, h, dq), 1, 0)
    scale = dq ** -0.5

    def one(qi):
        s = jnp.einsum('bqhd,bkhd->bhqk', qi, k).astype(jnp.float32) * scale
        p = jax.nn.softmax(s, axis=-1).astype(v.dtype)
        return jnp.einsum('bhqk,bkhd->bqhd', p, v)

    o = lax.map(one, qb)
    return jnp.moveaxis(o, 0, 1).reshape(b, lq, h, v.shape[-1])


def prefix_bidir(scan_f, scan_b, ctx_args, lat_args, h0):
    flip = lambda args: tuple(jnp.flip(a, axis=1) for a in args)
    yc_f, sc_f = scan_f(ctx_args, h0)
    yl_f, _ = scan_f(lat_args, sc_f)
    yc_b, sc_b = scan_b(flip(ctx_args), h0)
    yl_b, _ = scan_b(flip(lat_args), sc_b)
    return yc_f + jnp.flip(yc_b, axis=1), yl_f + jnp.flip(yl_b, axis=1)


def linear_scan(a, u, h0):
    def comb(l, r):
        return l[0] * r[0], r[0] * l[1] + r[1]
    a_cum, h = lax.associative_scan(comb, (a, u), axis=1)
    h = h + a_cum * h0[:, None]
    return h, h[:, -1]


def complex_linear_scan(ar, ai, ur, ui, h0):
    def comb(l, r):
        lar, lai, lur, lui = l
        rar, rai, rur, rui = r
        return (rar * lar - rai * lai, rar * lai + rai * lar,
                rar * lur - rai * lui + rur, rar * lui + rai * lur + rui)
    a_r, a_i, h_r, h_i = lax.associative_scan(comb, (ar, ai, ur, ui), axis=1)
    h0r, h0i = h0
    hr = h_r + a_r * h0r[:, None] - a_i * h0i[:, None]
    hi = h_i + a_r * h0i[:, None] + a_i * h0r[:, None]
    return (hr, hi), (hr[:, -1], hi[:, -1])


def ssd_chunked(xs, da, bm, cm, h0):
    b, n = xs.shape[:2]
    t, q = n // SSD_CHUNK, SSD_CHUNK
    xs = xs.reshape(b, t, q, M2_GROUPS, M2_HPG, M2_HEADDIM)
    da = da.reshape(b, t, q, M2_GROUPS, M2_HPG)
    bm = bm.reshape(b, t, q, M2_GROUPS, M2_STATE)
    cm = cm.reshape(b, t, q, M2_GROUPS, M2_STATE)
    acs = jnp.cumsum(da, axis=2)
    tri = jnp.tril(jnp.ones((q, q), dtype=bool))[:, :, None, None]
    decay_in = jnp.exp(jnp.where(tri, acs[:, :, :, None] - acs[:, :, None, :], -jnp.inf))
    scores = jnp.einsum('btlgn,btsgn->btlsg', cm, bm)
    y_diag = jnp.einsum('btlsg,btlsgk,btsgkp->btlgkp', scores, decay_in, xs)
    decay_out = jnp.exp(acs[:, :, -1:] - acs)
    chunk_states = jnp.einsum('btsgn,btsgk,btsgkp->btgkpn', bm, decay_out, xs)
    chunk_decay = jnp.exp(acs[:, :, -1])

    def step(h, inp):
        s, d = inp
        return h * d[..., None, None] + s, h

    h_last, h_in = lax.scan(step, h0, (jnp.moveaxis(chunk_states, 1, 0), jnp.moveaxis(chunk_decay, 1, 0)))
    y_off = jnp.einsum('btlgn,tbgkpn,btlgk->btlgkp', cm, h_in, jnp.exp(acs))
    return (y_diag + y_off).reshape(b, n, M2_GROUPS, M2_HPG, M2_HEADDIM), h_last


def mamba2_mixer(hc, hl, in_w, conv_w, conv_b, dt_bias, a_log, d_skip, norm_w, out_w, ctx_out):
    f32 = jnp.float32
    gn = M2_GROUPS * M2_STATE

    def project(h):
        b, n, _ = h.shape
        z, xbc, dt = jnp.split(h @ in_w, [M2_INNER, M2_INNER + M2_CONV_CH], axis=-1)
        xbc = jax.nn.silu(dwconv(xbc, conv_w, conv_b))
        xs, bm, cm = jnp.split(xbc, [M2_INNER, M2_INNER + gn], axis=-1)
        return (z,
                xs.reshape(b, n, M2_GROUPS, M2_HPG, M2_HEADDIM).astype(f32),
                bm.reshape(b, n, M2_GROUPS, M2_STATE).astype(f32),
                cm.reshape(b, n, M2_GROUPS, M2_STATE).astype(f32),
                dt.reshape(b, n, 2, M2_GROUPS, M2_HPG).astype(f32))

    pc = project(hc)
    pl = project(hl)

    def make_scan(dirn):
        a = -jnp.exp(a_log[dirn].astype(f32)).reshape(M2_GROUPS, M2_HPG)
        bias = dt_bias[dirn].astype(f32).reshape(M2_GROUPS, M2_HPG)

        def scan(args, h0):
            xs, bm, cm, dt = args
            dt = jax.nn.softplus(dt[:, :, dirn] + bias)
            return ssd_chunked(xs * dt[..., None], dt * a, bm, cm, h0)
        return scan

    h0 = jnp.zeros((hc.shape[0], M2_GROUPS, M2_HPG, M2_HEADDIM, M2_STATE), f32)
    yc, yl = prefix_bidir(make_scan(0), make_scan(1), pc[1:], pl[1:], h0)
    dsk = d_skip.astype(f32).reshape(M2_GROUPS, M2_HPG, 1)

    def finish(y, xs, z):
        b, n = z.shape[:2]
        y = (y + dsk * xs).reshape(b, n, M2_INNER).astype(z.dtype)
        return rms_norm(y * jax.nn.silu(z), norm_w) @ out_w

    y_c = finish(yc, pc[1], pc[0]) if ctx_out else None
    return y_c, finish(yl, pl[1], pl[0])


def rglru_mixer(hc, hl, in_w, conv_w, conv_b, gate_w, gate_b, a_param, out_w, ctx_out):
    f32 = jnp.float32

    def project(h):
        y, xr = jnp.split(h @ in_w, 2, axis=-1)
        return jax.nn.gelu(y), dwconv(xr, conv_w, conv_b)

    gy_c, xc = project(hc)
    gy_l, xl = project(hl)

    def make_scan(dirn):
        gw, gb = gate_w[dirn], gate_b[dirn]
        log_base = -LRU_C * jax.nn.softplus(-a_param[dirn].astype(f32))

        def scan(args, h0):
            (xs,) = args
            b, n, _ = xs.shape
            g = jnp.einsum('blnj,nji->blni', xs.reshape(b, n, LRU_BLOCKS, LRU_BLOCK), gw) + gb
            g = jax.nn.sigmoid(g.astype(f32))
            r = g[..., :LRU_BLOCK].reshape(b, n, LRU_WIDTH)
            i_g = g[..., LRU_BLOCK:].reshape(b, n, LRU_WIDTH)
            log_a = r * log_base
            mult = jnp.sqrt(jnp.maximum(-jnp.expm1(2.0 * log_a), 0.0))
            return linear_scan(jnp.exp(log_a), xs.astype(f32) * i_g * mult, h0)
        return scan

    h0 = jnp.zeros((hc.shape[0], LRU_WIDTH), f32)
    rc, rl = prefix_bidir(make_scan(0), make_scan(1), (xc,), (xl,), h0)
    y_c = (gy_c * rc.astype(gy_c.dtype)) @ out_w if ctx_out else None
    return y_c, (gy_l * rl.astype(gy_l.dtype)) @ out_w


def s5_mixer(hc, hl, lam_re, lam_im, log_step, b_re, b_im, c_re, c_im, d_skip, glu_w, glu_b, ctx_out):
    f32 = jnp.float32
    br, bi = b_re.astype(f32), b_im.astype(f32)

    def make_scan(dirn):
        lr = jnp.minimum(lam_re[dirn].astype(f32), -1e-4)
        li = lam_im[dirn].astype(f32)
        step = jnp.exp(log_step[dirn].astype(f32))[:, None]
        mag = jnp.exp(lr * step)
        abr, abi = mag * jnp.cos(li * step), mag * jnp.sin(li * step)
        den = lr * lr + li * li
        zr = ((abr - 1.0) * lr + abi * li) / den
        zi = (abi * lr - (abr - 1.0) * li) / den
        bbr = zr[..., None] * br - zi[..., None] * bi
        bbi = zr[..., None] * bi + zi[..., None] * br
        cr, ci = c_re[dirn].astype(f32), c_im[dirn].astype(f32)

        def scan(args, h0):
            (u,) = args
            n = u.shape[1]
            ur = jnp.einsum('blgj,gpj->blgp', u, bbr)
            ui = jnp.einsum('blgj,gpj->blgp', u, bbi)
            ar = jnp.broadcast_to(abr, (1, n) + abr.shape)
            ai = jnp.broadcast_to(abi, (1, n) + abi.shape)
            (hr, hi), state = complex_linear_scan(ar, ai, ur, ui, h0)
            y = jnp.einsum('blgp,gjp->blgj', hr, cr) - jnp.einsum('blgp,gjp->blgj', hi, ci)
            return y, state
        return scan

    b = hc.shape[0]
    uc = hc.astype(f32).reshape(b, hc.shape[1], S5_GROUPS, S5_GROUP)
    ul = hl.astype(f32).reshape(b, hl.shape[1], S5_GROUPS, S5_GROUP)
    h0 = (jnp.zeros((b, S5_GROUPS, S5_STATE), f32), jnp.zeros((b, S5_GROUPS, S5_STATE), f32))
    yc, yl = prefix_bidir(make_scan(0), make_scan(1), (uc,), (ul,), h0)
    dsk = d_skip.astype(f32).reshape(S5_GROUPS, S5_GROUP)

    def finish(y, u):
        bb, n = u.shape[:2]
        g = jax.nn.gelu((y + dsk * u).reshape(bb, n, D_MODEL)).astype(hc.dtype)
        a, gate = jnp.split(g @ glu_w + glu_b, 2, axis=-1)
        return a * jax.nn.sigmoid(gate)

    y_c = finish(yc, uc) if ctx_out else None
    return y_c, finish(yl, ul)


def mla_mixer(hc, hl, cos, sin, in_w, q_norm_w, kv_norm_w, qb_w, kvb_w, out_w, ctx_out):
    def project(h):
        b, n, _ = h.shape
        q_lat, kv_lat, k_rope = jnp.split(h @ in_w, [MLA_Q_RANK, MLA_Q_RANK + MLA_KV_RANK], axis=-1)
        q = (rms_norm(q_lat, q_norm_w) @ qb_w).reshape(b, n, MLA_HEADS, MLA_NOPE + MLA_ROPE)
        kv = (rms_norm(kv_lat, kv_norm_w) @ kvb_w).reshape(b, n, MLA_HEADS, MLA_NOPE + MLA_V)
        return q[..., :MLA_NOPE], q[..., MLA_NOPE:], kv[..., :MLA_NOPE], kv[..., MLA_NOPE:], k_rope

    def keys(k_nope, k_rope):
        shared = jnp.broadcast_to(k_rope[:, :, None, :], k_nope.shape[:3] + (MLA_ROPE,))
        return jnp.concatenate([k_nope, shared], axis=-1)

    def merge(o):
        b, n = o.shape[:2]
        return o.reshape(b, n, MLA_HEADS * MLA_V) @ out_w

    qn_c, qr_c, kn_c, v_c, kr_c = project(hc)
    qn_l, qr_l, kn_l, v_l, kr_l = project(hl)
    k_c = keys(kn_c, kr_c)
    k_l = keys(kn_l, rope2d(kr_l, cos, sin))
    q_l = jnp.concatenate([qn_l, rope2d(qr_l, cos[:, None], sin[:, None])], axis=-1)
    o_l = attend_blocks(q_l, jnp.concatenate([k_c, k_l], axis=1), jnp.concatenate([v_c, v_l], axis=1))
    y_c = merge(attend_blocks(jnp.concatenate([qn_c, qr_c], axis=-1), k_c, v_c)) if ctx_out else None
    return y_c, merge(o_l)


def setup_inputs(seed: int = 0) -> dict:
    key = jax.random.key(seed)
    ks = iter(jax.random.split(key, 64))
    f32 = jnp.float32

    def nrm(shape, scale):
        return jax.random.normal(next(ks), shape, f32) * scale

    def gain(shape):
        return 1.0 + nrm(shape, 0.02)

    def unif(shape, lo, hi):
        return jax.random.uniform(next(ks), shape, f32, lo, hi)

    na, nb, nc, nd = N_SSD_LAYERS, N_LRU_LAYERS, N_S5_LAYERS, N_MLA_LAYERS
    dt0 = jnp.exp(unif((na, 2, M2_HEADS), math.log(1e-3), math.log(1e-1)))
    lru_base = unif((nb, 2, LRU_WIDTH), 0.9, 0.999) ** (1.0 / LRU_C)
    return {
        'x': nrm((BATCH, SEQ, D_MODEL), 1.0),
        'c': nrm((BATCH, D_MODEL), 1.0),
        'ctx': nrm((BATCH, CTX_LEN, D_MODEL), 1.0),
        'c_ctx': nrm((D_MODEL,), 1.0),
        'ada_w': nrm((DEPTH, D_MODEL, 6 * D_MODEL), 0.5 * D_MODEL ** -0.5),
        'ada_b': nrm((DEPTH, 6 * D_MODEL), 0.02),
        'norm1_w': gain((DEPTH, D_MODEL)),
        'norm2_w': gain((DEPTH, D_MODEL)),
        'ffn_w13': nrm((DEPTH, D_MODEL, 2 * FFN_HIDDEN), D_MODEL ** -0.5),
        'ffn_w2': nrm((DEPTH, FFN_HIDDEN, D_MODEL), FFN_HIDDEN ** -0.5),
        'm2_in_w': nrm((na, D_MODEL, M2_PROJ), D_MODEL ** -0.5),
        'm2_conv_w': nrm((na, CONV_WIDTH, M2_CONV_CH), CONV_WIDTH ** -0.5),
        'm2_conv_b': nrm((na, M2_CONV_CH), 0.02),
        'm2_dt_bias': dt0 + jnp.log(-jnp.expm1(-dt0)),
        'm2_a_log': jnp.log(unif((na, 2, M2_HEADS), 1.0, 16.0)),
        'm2_d': gain((na, M2_HEADS)),
        'm2_norm_w': gain((na, M2_INNER)),
        'm2_out_w': nrm((na, M2_INNER, D_MODEL), M2_INNER ** -0.5),
        'lru_in_w': nrm((nb, D_MODEL, 2 * LRU_WIDTH), D_MODEL ** -0.5),
        'lru_conv_w': nrm((nb, CONV_WIDTH, LRU_WIDTH), CONV_WIDTH ** -0.5),
        'lru_conv_b': nrm((nb, LRU_WIDTH), 0.02),
        'lru_gate_w': nrm((nb, 2, LRU_BLOCKS, LRU_BLOCK, 2 * LRU_BLOCK), LRU_BLOCK ** -0.5),
        'lru_gate_b': nrm((nb, 2, LRU_BLOCKS, 2 * LRU_BLOCK), 0.02),
        'lru_a_param': jnp.log(lru_base) - jnp.log1p(-lru_base),
        'lru_out_w': nrm((nb, LRU_WIDTH, D_MODEL), LRU_WIDTH ** -0.5),
        's5_lambda_re': -0.5 + nrm((nc, 2, S5_GROUPS, S5_STATE), 0.01),
        's5_lambda_im': jnp.pi * jnp.arange(S5_STATE, dtype=f32) + nrm((nc, 2, S5_GROUPS, S5_STATE), 0.01),
        's5_log_step': unif((nc, 2, S5_GROUPS), math.log(1e-3), math.log(1e-1)),
        's5_b_re': nrm((nc, S5_GROUPS, S5_STATE, S5_GROUP), (2 * S5_GROUP) ** -0.5),
        's5_b_im': nrm((nc, S5_GROUPS, S5_STATE, S5_GROUP), (2 * S5_GROUP) ** -0.5),
        's5_c_re': nrm((nc, 2, S5_GROUPS, S5_GROUP, S5_STATE), (2 * S5_STATE) ** -0.5),
        's5_c_im': nrm((nc, 2, S5_GROUPS, S5_GROUP, S5_STATE), (2 * S5_STATE) ** -0.5),
        's5_d': nrm((nc, D_MODEL), 1.0),
        's5_glu_w': nrm((nc, D_MODEL, 2 * D_MODEL), D_MODEL ** -0.5),
        's5_glu_b': nrm((nc, 2 * D_MODEL), 0.02),
        'mla_in_w': nrm((nd, D_MODEL, MLA_IN), D_MODEL ** -0.5),
        'mla_q_norm_w': gain((nd, MLA_Q_RANK)),
        'mla_kv_norm_w': gain((nd, MLA_KV_RANK)),
        'mla_qb_w': nrm((nd, MLA_Q_RANK, MLA_HEADS * (MLA_NOPE + MLA_ROPE)), MLA_Q_RANK ** -0.5),
        'mla_kvb_w': nrm((nd, MLA_KV_RANK, MLA_HEADS * (MLA_NOPE + MLA_V)), MLA_KV_RANK ** -0.5),
        'mla_out_w': nrm((nd, MLA_HEADS * MLA_V, D_MODEL), (MLA_HEADS * MLA_V) ** -0.5),
        'final_norm_w': gain((D_MODEL,)),
    }


def reference(x, c, ctx, c_ctx, ada_w, ada_b, norm1_w, norm2_w, ffn_w13, ffn_w2,
              m2_in_w, m2_conv_w, m2_conv_b, m2_dt_bias, m2_a_log, m2_d, m2_norm_w, m2_out_w,
              lru_in_w, lru_conv_w, lru_conv_b, lru_gate_w, lru_gate_b, lru_a_param, lru_out_w,
              s5_lambda_re, s5_lambda_im, s5_log_step, s5_b_re, s5_b_im, s5_c_re, s5_c_im,
              s5_d, s5_glu_w, s5_glu_b,
              mla_in_w, mla_q_norm_w, mla_kv_norm_w, mla_qb_w, mla_kvb_w, mla_out_w,
              final_norm_w):
    n = x.shape[1]
    rows = n // GRID_W
    row = jnp.repeat(jnp.arange(rows, dtype=jnp.float32), GRID_W)
    col = jnp.tile(jnp.arange(GRID_W, dtype=jnp.float32), rows)
    inv_freq = ROPE_BASE ** (-jnp.arange(ROPE_FREQ, dtype=jnp.float32) / ROPE_FREQ)
    ang = jnp.stack([row[:, None] * inv_freq, col[:, None] * inv_freq], axis=1)
    cos, sin = jnp.cos(ang).astype(x.dtype), jnp.sin(ang).astype(x.dtype)

    cx = ctx
    for i in range(DEPTH):
        kind, j = i % N_MIXERS, i // N_MIXERS
        ctx_out = i < DEPTH - 1
        mod_l = jnp.split(jax.nn.silu(c) @ ada_w[i] + ada_b[i], 6, axis=-1)
        sh1, sc1, g1, sh2, sc2, g2 = [m[:, None, :] for m in mod_l]
        csh1, csc1, cg1, csh2, csc2, cg2 = jnp.split(jax.nn.silu(c_ctx) @ ada_w[i] + ada_b[i], 6, axis=-1)
        hl = rms_norm(x, norm1_w[i]) * (1.0 + sc1) + sh1
        hc = rms_norm(cx, norm1_w[i]) * (1.0 + csc1) + csh1
        if kind == 0:
            yc, yl = mamba2_mixer(hc, hl, m2_in_w[j], m2_conv_w[j], m2_conv_b[j], m2_dt_bias[j],
                                  m2_a_log[j], m2_d[j], m2_norm_w[j], m2_out_w[j], ctx_out)
        elif kind == 1:
            yc, yl = rglru_mixer(hc, hl, lru_in_w[j], lru_conv_w[j], lru_conv_b[j], lru_gate_w[j],
                                 lru_gate_b[j], lru_a_param[j], lru_out_w[j], ctx_out)
        elif kind == 2:
            yc, yl = s5_mixer(hc, hl, s5_lambda_re[j], s5_lambda_im[j], s5_log_step[j], s5_b_re[j],
                              s5_b_im[j], s5_c_re[j], s5_c_im[j], s5_d[j], s5_glu_w[j], s5_glu_b[j], ctx_out)
        else:
            yc, yl = mla_mixer(hc, hl, cos, sin, mla_in_w[j], mla_q_norm_w[j], mla_kv_norm_w[j],
                               mla_qb_w[j], mla_kvb_w[j], mla_out_w[j], ctx_out)
        x = x + g1 * yl
        x = x + g2 * swiglu(rms_norm(x, norm2_w[i]) * (1.0 + sc2) + sh2, ffn_w13[i], ffn_w2[i])
        if ctx_out:
            cx = cx + cg1 * yc
            cx = cx + cg2 * swiglu(rms_norm(cx, norm2_w[i]) * (1.0 + csc2) + csh2, ffn_w13[i], ffn_w2[i])
    return rms_norm(x, final_norm_w)
```

```python
import functools
import math

import jax
import jax.numpy as jnp
from jax import lax
from jax.experimental import pallas as pl
from jax.experimental.pallas import tpu as pltpu

F32 = jnp.float32
BF16 = jnp.bfloat16

LANES = 128
SUBLANES = 8
VMEM_BUDGET_BYTES = 56 * 2**20

GRID_W = 64
N_MIXERS = 4
NORM_EPS = 1e-6
CONV_WIDTH = 4
CONV_PAD_LEFT = 2
M2_HEADDIM = 64
M2_GROUPS = 4
M2_STATE = 128
SSD_CHUNK = 128
LRU_BLOCK = 128
LRU_C = 8.0
S5_GROUP = 16
S5_STATE = 64
MLA_HEADS = 16
MLA_NOPE = 64
MLA_ROPE = 32
MLA_V = 64
ROPE_FREQ = MLA_ROPE // 4
ROPE_BASE = 10000.0
HEAD_PAD = 128


def _pick(n, cands):
    for c in cands:
        if n % c == 0:
            return c
    raise ValueError(f"no tile in {cands} divides {n}")


def _params(sem, est_bytes):
    limit = int(min(max(2 * est_bytes, 32 * 2**20), VMEM_BUDGET_BYTES))
    return pltpu.CompilerParams(dimension_semantics=sem, vmem_limit_bytes=limit)


def _nbytes(shape, dtype):
    return math.prod(s for s in shape if s is not None) * jnp.dtype(dtype).itemsize


def _sigmoid(x):
    return 1.0 / (1.0 + jnp.exp(-x))


def _silu(x):
    return x * _sigmoid(x)


def _softplus(x):
    return jnp.maximum(x, 0.0) + jnp.log1p(jnp.exp(-jnp.abs(x)))


def _gelu_tanh(x):
    return 0.5 * x * (1.0 + jnp.tanh(math.sqrt(2.0 / math.pi) * (x + 0.044715 * (x * x * x))))


def _rms(x, w):
    return x * lax.rsqrt(jnp.mean(x * x, axis=-1, keepdims=True) + NORM_EPS) * w


def _norm_mod(x, w, sc, sh):
    return _rms(x, w) * (1.0 + sc) + sh


def _split3(q):
    q1 = q.astype(BF16)
    r1 = q - q1.astype(F32)
    q2 = r1.astype(BF16)
    q3 = (r1 - q2.astype(F32)).astype(BF16)
    return q1, q2, q3


def _select_cols(q, e):
    return sum(jnp.dot(p, e, preferred_element_type=F32) for p in _split3(q))


def _select_rows(t, q):
    return sum(jnp.dot(t, p, preferred_element_type=F32) for p in _split3(q))


def _fused_matmul(pro_ins, weights, epi_ins, prologue, epilogue, *, rows, tm, k, n, tn, out_dtypes):
    n_p, n_w, n_e, n_o = len(pro_ins), len(weights), len(epi_ins), len(out_dtypes)

    def body(*refs):
        p = refs[:n_p]
        w = refs[n_p:n_p + n_w]
        e = refs[n_p + n_w:n_p + n_w + n_e]
        o = refs[n_p + n_w + n_e:n_p + n_w + n_e + n_o]
        a_scr = refs[-1]

        @pl.when(pl.program_id(1) == 0)
        def _():
            a_scr[...] = prologue(*[r[...] for r in p]).astype(BF16)

        a = a_scr[...]
        accs = [jnp.dot(a, wr[...], preferred_element_type=F32) for wr in w]
        outs = epilogue(accs, *[r[...] for r in e])
        for o_ref, val in zip(o, outs):
            o_ref[...] = val.astype(o_ref.dtype)

    in_specs = ([pl.BlockSpec(bs, im) for (_, bs, im) in pro_ins]
                + [pl.BlockSpec((k, tn), lambda i, j: (0, j)) for _ in weights]
                + [pl.BlockSpec(bs, im) for (_, bs, im) in epi_ins])
    out_specs = [pl.BlockSpec((tm, tn), lambda i, j: (i, j)) for _ in out_dtypes]
    est = (sum(_nbytes(bs, a.dtype) for (a, bs, _) in pro_ins + epi_ins) * 2
           + n_w * k * tn * 2 * 2 + sum(tm * tn * jnp.dtype(d).itemsize for d in out_dtypes) * 2
           + tm * k * 2 + (n_w + 2) * tm * tn * 4 + tm * k * 8)
    outs = pl.pallas_call(
        body,
        out_shape=[jax.ShapeDtypeStruct((rows, n), d) for d in out_dtypes],
        grid=(rows // tm, n // tn),
        in_specs=in_specs,
        out_specs=out_specs,
        scratch_shapes=[pltpu.VMEM((tm, k), BF16)],
        compiler_params=_params(("parallel", "arbitrary"), est),
    )(*[a for (a, _, _) in pro_ins], *weights, *[a for (a, _, _) in epi_ins])
    return outs


def _rows_in(a, tm, width, col_block=0):
    return (a, (tm, width), lambda i, j: (i, col_block))


def _batch_in(a, tpb, width):
    if a.shape[0] == 1:
        return (a, (None, 1, width), lambda i, j: (0, 0, 0))
    return (a, (None, 1, width), lambda i, j: (i // tpb, 0, 0))


def _const_in(a, width):
    return (a, (1, width), lambda i, j: (0, 0))


def _tile_in(a, tm, tn):
    return (a, (tm, tn), lambda i, j: (i, j))


def _batch_tile_in(a, tpb, tn):
    if a.shape[0] == 1:
        return (a, (None, 1, tn), lambda i, j: (0, 0, j))
    return (a, (None, 1, tn), lambda i, j: (i // tpb, 0, j))


def _const_tile_in(a, tn):
    return (a, (1, tn), lambda i, j: (0, j))


def _norm_mod_rows(x, w, sc, sh, *, tm, tpb):
    rows, d = x.shape

    def body(x_ref, w_ref, sc_ref, sh_ref, o_ref):
        o_ref[...] = _norm_mod(x_ref[...], w_ref[...], sc_ref[...], sh_ref[...])

    def bidx(a):
        if a.shape[0] == 1:
            return lambda i: (0, 0, 0)
        return lambda i: (i // tpb, 0, 0)

    return pl.pallas_call(
        body,
        out_shape=jax.ShapeDtypeStruct((rows, d), F32),
        grid=(rows // tm,),
        in_specs=[pl.BlockSpec((tm, d), lambda i: (i, 0)),
                  pl.BlockSpec((1, d), lambda i: (0, 0)),
                  pl.BlockSpec((None, 1, d), bidx(sc)),
                  pl.BlockSpec((None, 1, d), bidx(sh))],
        out_specs=pl.BlockSpec((tm, d), lambda i: (i, 0)),
        compiler_params=_params(("parallel",), 6 * tm * d * 4),
    )(x, w, sc, sh)


def _dwconv(x, w, b, *, nb, seq, col_off, channels, act):
    cb = channels // LANES
    rc = _pick(seq, (512, 256, 128))
    pad = SUBLANES

    def body(x_ref, w_ref, b_ref, o_ref, p_scr):
        p_scr[0:pad, :] = jnp.zeros((pad, LANES), F32)
        p_scr[pad + seq:pad + seq + pad, :] = jnp.zeros((pad, LANES), F32)
        p_scr[pad:pad + seq, :] = x_ref[...]
        wv = w_ref[...]
        bv = b_ref[...]

        def step(i, carry):
            r0 = pl.multiple_of(i * rc, rc)
            y = bv
            for t in range(CONV_WIDTH):
                y = y + wv[t:t + 1, :] * p_scr[pl.ds(r0 + pad - CONV_PAD_LEFT + t, rc), :]
            if act:
                y = _silu(y)
            o_ref[pl.ds(r0, rc), :] = y
            return carry

        lax.fori_loop(0, seq // rc, step, 0)

    return pl.pallas_call(
        body,
        out_shape=jax.ShapeDtypeStruct((nb * seq, channels), F32),
        grid=(nb, cb),
        in_specs=[pl.BlockSpec((seq, LANES), lambda bi, ci: (bi, col_off + ci)),
                  pl.BlockSpec((CONV_WIDTH, LANES), lambda bi, ci: (0, ci)),
                  pl.BlockSpec((1, LANES), lambda bi, ci: (0, ci))],
        out_specs=pl.BlockSpec((seq, LANES), lambda bi, ci: (bi, ci)),
        scratch_shapes=[pltpu.VMEM((seq + 2 * pad, LANES), F32)],
        compiler_params=_params(("parallel", "parallel"), 5 * seq * LANES * 4),
    )(x, w, b)


def _ssd_direction(xbc, dt, par, dskip, h0, yprev, consts, *, nb, seq, reverse):
    tri, e_head, e_wide = consts
    q = SSD_CHUNK
    nc = seq // q
    inner = xbc.shape[1] - 2 * M2_GROUPS * M2_STATE
    gw = inner // M2_GROUPS
    heads = inner // M2_HEADDIM
    hpg = heads // M2_GROUPS
    has_prev = yprev is not None
    has_skip = dskip is not None

    def body(*refs):
        it = iter(refs)
        xbc_ref, dt_ref, par_ref = next(it), next(it), next(it)
        dsk_ref = next(it) if has_skip else None
        h0_ref = next(it)
        yp_ref = next(it) if has_prev else None
        tri_ref, eh_ref, ew_ref = next(it), next(it), next(it)
        y_ref, hout_ref, h_scr = next(it), next(it), next(it)
        c = pl.program_id(1)

        @pl.when(c == 0)
        def _():
            h_scr[...] = h0_ref[...]

        dtv = _softplus(dt_ref[...] + par_ref[0:1, :])
        da = dtv * (-jnp.exp(par_ref[1:2, :]))
        acs = _select_rows(tri_ref[...], da)
        acs_t = acs.T
        last = acs[0:1, :] if reverse else acs[q - 1:q, :]
        eh = eh_ref[...]
        dt_e = _select_cols(dtv, eh)
        dout_e = _select_cols(jnp.exp(last - acs), eh)
        eacs_e = _select_cols(jnp.exp(acs), eh)
        cdec_e = _select_cols(jnp.broadcast_to(jnp.exp(last), (SUBLANES, LANES)), eh)[0:1, :]
        acol = _select_cols(acs, ew_ref[...])
        xs = xbc_ref[:, 0:inner]
        x_dt = xs * dt_e
        x_b = x_dt.astype(BF16)
        x_d = (x_dt * dout_e).astype(BF16)
        row = lax.broadcasted_iota(jnp.int32, (q, q), 0)
        col = lax.broadcasted_iota(jnp.int32, (q, q), 1)
        mask = (col >= row) if reverse else (col <= row)
        lane = lax.broadcasted_iota(jnp.int32, (q, LANES), 1)
        for g in range(M2_GROUPS):
            bm = xbc_ref[:, inner + g * M2_STATE:inner + (g + 1) * M2_STATE]
            cm = xbc_ref[:, inner + (M2_GROUPS + g) * M2_STATE:inner + (M2_GROUPS + g + 1) * M2_STATE]
            cm_b = cm.astype(BF16)
            scores = lax.dot_general(cm_b, bm.astype(BF16), (((1,), (1,)), ((), ())),
                                     preferred_element_type=F32)
            h_t = h_scr[g]
            y_off = jnp.dot(cm_b, h_t.astype(BF16), preferred_element_type=F32) * eacs_e[:, g * gw:(g + 1) * gw]
            s_t = jnp.dot(bm.T.astype(BF16), x_d[:, g * gw:(g + 1) * gw], preferred_element_type=F32)
            h_scr[g] = h_t * cdec_e[:, g * gw:(g + 1) * gw] + s_t
            for pr in range(hpg // 2):
                k0 = g * hpg + 2 * pr
                c0 = k0 * M2_HEADDIM
                ys = []
                for k in (k0, k0 + 1):
                    diff = acol[:, k * LANES:(k + 1) * LANES] - acs_t[k:k + 1, :]
                    decay = jnp.exp(jnp.where(mask, diff, -jnp.inf))
                    m = (scores * decay).astype(BF16)
                    ys.append(jnp.dot(m, x_b[:, c0:c0 + LANES], preferred_element_type=F32))
                y = jnp.where(lane < M2_HEADDIM, ys[0], ys[1]) + y_off[:, 2 * pr * M2_HEADDIM:2 * pr * M2_HEADDIM + LANES]
                if has_skip:
                    y = y + dsk_ref[:, c0:c0 + LANES] * xs[:, c0:c0 + LANES]
                if has_prev:
                    y = y + yp_ref[:, c0:c0 + LANES]
                y_ref[:, c0:c0 + LANES] = y

        @pl.when(c == nc - 1)
        def _():
            hout_ref[...] = h_scr[...]

    def rb(bi, ci):
        return bi * nc + ((nc - 1 - ci) if reverse else ci)

    width = xbc.shape[1]
    st_shape = (M2_GROUPS, M2_STATE, gw)
    ins = [xbc, dt, par]
    in_specs = [pl.BlockSpec((q, width), lambda bi, ci: (rb(bi, ci), 0)),
                pl.BlockSpec((q, LANES), lambda bi, ci: (rb(bi, ci), 0)),
                pl.BlockSpec((SUBLANES, LANES), lambda bi, ci: (0, 0))]
    if has_skip:
        ins.append(dskip)
        in_specs.append(pl.BlockSpec((1, inner), lambda bi, ci: (0, 0)))
    ins.append(h0)
    in_specs.append(pl.BlockSpec((None,) + st_shape, lambda bi, ci: (bi, 0, 0, 0)))
    if has_prev:
        ins.append(yprev)
        in_specs.append(pl.BlockSpec((q, inner), lambda bi, ci: (rb(bi, ci), 0)))
    ins += [tri, e_head, e_wide]
    in_specs += [pl.BlockSpec(tri.shape, lambda bi, ci: (0, 0)),
                 pl.BlockSpec(e_head.shape, lambda bi, ci: (0, 0)),
                 pl.BlockSpec(e_wide.shape, lambda bi, ci: (0, 0))]
    est = (q * width * 4 * 2 + q * inner * 4 * 4 + 3 * math.prod(st_shape) * 4 * 2
           + (e_head.size + e_wide.size) * 2 * 2 + 12 * q * inner * 4 + q * heads * LANES * 4)
    y, h_last = pl.pallas_call(
        body,
        out_shape=[jax.ShapeDtypeStruct((nb * seq, inner), F32),
                   jax.ShapeDtypeStruct((nb,) + st_shape, F32)],
        grid=(nb, nc),
        in_specs=in_specs,
        out_specs=[pl.BlockSpec((q, inner), lambda bi, ci: (rb(bi, ci), 0)),
                   pl.BlockSpec((None,) + st_shape, lambda bi, ci: (bi, 0, 0, 0))],
        scratch_shapes=[pltpu.VMEM(st_shape, F32)],
        compiler_params=_params(("parallel", "arbitrary"), est),
    )(*ins)
    return y, h_last


def _ssd_consts(inner, reverse):
    heads = inner // M2_HEADDIM
    r = jnp.arange(SSD_CHUNK)
    tri = (r[None, :] >= r[:, None]) if reverse else (r[None, :] <= r[:, None])
    hk = jnp.arange(LANES)[:, None]
    e_head = (hk == (jnp.arange(inner)[None, :] // M2_HEADDIM))
    e_wide = (hk == (jnp.arange(heads * LANES)[None, :] // LANES))
    return tri.astype(BF16), e_head.astype(BF16), e_wide.astype(BF16)


def _tile_scan(a, u, reverse):
    row = lax.broadcasted_iota(jnp.int32, a.shape, 0)
    for d in (1, 2, 4):
        shift = (SUBLANES - d) if reverse else d
        valid = (row < SUBLANES - d) if reverse else (row >= d)
        a_s = jnp.where(valid, pltpu.roll(a, shift, 0), 1.0)
        u_s = jnp.where(valid, pltpu.roll(u, shift, 0), 0.0)
        u = a * u_s + u
        a = a * a_s
    return a, u


def _rglru_direction(xc, gate_w, gate_b, a_param, h0, rprev, *, nb, seq, reverse):
    width = xc.shape[1]
    nblk = width // LRU_BLOCK
    tb = _pick(seq, (256, 128))
    nt = seq // tb
    has_prev = rprev is not None

    def body(*refs):
        it = iter(refs)
        x_ref, gw_ref, gb_ref, ap_ref, h0_ref = next(it), next(it), next(it), next(it), next(it)
        rp_ref = next(it) if has_prev else None
        r_ref, hout_ref, a_scr, u_scr, h_scr = next(it), next(it), next(it), next(it), next(it)
        tstep = pl.program_id(1)

        @pl.when(tstep == 0)
        def _():
            h_scr[...] = jnp.broadcast_to(h0_ref[...], (SUBLANES, width))

        log_base = -LRU_C * _softplus(-ap_ref[...])
        for nbk in range(nblk):
            cs = slice(nbk * LRU_BLOCK, (nbk + 1) * LRU_BLOCK)
            xb = x_ref[:, cs]
            g = jnp.dot(xb.astype(BF16), gw_ref[nbk], preferred_element_type=F32) + gb_ref[nbk]
            g = _sigmoid(g)
            log_a = g[:, :LRU_BLOCK] * log_base[:, cs]
            a = jnp.exp(log_a)
            mult = jnp.sqrt(jnp.maximum(1.0 - a * a, 0.0))
            a_scr[:, cs] = a
            u_scr[:, cs] = xb * g[:, LRU_BLOCK:] * mult

        def step(i, h):
            grp = (tb // SUBLANES - 1 - i) if reverse else i
            r0 = pl.multiple_of(grp * SUBLANES, SUBLANES)
            a_c, h_loc = _tile_scan(a_scr[pl.ds(r0, SUBLANES), :], u_scr[pl.ds(r0, SUBLANES), :], reverse)
            h_new = a_c * h + h_loc
            out = h_new
            if has_prev:
                out = out + rp_ref[pl.ds(r0, SUBLANES), :]
            r_ref[pl.ds(r0, SUBLANES), :] = out
            edge = h_new[0:1, :] if reverse else h_new[SUBLANES - 1:SUBLANES, :]
            return jnp.broadcast_to(edge, (SUBLANES, width))

        h_fin = lax.fori_loop(0, tb // SUBLANES, step, h_scr[...])
        h_scr[...] = h_fin

        @pl.when(tstep == nt - 1)
        def _():
            hout_ref[...] = h_fin[0:1, :]

    def rb(bi, ti):
        return bi * nt + ((nt - 1 - ti) if reverse else ti)

    ins = [xc, gate_w, gate_b, a_param, h0]
    in_specs = [pl.BlockSpec((tb, width), lambda bi, ti: (rb(bi, ti), 0)),
                pl.BlockSpec(gate_w.shape, lambda bi, ti: (0, 0, 0)),
                pl.BlockSpec(gate_b.shape, lambda bi, ti: (0, 0, 0)),
                pl.BlockSpec((1, width), lambda bi, ti: (0, 0)),
                pl.BlockSpec((None, 1, width), lambda bi, ti: (bi, 0, 0))]
    if has_prev:
        ins.append(rprev)
        in_specs.append(pl.BlockSpec((tb, width), lambda bi, ti: (rb(bi, ti), 0)))
    est = 10 * tb * width * 4 + gate_w.size * 2 * 2
    r, h_last = pl.pallas_call(
        body,
        out_shape=[jax.ShapeDtypeStruct((nb * seq, width), F32),
                   jax.ShapeDtypeStruct((nb, 1, width), F32)],
        grid=(nb, nt),
        in_specs=in_specs,
        out_specs=[pl.BlockSpec((tb, width), lambda bi, ti: (rb(bi, ti), 0)),
                   pl.BlockSpec((None, 1, width), lambda bi, ti: (bi, 0, 0))],
        scratch_shapes=[pltpu.VMEM((tb, width), F32), pltpu.VMEM((tb, width), F32),
                        pltpu.VMEM((SUBLANES, width), F32)],
        compiler_params=_params(("parallel", "arbitrary"), est),
    )(*ins)
    return r, h_last


def _s5_direction(u, bb_re, bb_im, c_re, c_im, pw, h0, yprev, *, nb, seq, reverse):
    d_model = u.shape[1]
    lbs = d_model // LANES
    sw = bb_re.shape[2]
    tb = _pick(seq, (512, 256, 128))
    nt = seq // tb
    has_prev = yprev is not None

    def body(*refs):
        it = iter(refs)
        u_ref, br_ref, bi_ref, cr_ref, ci_ref, pw_ref, h0_ref = (next(it) for _ in range(7))
        yp_ref = next(it) if has_prev else None
        y_ref, hout_ref, r_scr, i_scr, h_scr = next(it), next(it), next(it), next(it), next(it)
        tstep = pl.program_id(2)

        @pl.when(tstep == 0)
        def _():
            h_scr[0] = jnp.broadcast_to(h0_ref[0:1, :], (SUBLANES, sw))
            h_scr[1] = jnp.broadcast_to(h0_ref[1:2, :], (SUBLANES, sw))

        ub = u_ref[...].astype(BF16)
        r_scr[...] = jnp.dot(ub, br_ref[...], preferred_element_type=F32)
        i_scr[...] = jnp.dot(ub, bi_ref[...], preferred_element_type=F32)

        def step(i, carry):
            hr, hi = carry
            grp = (tb // SUBLANES - 1 - i) if reverse else i
            r0 = pl.multiple_of(grp * SUBLANES, SUBLANES)
            xr = r_scr[pl.ds(r0, SUBLANES), :]
            xi = i_scr[pl.ds(r0, SUBLANES), :]
            for n_step, dist in enumerate((1, 2, 4)):
                shift = (SUBLANES - dist) if reverse else dist
                pr = pw_ref[2 * n_step]
                pi = pw_ref[2 * n_step + 1]
                sr = pltpu.roll(xr, shift, 0)
                si = pltpu.roll(xi, shift, 0)
                xr, xi = xr + (pr * sr - pi * si), xi + (pr * si + pi * sr)
            ar = pw_ref[6]
            ai = pw_ref[7]
            nr = xr + (ar * hr - ai * hi)
            ni = xi + (ar * hi + ai * hr)
            r_scr[pl.ds(r0, SUBLANES), :] = nr
            i_scr[pl.ds(r0, SUBLANES), :] = ni
            e = 0 if reverse else SUBLANES - 1
            return (jnp.broadcast_to(nr[e:e + 1, :], (SUBLANES, sw)),
                    jnp.broadcast_to(ni[e:e + 1, :], (SUBLANES, sw)))

        hr, hi = lax.fori_loop(0, tb // SUBLANES, step, (h_scr[0], h_scr[1]))
        h_scr[0] = hr
        h_scr[1] = hi
        y = (jnp.dot(r_scr[...].astype(BF16), cr_ref[...], preferred_element_type=F32)
             - jnp.dot(i_scr[...].astype(BF16), ci_ref[...], preferred_element_type=F32))
        if has_prev:
            y = y + yp_ref[...]
        y_ref[...] = y

        @pl.when(tstep == nt - 1)
        def _():
            hout_ref[0:1, :] = hr[0:1, :]
            hout_ref[1:2, :] = hi[0:1, :]

    def rb(bi, ti):
        return bi * nt + ((nt - 1 - ti) if reverse else ti)

    ins = [u, bb_re, bb_im, c_re, c_im, pw, h0]
    in_specs = [pl.BlockSpec((tb, LANES), lambda bi, li, ti: (rb(bi, ti), li)),
                pl.BlockSpec((None, LANES, sw), lambda bi, li, ti: (li, 0, 0)),
                pl.BlockSpec((None, LANES, sw), lambda bi, li, ti: (li, 0, 0)),
                pl.BlockSpec((None, sw, LANES), lambda bi, li, ti: (li, 0, 0)),
                pl.BlockSpec((None, sw, LANES), lambda bi, li, ti: (li, 0, 0)),
                pl.BlockSpec((None, 8, SUBLANES, sw), lambda bi, li, ti: (li, 0, 0, 0)),
                pl.BlockSpec((None, None, 2, sw), lambda bi, li, ti: (bi, li, 0, 0))]
    if has_prev:
        ins.append(yprev)
        in_specs.append(pl.BlockSpec((tb, LANES), lambda bi, li, ti: (rb(bi, ti), li)))
    est = 6 * tb * sw * 4 + 6 * tb * LANES * 4 + 8 * LANES * sw * 2
    y, h_last = pl.pallas_call(
        body,
        out_shape=[jax.ShapeDtypeStruct((nb * seq, d_model), F32),
                   jax.ShapeDtypeStruct((nb, lbs, 2, sw), F32)],
        grid=(nb, lbs, nt),
        in_specs=in_specs,
        out_specs=[pl.BlockSpec((tb, LANES), lambda bi, li, ti: (rb(bi, ti), li)),
                   pl.BlockSpec((None, None, 2, sw), lambda bi, li, ti: (bi, li, 0, 0))],
        scratch_shapes=[pltpu.VMEM((tb, sw), F32), pltpu.VMEM((tb, sw), F32),
                        pltpu.VMEM((2, SUBLANES, sw), F32)],
        compiler_params=_params(("parallel", "parallel", "arbitrary"), est),
    )(*ins)
    return y, h_last


def _attention(q, k, v, *, nb, lq, lk, scale):
    heads = q.shape[1] // HEAD_PAD
    tq = _pick(lq, (1024, 512, 256, 128))
    tk = _pick(lk, (1408, 1024, 768, 512, 384, 256, 128))
    nq, nk = lq // tq, lk // tk

    def body(q_ref, k_ref, v_ref, o_ref, m_scr, l_scr, acc_scr):
        kk = pl.program_id(3)

        @pl.when(kk == 0)
        def _():
            m_scr[...] = jnp.full((tq, 1), -jnp.inf, F32)
            l_scr[...] = jnp.zeros((tq, 1), F32)
            acc_scr[...] = jnp.zeros((tq, HEAD_PAD), F32)

        s = lax.dot_general(q_ref[...], k_ref[...], (((1,), (1,)), ((), ())),
                            preferred_element_type=F32) * scale
        m_prev = m_scr[...]
        m_new = jnp.maximum(m_prev, jnp.max(s, axis=1, keepdims=True))
        alpha = jnp.exp(m_prev - m_new)
        p = jnp.exp(s - m_new)
        l_scr[...] = alpha * l_scr[...] + jnp.sum(p, axis=1, keepdims=True)
        acc_scr[...] = alpha * acc_scr[...] + jnp.dot(p.astype(BF16), v_ref[...], preferred_element_type=F32)
        m_scr[...] = m_new

        @pl.when(kk == nk - 1)
        def _():
            o_ref[...] = (acc_scr[...] / l_scr[...]).astype(o_ref.dtype)

    est = (tq * HEAD_PAD * 2 * 4 + tk * HEAD_PAD * 2 * 4 + 3 * tq * tk * 4 + 3 * tq * HEAD_PAD * 4)
    return pl.pallas_call(
        body,
        out_shape=jax.ShapeDtypeStruct(q.shape, BF16),
        grid=(nb, heads, nq, nk),
        in_specs=[pl.BlockSpec((tq, HEAD_PAD), lambda b, h, qi, ki: (b * nq + qi, h)),
                  pl.BlockSpec((tk, HEAD_PAD), lambda b, h, qi, ki: (b * nk + ki, h)),
                  pl.BlockSpec((tk, HEAD_PAD), lambda b, h, qi, ki: (b * nk + ki, h))],
        out_specs=pl.BlockSpec((tq, HEAD_PAD), lambda b, h, qi, ki: (b * nq + qi, h)),
        scratch_shapes=[pltpu.VMEM((tq, 1), F32), pltpu.VMEM((tq, 1), F32),
                        pltpu.VMEM((tq, HEAD_PAD), F32)],
        compiler_params=_params(("parallel", "parallel", "parallel", "arbitrary"), est),
    )(q, k, v)


class _Stream:
    def __init__(self, x, nb, seq, mods):
        self.x = x
        self.nb = nb
        self.seq = seq
        self.tm = _pick(seq, (512, 256, 128))
        self.tpb = seq // self.tm
        self.mods = mods
        self.rows = nb * seq


def _in_proj(st, norm_w, sc, sh, weights, n, tn, out_dtypes=None):
    d = st.x.shape[1]
    pro = [_rows_in(st.x, st.tm, d), _const_in(norm_w, d), _batch_in(sc, st.tpb, d), _batch_in(sh, st.tpb, d)]
    return _fused_matmul(pro, weights, [], _norm_mod, lambda accs: accs, rows=st.rows, tm=st.tm, k=d, n=n,
                         tn=tn, out_dtypes=out_dtypes or [F32] * len(weights))


def _out_proj(st, pro_ins, prologue, weight, gate, k):
    d = st.x.shape[1]
    epi = [_tile_in(st.x, st.tm, d), _batch_tile_in(gate, st.tpb, d)]
    (out,) = _fused_matmul(pro_ins, [weight], epi, prologue, lambda accs, x, g: [x + g * accs[0]],
                           rows=st.rows, tm=st.tm, k=k, n=d, tn=d, out_dtypes=[F32])
    return out


def _ffn(st, norm_w, sc, sh, gate, w1, w3, w2):
    d = st.x.shape[1]
    hidden = w1.shape[1]
    tn = _pick(hidden, (1408, 1024, 512, 256, 128))
    pro = [_rows_in(st.x, st.tm, d), _const_in(norm_w, d), _batch_in(sc, st.tpb, d), _batch_in(sh, st.tpb, d)]
    (act,) = _fused_matmul(pro, [w1, w3], [], _norm_mod, lambda accs: [_silu(accs[0]) * accs[1]],
                           rows=st.rows, tm=st.tm, k=d, n=hidden, tn=tn, out_dtypes=[BF16])
    return _out_proj(st, [_rows_in(act, st.tm, hidden)], lambda a: a, w2, gate, hidden)


def _mamba_layer(lat, ctx, p, ctx_out):
    inner = p["inner"]
    conv_ch = p["conv_ch"]
    heads = inner // M2_HEADDIM
    gw = inner // M2_GROUPS
    outs = {}
    acts = {}
    for name, st in (("ctx", ctx), ("lat", lat)):
        m = st.mods
        (zx,) = _in_proj(st, p["norm1"], m["sc1"], m["sh1"], [p["w_zx"]], inner + conv_ch,
                         _pick(inner + conv_ch, (1024, 512, 256, 128)))
        dt0, dt1 = _in_proj(st, p["norm1"], m["sc1"], m["sh1"], [p["w_dt0"], p["w_dt1"]], LANES, LANES)
        xbc = _dwconv(zx, p["conv_w"], p["conv_b"], nb=st.nb, seq=st.seq, col_off=inner // LANES,
                      channels=conv_ch, act=True)
        acts[name] = (zx, (dt0, dt1), xbc)
    ys = {}
    for dirn, reverse in ((0, False), (1, True)):
        consts = _ssd_consts(inner, reverse)
        h = jnp.zeros((ctx.nb, M2_GROUPS, M2_STATE, gw), F32)
        for name, st in (("ctx", ctx), ("lat", lat)):
            zx, dts, xbc = acts[name]
            y, h = _ssd_direction(xbc, dts[dirn], p["par"][dirn], p["dskip"] if dirn == 0 else None, h,
                                  ys.get(name), consts, nb=st.nb, seq=st.seq, reverse=reverse)
            ys[name] = y

    def gated_norm(y, z, w):
        return _rms(y * _silu(z), w)

    for name, st in (("ctx", ctx), ("lat", lat)):
        if name == "ctx" and not ctx_out:
            continue
        zx = acts[name][0]
        pro = [_rows_in(ys[name], st.tm, inner), _rows_in(zx, st.tm, inner), _const_in(p["norm_w"], inner)]
        outs[name] = _out_proj(st, pro, gated_norm, p["w_out"], st.mods["g1"], inner)
    return outs


def _rglru_layer(lat, ctx, p, ctx_out):
    width = p["width"]
    outs = {}
    acts = {}
    for name, st in (("ctx", ctx), ("lat", lat)):
        m = st.mods
        (yx,) = _in_proj(st, p["norm1"], m["sc1"], m["sh1"], [p["w_in"]], 2 * width,
                         _pick(2 * width, (1280, 1024, 512, 256, 128)))
        xc = _dwconv(yx, p["conv_w"], p["conv_b"], nb=st.nb, seq=st.seq, col_off=width // LANES,
                     channels=width, act=False)
        acts[name] = (yx, xc)
    rs = {}
    for dirn, reverse in ((0, False), (1, True)):
        h = jnp.zeros((ctx.nb, 1, width), F32)
        for name, st in (("ctx", ctx), ("lat", lat)):
            r, h = _rglru_direction(acts[name][1], p["gate_w"][dirn], p["gate_b"][dirn], p["a_param"][dirn], h,
                                    rs.get(name), nb=st.nb, seq=st.seq, reverse=reverse)
            rs[name] = r
    for name, st in (("ctx", ctx), ("lat", lat)):
        if name == "ctx" and not ctx_out:
            continue
        pro = [_rows_in(acts[name][0], st.tm, width), _rows_in(rs[name], st.tm, width)]
        outs[name] = _out_proj(st, pro, lambda y, r: _gelu_tanh(y) * r, p["w_out"], st.mods["g1"], width)
    return outs


def _s5_layer(lat, ctx, p, ctx_out):
    d = lat.x.shape[1]
    lbs = d // LANES
    sw = p["bb_re"][0].shape[2]
    us = {}
    for name, st in (("ctx", ctx), ("lat", lat)):
        m = st.mods
        us[name] = _norm_mod_rows(st.x, p["norm1"], m["sc1"], m["sh1"], tm=st.tm, tpb=st.tpb)
    ys = {}
    for dirn, reverse in ((0, False), (1, True)):
        h = jnp.zeros((ctx.nb, lbs, 2, sw), F32)
        for name, st in (("ctx", ctx), ("lat", lat)):
            y, h = _s5_direction(us[name], p["bb_re"][dirn], p["bb_im"][dirn], p["c_re"][dirn], p["c_im"][dirn],
                                 p["pw"][dirn], h, ys.get(name), nb=st.nb, seq=st.seq, reverse=reverse)
            ys[name] = y
    outs = {}
    for name, st in (("ctx", ctx), ("lat", lat)):
        if name == "ctx" and not ctx_out:
            continue
        pro = [_rows_in(ys[name], st.tm, d), _rows_in(us[name], st.tm, d), _const_in(p["dskip"], d)]
        epi = [_const_tile_in(p["glu_ba"], d), _const_tile_in(p["glu_bg"], d),
               _tile_in(st.x, st.tm, d), _batch_tile_in(st.mods["g1"], st.tpb, d)]
        (out,) = _fused_matmul(
            pro, [p["glu_wa"], p["glu_wg"]], epi,
            lambda y, u, dsk: _gelu_tanh(y + dsk * u),
            lambda accs, ba, bg, x, g: [x + g * ((accs[0] + ba) * _sigmoid(accs[1] + bg))],
            rows=st.rows, tm=st.tm, k=d, n=d, tn=d, out_dtypes=[F32])
        outs[name] = out
    return outs


def _mla_layer(lat, ctx, p, ctx_out):
    d = lat.x.shape[1]
    qr, kvr = p["q_rank"], p["kv_rank"]
    hp = MLA_HEADS * HEAD_PAD
    tn_h = _pick(hp, (512, 256, 128))
    reps = tn_h // HEAD_PAD
    qkv = {}
    for name, st in (("ctx", ctx), ("lat", lat)):
        m = st.mods
        n_in = p["w_in"].shape[1]
        (lat_all,) = _in_proj(st, p["norm1"], m["sc1"], m["sh1"], [p["w_in"]], n_in, n_in)
        cos_t, sin_t = p["tables"][name]
        tab_spec = lambda a, st=st: (a, (st.tm, HEAD_PAD), lambda i, j: (i % st.tpb, 0))

        def rope_epi(accs, cos, sin, reps=reps):
            cos_r = jnp.concatenate([cos] * reps, axis=1)
            sin_r = jnp.concatenate([sin] * reps, axis=1)
            return [accs[0] * cos_r + accs[1] * sin_r]

        want_q = name == "lat" or ctx_out
        q = None
        if want_q:
            (q,) = _fused_matmul([_rows_in(lat_all, st.tm, qr, 0), _const_in(p["q_norm"], qr)],
                                 [p["w_qa"], p["w_qb"]], [tab_spec(cos_t), tab_spec(sin_t)],
                                 _rms, rope_epi, rows=st.rows, tm=st.tm, k=qr, n=hp, tn=tn_h, out_dtypes=[BF16])

        def kv_epi(accs, kra, krb, cos, sin, reps=reps):
            kr = kra * cos + krb * sin
            return [accs[0] + jnp.concatenate([kr] * reps, axis=1), accs[1]]

        kra_blk = (qr // LANES)
        kv_blk = (qr + LANES) // kvr
        krb_blk = (qr + LANES + kvr) // LANES
        k, v = _fused_matmul(
            [_rows_in(lat_all, st.tm, kvr, kv_blk), _const_in(p["kv_norm"], kvr)],
            [p["w_k"], p["w_v"]],
            [(lat_all, (st.tm, LANES), lambda i, j: (i, kra_blk)),
             (lat_all, (st.tm, LANES), lambda i, j: (i, krb_blk)),
             tab_spec(cos_t), tab_spec(sin_t)],
            _rms, kv_epi, rows=st.rows, tm=st.tm, k=kvr, n=hp, tn=tn_h, out_dtypes=[BF16, BF16])
        qkv[name] = (q, k, v)
    scale = (MLA_NOPE + MLA_ROPE) ** -0.5
    nb = lat.nb
    lk = ctx.seq + lat.seq

    def cat(a_c, a_l):
        return jnp.concatenate([a_c.reshape(nb, ctx.seq, hp), a_l.reshape(nb, lat.seq, hp)], axis=1).reshape(nb * lk, hp)

    k_all = cat(qkv["ctx"][1], qkv["lat"][1])
    v_all = cat(qkv["ctx"][2], qkv["lat"][2])
    outs = {}
    o_l = _attention(qkv["lat"][0], k_all, v_all, nb=nb, lq=lat.seq, lk=lk, scale=scale)
    outs["lat"] = _out_proj(lat, [_rows_in(o_l, lat.tm, hp)], lambda a: a, p["w_out"], lat.mods["g1"], hp)
    if ctx_out:
        o_c = _attention(qkv["ctx"][0], qkv["ctx"][1], qkv["ctx"][2], nb=nb, lq=ctx.seq, lk=ctx.seq, scale=scale)
        outs["ctx"] = _out_proj(ctx, [_rows_in(o_c, ctx.tm, hp)], lambda a: a, p["w_out"], ctx.mods["g1"], hp)
    return outs


def _pad_cols(w, n):
    return jnp.pad(w, ((0, 0), (0, n - w.shape[1])))


def _mamba_params(in_w, conv_w, conv_b, dt_bias, a_log, d_skip, norm_w, out_w):
    d, proj = in_w.shape
    heads = dt_bias.shape[1]
    inner = heads * M2_HEADDIM
    conv_ch = inner + 2 * M2_GROUPS * M2_STATE
    par = []
    for dirn in range(2):
        rows = jnp.zeros((SUBLANES, LANES), F32)
        rows = rows.at[0, :heads].set(dt_bias[dirn].astype(F32)).at[1, :heads].set(a_log[dirn].astype(F32))
        par.append(rows)
    off = inner + conv_ch
    return dict(
        inner=inner, conv_ch=conv_ch,
        w_zx=in_w[:, :off].astype(BF16),
        w_dt0=_pad_cols(in_w[:, off:off + heads], LANES).astype(BF16),
        w_dt1=_pad_cols(in_w[:, off + heads:off + 2 * heads], LANES).astype(BF16),
        conv_w=conv_w.astype(F32), conv_b=conv_b.astype(F32)[None, :],
        par=par, dskip=jnp.repeat(d_skip.astype(F32), M2_HEADDIM)[None, :],
        norm_w=norm_w.astype(F32)[None, :], w_out=out_w.astype(BF16))


def _rglru_params(in_w, conv_w, conv_b, gate_w, gate_b, a_param, out_w):
    width = conv_w.shape[1]
    return dict(
        width=width, w_in=in_w.astype(BF16), conv_w=conv_w.astype(F32), conv_b=conv_b.astype(F32)[None, :],
        gate_w=[gate_w[dirn].astype(BF16) for dirn in range(2)],
        gate_b=[gate_b[dirn].astype(F32)[:, None, :] for dirn in range(2)],
        a_param=[a_param[dirn].astype(F32)[None, :] for dirn in range(2)],
        w_out=out_w.astype(BF16))


def _block_diag(m, per):
    g, r, c = m.shape
    m = m.reshape(g // per, per, r, c)
    eye = jnp.eye(per, dtype=m.dtype)
    return jnp.einsum("qarc,ab->qarbc", m, eye).reshape(g // per, per * r, per * c)


def _s5_params(lam_re, lam_im, log_step, b_re, b_im, c_re, c_im, d_skip, glu_w, glu_b):
    per = LANES // S5_GROUP
    br, bi = b_re.astype(F32), b_im.astype(F32)
    out = dict(bb_re=[], bb_im=[], c_re=[], c_im=[], pw=[])
    for dirn in range(2):
        lr = jnp.minimum(lam_re[dirn].astype(F32), -1e-4)
        li = lam_im[dirn].astype(F32)
        step = jnp.exp(log_step[dirn].astype(F32))[:, None]
        mag = jnp.exp(lr * step)
        abr, abi = mag * jnp.cos(li * step), mag * jnp.sin(li * step)
        den = lr * lr + li * li
        zr = ((abr - 1.0) * lr + abi * li) / den
        zi = (abi * lr - (abr - 1.0) * li) / den
        bbr = zr[..., None] * br - zi[..., None] * bi
        bbi = zr[..., None] * bi + zi[..., None] * br
        out["bb_re"].append(_block_diag(jnp.swapaxes(bbr, 1, 2), per).astype(BF16))
        out["bb_im"].append(_block_diag(jnp.swapaxes(bbi, 1, 2), per).astype(BF16))
        out["c_re"].append(_block_diag(jnp.swapaxes(c_re[dirn].astype(F32), 1, 2), per).astype(BF16))
        out["c_im"].append(_block_diag(jnp.swapaxes(c_im[dirn].astype(F32), 1, 2), per).astype(BF16))
        pr, pi = [abr], [abi]
        for _ in range(SUBLANES - 1):
            pr, pi = pr + [pr[-1] * abr - pi[-1] * abi], pi + [pr[-1] * abi + pi[-1] * abr]
        g = abr.shape[0]
        flat = lambda a: a.reshape(g // per, per * a.shape[1])
        t = jnp.arange(SUBLANES)[None, :, None]
        tiles = []
        for dist in (1, 2, 4):
            valid = (t < SUBLANES - dist) if dirn == 1 else (t >= dist)
            tiles.append(jnp.where(valid, flat(pr[dist - 1])[:, None, :], 0.0))
            tiles.append(jnp.where(valid, flat(pi[dist - 1])[:, None, :], 0.0))
        order = list(range(SUBLANES))[::-1] if dirn == 1 else list(range(SUBLANES))
        tiles.append(jnp.stack([flat(pr[o]) for o in order], axis=1))
        tiles.append(jnp.stack([flat(pi[o]) for o in order], axis=1))
        out["pw"].append(jnp.stack(tiles, axis=1).astype(F32))
    d = glu_w.shape[0]
    out.update(dskip=d_skip.astype(F32)[None, :],
               glu_wa=glu_w[:, :d].astype(BF16), glu_wg=glu_w[:, d:].astype(BF16),
               glu_ba=glu_b[:d].astype(F32)[None, :], glu_bg=glu_b[d:].astype(F32)[None, :])
    return out


def _head_pad_cols(w, per_head, take):
    kdim = w.shape[0]
    w = w.reshape(kdim, MLA_HEADS, per_head)[:, :, take]
    w = jnp.pad(w, ((0, 0), (0, 0), (0, HEAD_PAD - w.shape[2])))
    return w.reshape(kdim, MLA_HEADS * HEAD_PAD)


def _rope_swap(w):
    idx = jnp.arange(MLA_ROPE)
    blk, pos = idx // (2 * ROPE_FREQ), idx % (2 * ROPE_FREQ)
    return w[..., blk * 2 * ROPE_FREQ + (pos + ROPE_FREQ) % (2 * ROPE_FREQ)]


def _mla_params(in_w, q_norm_w, kv_norm_w, qb_w, kvb_w, out_w, seq, ctx_len):
    d = in_w.shape[0]
    qr, kvr = q_norm_w.shape[0], kv_norm_w.shape[0]
    dq = MLA_NOPE + MLA_ROPE
    w_q, w_kv, w_kr = in_w[:, :qr], in_w[:, qr:qr + kvr], in_w[:, qr + kvr:]

    def place(w):
        return jnp.pad(w, ((0, 0), (MLA_NOPE, HEAD_PAD - dq)))

    w_in = jnp.concatenate([w_q, place(w_kr), w_kv, place(_rope_swap(w_kr))], axis=1).astype(BF16)
    qb = qb_w.reshape(qr, MLA_HEADS, dq)
    qa = jnp.pad(qb, ((0, 0), (0, 0), (0, HEAD_PAD - dq))).reshape(qr, -1)
    qb_sw = jnp.pad(_rope_swap(qb[:, :, MLA_NOPE:]), ((0, 0), (0, 0), (MLA_NOPE, HEAD_PAD - dq))).reshape(qr, -1)
    per = MLA_NOPE + MLA_V
    w_k = _head_pad_cols(kvb_w, per, slice(0, MLA_NOPE))
    w_v = _head_pad_cols(kvb_w, per, slice(MLA_NOPE, per))
    w_out = jnp.pad(out_w.reshape(MLA_HEADS, MLA_V, d), ((0, 0), (0, HEAD_PAD - MLA_V), (0, 0))).reshape(-1, d)
    rows = seq // GRID_W
    row = jnp.repeat(jnp.arange(rows, dtype=F32), GRID_W)
    col = jnp.tile(jnp.arange(GRID_W, dtype=F32), rows)
    inv_freq = ROPE_BASE ** (-jnp.arange(ROPE_FREQ, dtype=F32) / ROPE_FREQ)
    ang = jnp.stack([row[:, None] * inv_freq, col[:, None] * inv_freq], axis=1)
    cos, sin = jnp.cos(ang), jnp.sin(ang)
    cos32 = jnp.concatenate([cos, cos], axis=2).reshape(seq, MLA_ROPE)
    sin32 = jnp.concatenate([-sin, sin], axis=2).reshape(seq, MLA_ROPE)
    ones = jnp.ones((seq, MLA_NOPE), F32)
    cos_l = jnp.pad(jnp.concatenate([ones, cos32], axis=1), ((0, 0), (0, HEAD_PAD - dq)))
    sin_l = jnp.pad(sin32, ((0, 0), (MLA_NOPE, HEAD_PAD - dq)))
    cos_c = jnp.pad(jnp.ones((ctx_len, dq), F32), ((0, 0), (0, HEAD_PAD - dq)))
    sin_c = jnp.zeros((ctx_len, HEAD_PAD), F32)
    return dict(q_rank=qr, kv_rank=kvr, w_in=w_in, q_norm=q_norm_w.astype(F32)[None, :],
                kv_norm=kv_norm_w.astype(F32)[None, :], w_qa=qa.astype(BF16), w_qb=qb_sw.astype(BF16),
                w_k=w_k.astype(BF16), w_v=w_v.astype(BF16), w_out=w_out.astype(BF16),
                tables=dict(lat=(cos_l, sin_l), ctx=(cos_c, sin_c)))


def _modulation(c, c_ctx, ada_w, ada_b):
    nb, d = c.shape
    rows = jnp.zeros((SUBLANES, d), F32).at[:nb].set(c).at[nb].set(c_ctx)
    n = ada_w.shape[1]
    (mod,) = _fused_matmul([(rows, (SUBLANES, d), lambda i, j: (0, 0))], [ada_w.astype(BF16)],
                           [_const_tile_in(ada_b.astype(F32)[None, :], _pick(n, (1024, 512, 256, 128)))],
                           _silu, lambda accs, b: [accs[0] + b], rows=SUBLANES, tm=SUBLANES, k=d, n=n,
                           tn=_pick(n, (1024, 512, 256, 128)), out_dtypes=[F32])
    names = ("sh1", "sc1", "g1", "sh2", "sc2", "g2")
    parts = jnp.split(mod, 6, axis=1)
    lat = {nm: pt[:nb][:, None, :] for nm, pt in zip(names, parts)}
    ctx = {nm: pt[nb:nb + 1][:, None, :] for nm, pt in zip(names, parts)}
    return lat, ctx


def kernel(x, c, ctx, c_ctx, ada_w, ada_b, norm1_w, norm2_w, ffn_w13, ffn_w2, m2_in_w, m2_conv_w, m2_conv_b, m2_dt_bias, m2_a_log, m2_d, m2_norm_w, m2_out_w, lru_in_w, lru_conv_w, lru_conv_b, lru_gate_w, lru_gate_b, lru_a_param, lru_out_w, s5_lambda_re, s5_lambda_im, s5_log_step, s5_b_re, s5_b_im, s5_c_re, s5_c_im, s5_d, s5_glu_w, s5_glu_b, mla_in_w, mla_q_norm_w, mla_kv_norm_w, mla_qb_w, mla_kvb_w, mla_out_w, final_norm_w):
    nb, seq, d = x.shape
    ctx_len = ctx.shape[1]
    depth = ada_w.shape[0]
    hidden = ffn_w2.shape[1]
    xl = x.reshape(nb * seq, d).astype(F32)
    xc = ctx.reshape(nb * ctx_len, d).astype(F32)
    for i in range(depth):
        kind, j = i % N_MIXERS, i // N_MIXERS
        ctx_out = i < depth - 1
        mods_l, mods_c = _modulation(c.astype(F32), c_ctx.astype(F32), ada_w[i], ada_b[i])
        lat_s = _Stream(xl, nb, seq, mods_l)
        ctx_s = _Stream(xc, nb, ctx_len, mods_c)
        n1 = norm1_w[i].astype(F32)[None, :]
        if kind == 0:
            p = _mamba_params(m2_in_w[j], m2_conv_w[j], m2_conv_b[j], m2_dt_bias[j], m2_a_log[j], m2_d[j],
                              m2_norm_w[j], m2_out_w[j])
            p["norm1"] = n1
            outs = _mamba_layer(lat_s, ctx_s, p, ctx_out)
        elif kind == 1:
            p = _rglru_params(lru_in_w[j], lru_conv_w[j], lru_conv_b[j], lru_gate_w[j], lru_gate_b[j],
                              lru_a_param[j], lru_out_w[j])
            p["norm1"] = n1
            outs = _rglru_layer(lat_s, ctx_s, p, ctx_out)
        elif kind == 2:
            p = _s5_params(s5_lambda_re[j], s5_lambda_im[j], s5_log_step[j], s5_b_re[j], s5_b_im[j],
                           s5_c_re[j], s5_c_im[j], s5_d[j], s5_glu_w[j], s5_glu_b[j])
            p["norm1"] = n1
            outs = _s5_layer(lat_s, ctx_s, p, ctx_out)
        else:
            p = _mla_params(mla_in_w[j], mla_q_norm_w[j], mla_kv_norm_w[j], mla_qb_w[j], mla_kvb_w[j],
                            mla_out_w[j], seq, ctx_len)
            p["norm1"] = n1
            outs = _mla_layer(lat_s, ctx_s, p, ctx_out)
        n2 = norm2_w[i].astype(F32)[None, :]
        w1 = ffn_w13[i][:, :hidden].astype(BF16)
        w3 = ffn_w13[i][:, hidden:].astype(BF16)
        w2 = ffn_w2[i].astype(BF16)
        lat_s = _Stream(outs["lat"], nb, seq, mods_l)
        xl = _ffn(lat_s, n2, mods_l["sc2"], mods_l["sh2"], mods_l["g2"], w1, w3, w2)
        if ctx_out:
            ctx_s = _Stream(outs["ctx"], nb, ctx_len, mods_c)
            xc = _ffn(ctx_s, n2, mods_c["sc2"], mods_c["sh2"], mods_c["g2"], w1, w3, w2)
    zero = jnp.zeros((1, 1, d), F32)
    lat_s = _Stream(xl, nb, seq, None)
    out = _norm_mod_rows(xl, final_norm_w.astype(F32)[None, :], zero, zero, tm=lat_s.tm, tpb=lat_s.tpb)
    return out.reshape(nb, seq, d).astype(x.dtype)
```

```python
import functools
import math

import jax
import jax.numpy as jnp
from jax import lax
from jax.experimental import pallas as pl
from jax.experimental.pallas import tpu as pltpu

F32 = jnp.float32
BF16 = jnp.bfloat16

LANES = 128
SUBLANES = 8
VMEM_BUDGET_BYTES = 56 * 2**20

GRID_W = 64
N_MIXERS = 4
NORM_EPS = 1e-6
CONV_WIDTH = 4
CONV_PAD_LEFT = 2
M2_HEADDIM = 64
M2_GROUPS = 4
M2_STATE = 128
SSD_CHUNK = 128
LRU_BLOCK = 128
LRU_C = 8.0
S5_GROUP = 16
S5_STATE = 64
MLA_HEADS = 16
MLA_NOPE = 64
MLA_ROPE = 32
MLA_V = 64
ROPE_FREQ = MLA_ROPE // 4
ROPE_BASE = 10000.0
HEAD_PAD = 128
V_ONES_LANE = MLA_V


def _pick(n, cands):
    for c in cands:
        if n % c == 0:
            return c
    raise ValueError(f"no tile in {cands} divides {n}")


def _params(sem, est_bytes):
    limit = int(min(max(2 * est_bytes, 32 * 2**20), VMEM_BUDGET_BYTES))
    return pltpu.CompilerParams(dimension_semantics=sem, vmem_limit_bytes=limit)


def _nbytes(shape, dtype):
    return math.prod(s for s in shape if s is not None) * jnp.dtype(dtype).itemsize


def _sigmoid(x):
    return 1.0 / (1.0 + jnp.exp(-x))


def _silu(x):
    return x * _sigmoid(x)


def _softplus(x):
    return jnp.maximum(x, 0.0) + jnp.log1p(jnp.exp(-jnp.abs(x)))


def _gelu_tanh(x):
    return 0.5 * x * (1.0 + jnp.tanh(math.sqrt(2.0 / math.pi) * (x + 0.044715 * (x * x * x))))


def _rms(x, w):
    return x * lax.rsqrt(jnp.mean(x * x, axis=-1, keepdims=True) + NORM_EPS) * w


def _norm_mod(x, w, sc, sh):
    return _rms(x, w) * (1.0 + sc) + sh


def _split3(q):
    q1 = q.astype(BF16)
    r1 = q - q1.astype(F32)
    q2 = r1.astype(BF16)
    q3 = (r1 - q2.astype(F32)).astype(BF16)
    return q1, q2, q3


def _select_cols(q, e):
    return sum(jnp.dot(p, e, preferred_element_type=F32) for p in _split3(q))


def _select_rows(t, q):
    return sum(jnp.dot(t, p, preferred_element_type=F32) for p in _split3(q))


def _fused_matmul(pro_ins, weights, epi_ins, prologue, epilogue, *, rows, tm, k, n, tn, out_dtypes):
    n_p, n_w, n_e, n_o = len(pro_ins), len(weights), len(epi_ins), len(out_dtypes)

    def body(*refs):
        p = refs[:n_p]
        w = refs[n_p:n_p + n_w]
        e = refs[n_p + n_w:n_p + n_w + n_e]
        o = refs[n_p + n_w + n_e:n_p + n_w + n_e + n_o]
        a_scr = refs[-1]

        @pl.when(pl.program_id(1) == 0)
        def _():
            a_scr[...] = prologue(*[r[...] for r in p]).astype(BF16)

        a = a_scr[...]
        accs = [jnp.dot(a, wr[...], preferred_element_type=F32) for wr in w]
        outs = epilogue(accs, *[r[...] for r in e])
        for o_ref, val in zip(o, outs):
            o_ref[...] = val.astype(o_ref.dtype)

    in_specs = ([pl.BlockSpec(bs, im) for (_, bs, im) in pro_ins]
                + [pl.BlockSpec((k, tn), lambda i, j: (0, j)) for _ in weights]
                + [pl.BlockSpec(bs, im) for (_, bs, im) in epi_ins])
    out_specs = [pl.BlockSpec((tm, tn), lambda i, j: (i, j)) for _ in out_dtypes]
    est = (sum(_nbytes(bs, a.dtype) for (a, bs, _) in pro_ins + epi_ins) * 2
           + n_w * k * tn * 2 * 2 + sum(tm * tn * jnp.dtype(d).itemsize for d in out_dtypes) * 2
           + tm * k * 2 + (n_w + 2) * tm * tn * 4 + tm * k * 8)
    outs = pl.pallas_call(
        body,
        out_shape=[jax.ShapeDtypeStruct((rows, n), d) for d in out_dtypes],
        grid=(rows // tm, n // tn),
        in_specs=in_specs,
        out_specs=out_specs,
        scratch_shapes=[pltpu.VMEM((tm, k), BF16)],
        compiler_params=_params(("parallel", "arbitrary"), est),
        name=f"mm_r{rows}_k{k}_n{n}x{n_w}",
    )(*[a for (a, _, _) in pro_ins], *weights, *[a for (a, _, _) in epi_ins])
    return outs


def _rows_in(a, tm, width, col_block=0):
    return (a, (tm, width), lambda i, j: (i, col_block))


def _batch_in(a, tpb, width):
    if a.shape[0] == 1:
        return (a, (None, 1, width), lambda i, j: (0, 0, 0))
    return (a, (None, 1, width), lambda i, j: (i // tpb, 0, 0))


def _const_in(a, width):
    return (a, (1, width), lambda i, j: (0, 0))


def _tile_in(a, tm, tn):
    return (a, (tm, tn), lambda i, j: (i, j))


def _batch_tile_in(a, tpb, tn):
    if a.shape[0] == 1:
        return (a, (None, 1, tn), lambda i, j: (0, 0, j))
    return (a, (None, 1, tn), lambda i, j: (i // tpb, 0, j))


def _const_tile_in(a, tn):
    return (a, (1, tn), lambda i, j: (0, j))


def _norm_mod_rows(x, w, sc, sh, *, tm, tpb):
    rows, d = x.shape

    def body(x_ref, w_ref, sc_ref, sh_ref, o_ref):
        o_ref[...] = _norm_mod(x_ref[...], w_ref[...], sc_ref[...], sh_ref[...])

    def bidx(a):
        if a.shape[0] == 1:
            return lambda i: (0, 0, 0)
        return lambda i: (i // tpb, 0, 0)

    return pl.pallas_call(
        body,
        out_shape=jax.ShapeDtypeStruct((rows, d), F32),
        grid=(rows // tm,),
        in_specs=[pl.BlockSpec((tm, d), lambda i: (i, 0)),
                  pl.BlockSpec((1, d), lambda i: (0, 0)),
                  pl.BlockSpec((None, 1, d), bidx(sc)),
                  pl.BlockSpec((None, 1, d), bidx(sh))],
        out_specs=pl.BlockSpec((tm, d), lambda i: (i, 0)),
        compiler_params=_params(("parallel",), 6 * tm * d * 4),
        name="norm_mod",
    )(x, w, sc, sh)


def _dwconv(x, w, b, *, nb, seq, col_off, channels, act):
    cb = channels // LANES
    rc = _pick(seq, (512, 256, 128))
    pad = SUBLANES

    def body(x_ref, w_ref, b_ref, o_ref, p_scr):
        p_scr[0:pad, :] = jnp.zeros((pad, LANES), F32)
        p_scr[pad + seq:pad + seq + pad, :] = jnp.zeros((pad, LANES), F32)
        p_scr[pad:pad + seq, :] = x_ref[...]
        wv = w_ref[...]
        bv = b_ref[...]

        def step(i, carry):
            r0 = pl.multiple_of(i * rc, rc)
            y = bv
            for t in range(CONV_WIDTH):
                y = y + wv[t:t + 1, :] * p_scr[pl.ds(r0 + pad - CONV_PAD_LEFT + t, rc), :]
            if act:
                y = _silu(y)
            o_ref[pl.ds(r0, rc), :] = y
            return carry

        lax.fori_loop(0, seq // rc, step, 0)

    return pl.pallas_call(
        body,
        out_shape=jax.ShapeDtypeStruct((nb * seq, channels), F32),
        grid=(nb, cb),
        in_specs=[pl.BlockSpec((seq, LANES), lambda bi, ci: (bi, col_off + ci)),
                  pl.BlockSpec((CONV_WIDTH, LANES), lambda bi, ci: (0, ci)),
                  pl.BlockSpec((1, LANES), lambda bi, ci: (0, ci))],
        out_specs=pl.BlockSpec((seq, LANES), lambda bi, ci: (bi, ci)),
        scratch_shapes=[pltpu.VMEM((seq + 2 * pad, LANES), F32)],
        compiler_params=_params(("parallel", "parallel"), 5 * seq * LANES * 4),
        name="dwconv",
    )(x, w, b)


def _ssd_direction(xbc, dt, par, dskip, h0, yprev, consts, *, nb, seq, reverse):
    tri, e_head, e_wide = consts
    q = SSD_CHUNK
    nc = seq // q
    inner = xbc.shape[1] - 2 * M2_GROUPS * M2_STATE
    gw = inner // M2_GROUPS
    heads = inner // M2_HEADDIM
    hpg = heads // M2_GROUPS
    has_prev = yprev is not None
    has_skip = dskip is not None

    def body(*refs):
        it = iter(refs)
        xbc_ref, dt_ref, par_ref = next(it), next(it), next(it)
        dsk_ref = next(it) if has_skip else None
        h0_ref = next(it)
        yp_ref = next(it) if has_prev else None
        tri_ref, eh_ref, ew_ref = next(it), next(it), next(it)
        y_ref, hout_ref, h_scr = next(it), next(it), next(it)
        c = pl.program_id(1)

        @pl.when(c == 0)
        def _():
            h_scr[...] = h0_ref[...]

        dtv = _softplus(dt_ref[...] + par_ref[0:1, :])
        da = dtv * (-jnp.exp(par_ref[1:2, :]))
        acs = _select_rows(tri_ref[...], da)
        acs_t = acs.T
        last = acs[0:1, :] if reverse else acs[q - 1:q, :]
        eh = eh_ref[...]
        dt_e = _select_cols(dtv, eh)
        dout_e = _select_cols(jnp.exp(last - acs), eh)
        eacs_e = _select_cols(jnp.exp(acs), eh)
        cdec_e = _select_cols(jnp.broadcast_to(jnp.exp(last), (SUBLANES, LANES)), eh)[0:1, :]
        acol = _select_cols(acs, ew_ref[...])
        xs = xbc_ref[:, 0:inner]
        x_dt = xs * dt_e
        x_b = x_dt.astype(BF16)
        x_d = (x_dt * dout_e).astype(BF16)
        row = lax.broadcasted_iota(jnp.int32, (q, q), 0)
        col = lax.broadcasted_iota(jnp.int32, (q, q), 1)
        mask = (col >= row) if reverse else (col <= row)
        lane = lax.broadcasted_iota(jnp.int32, (q, LANES), 1)
        for g in range(M2_GROUPS):
            bm = xbc_ref[:, inner + g * M2_STATE:inner + (g + 1) * M2_STATE]
            cm = xbc_ref[:, inner + (M2_GROUPS + g) * M2_STATE:inner + (M2_GROUPS + g + 1) * M2_STATE]
            cm_b = cm.astype(BF16)
            scores = lax.dot_general(cm_b, bm.astype(BF16), (((1,), (1,)), ((), ())),
                                     preferred_element_type=F32)
            h_t = h_scr[g]
            y_off = jnp.dot(cm_b, h_t.astype(BF16), preferred_element_type=F32) * eacs_e[:, g * gw:(g + 1) * gw]
            s_t = jnp.dot(bm.T.astype(BF16), x_d[:, g * gw:(g + 1) * gw], preferred_element_type=F32)
            h_scr[g] = h_t * cdec_e[:, g * gw:(g + 1) * gw] + s_t
            for pr in range(hpg // 2):
                k0 = g * hpg + 2 * pr
                c0 = k0 * M2_HEADDIM
                ys = []
                for k in (k0, k0 + 1):
                    diff = acol[:, k * LANES:(k + 1) * LANES] - acs_t[k:k + 1, :]
                    decay = jnp.exp(jnp.where(mask, diff, -jnp.inf))
                    m = (scores * decay).astype(BF16)
                    ys.append(jnp.dot(m, x_b[:, c0:c0 + LANES], preferred_element_type=F32))
                y = jnp.where(lane < M2_HEADDIM, ys[0], ys[1]) + y_off[:, 2 * pr * M2_HEADDIM:2 * pr * M2_HEADDIM + LANES]
                if has_skip:
                    y = y + dsk_ref[:, c0:c0 + LANES] * xs[:, c0:c0 + LANES]
                if has_prev:
                    y = y + yp_ref[:, c0:c0 + LANES]
                y_ref[:, c0:c0 + LANES] = y

        @pl.when(c == nc - 1)
        def _():
            hout_ref[...] = h_scr[...]

    def rb(bi, ci):
        return bi * nc + ((nc - 1 - ci) if reverse else ci)

    width = xbc.shape[1]
    st_shape = (M2_GROUPS, M2_STATE, gw)
    ins = [xbc, dt, par]
    in_specs = [pl.BlockSpec((q, width), lambda bi, ci: (rb(bi, ci), 0)),
                pl.BlockSpec((q, LANES), lambda bi, ci: (rb(bi, ci), 0)),
                pl.BlockSpec((SUBLANES, LANES), lambda bi, ci: (0, 0))]
    if has_skip:
        ins.append(dskip)
        in_specs.append(pl.BlockSpec((1, inner), lambda bi, ci: (0, 0)))
    ins.append(h0)
    in_specs.append(pl.BlockSpec((None,) + st_shape, lambda bi, ci: (bi, 0, 0, 0)))
    if has_prev:
        ins.append(yprev)
        in_specs.append(pl.BlockSpec((q, inner), lambda bi, ci: (rb(bi, ci), 0)))
    ins += [tri, e_head, e_wide]
    in_specs += [pl.BlockSpec(tri.shape, lambda bi, ci: (0, 0)),
                 pl.BlockSpec(e_head.shape, lambda bi, ci: (0, 0)),
                 pl.BlockSpec(e_wide.shape, lambda bi, ci: (0, 0))]
    est = (q * width * 4 * 2 + q * inner * 4 * 4 + 3 * math.prod(st_shape) * 4 * 2
           + (e_head.size + e_wide.size) * 2 * 2 + 12 * q * inner * 4 + q * heads * LANES * 4)
    y, h_last = pl.pallas_call(
        body,
        out_shape=[jax.ShapeDtypeStruct((nb * seq, inner), F32),
                   jax.ShapeDtypeStruct((nb,) + st_shape, F32)],
        grid=(nb, nc),
        in_specs=in_specs,
        out_specs=[pl.BlockSpec((q, inner), lambda bi, ci: (rb(bi, ci), 0)),
                   pl.BlockSpec((None,) + st_shape, lambda bi, ci: (bi, 0, 0, 0))],
        scratch_shapes=[pltpu.VMEM(st_shape, F32)],
        compiler_params=_params(("parallel", "arbitrary"), est),
        name="ssd_bwd" if reverse else "ssd_fwd",
    )(*ins)
    return y, h_last


def _ssd_consts(inner, reverse):
    heads = inner // M2_HEADDIM
    r = jnp.arange(SSD_CHUNK)
    tri = (r[None, :] >= r[:, None]) if reverse else (r[None, :] <= r[:, None])
    hk = jnp.arange(LANES)[:, None]
    e_head = (hk == (jnp.arange(inner)[None, :] // M2_HEADDIM))
    e_wide = (hk == (jnp.arange(heads * LANES)[None, :] // LANES))
    return tri.astype(BF16), e_head.astype(BF16), e_wide.astype(BF16)


def _tile_scan(a, u, reverse):
    row = lax.broadcasted_iota(jnp.int32, a.shape, 0)
    for d in (1, 2, 4):
        shift = (SUBLANES - d) if reverse else d
        valid = (row < SUBLANES - d) if reverse else (row >= d)
        a_s = jnp.where(valid, pltpu.roll(a, shift, 0), 1.0)
        u_s = jnp.where(valid, pltpu.roll(u, shift, 0), 0.0)
        u = a * u_s + u
        a = a * a_s
    return a, u


def _rglru_direction(xc, gate_w, gate_b, a_param, h0, rprev, *, nb, seq, reverse):
    width = xc.shape[1]
    nblk = width // LRU_BLOCK
    tb = _pick(seq, (256, 128))
    nt = seq // tb
    has_prev = rprev is not None

    def body(*refs):
        it = iter(refs)
        x_ref, gw_ref, gb_ref, ap_ref, h0_ref = next(it), next(it), next(it), next(it), next(it)
        rp_ref = next(it) if has_prev else None
        r_ref, hout_ref, a_scr, u_scr, h_scr = next(it), next(it), next(it), next(it), next(it)
        tstep = pl.program_id(1)

        @pl.when(tstep == 0)
        def _():
            h_scr[...] = jnp.broadcast_to(h0_ref[...], (SUBLANES, width))

        log_base = -LRU_C * _softplus(-ap_ref[...])
        for nbk in range(nblk):
            cs = slice(nbk * LRU_BLOCK, (nbk + 1) * LRU_BLOCK)
            xb = x_ref[:, cs]
            g = jnp.dot(xb.astype(BF16), gw_ref[nbk], preferred_element_type=F32) + gb_ref[nbk]
            g = _sigmoid(g)
            log_a = g[:, :LRU_BLOCK] * log_base[:, cs]
            a = jnp.exp(log_a)
            mult = jnp.sqrt(jnp.maximum(1.0 - a * a, 0.0))
            a_scr[:, cs] = a
            u_scr[:, cs] = xb * g[:, LRU_BLOCK:] * mult

        def step(i, h):
            grp = (tb // SUBLANES - 1 - i) if reverse else i
            r0 = pl.multiple_of(grp * SUBLANES, SUBLANES)
            a_c, h_loc = _tile_scan(a_scr[pl.ds(r0, SUBLANES), :], u_scr[pl.ds(r0, SUBLANES), :], reverse)
            h_new = a_c * h + h_loc
            out = h_new
            if has_prev:
                out = out + rp_ref[pl.ds(r0, SUBLANES), :]
            r_ref[pl.ds(r0, SUBLANES), :] = out
            edge = h_new[0:1, :] if reverse else h_new[SUBLANES - 1:SUBLANES, :]
            return jnp.broadcast_to(edge, (SUBLANES, width))

        h_fin = lax.fori_loop(0, tb // SUBLANES, step, h_scr[...])
        h_scr[...] = h_fin

        @pl.when(tstep == nt - 1)
        def _():
            hout_ref[...] = h_fin[0:1, :]

    def rb(bi, ti):
        return bi * nt + ((nt - 1 - ti) if reverse else ti)

    ins = [xc, gate_w, gate_b, a_param, h0]
    in_specs = [pl.BlockSpec((tb, width), lambda bi, ti: (rb(bi, ti), 0)),
                pl.BlockSpec(gate_w.shape, lambda bi, ti: (0, 0, 0)),
                pl.BlockSpec(gate_b.shape, lambda bi, ti: (0, 0, 0)),
                pl.BlockSpec((1, width), lambda bi, ti: (0, 0)),
                pl.BlockSpec((None, 1, width), lambda bi, ti: (bi, 0, 0))]
    if has_prev:
        ins.append(rprev)
        in_specs.append(pl.BlockSpec((tb, width), lambda bi, ti: (rb(bi, ti), 0)))
    est = 10 * tb * width * 4 + gate_w.size * 2 * 2
    r, h_last = pl.pallas_call(
        body,
        out_shape=[jax.ShapeDtypeStruct((nb * seq, width), F32),
                   jax.ShapeDtypeStruct((nb, 1, width), F32)],
        grid=(nb, nt),
        in_specs=in_specs,
        out_specs=[pl.BlockSpec((tb, width), lambda bi, ti: (rb(bi, ti), 0)),
                   pl.BlockSpec((None, 1, width), lambda bi, ti: (bi, 0, 0))],
        scratch_shapes=[pltpu.VMEM((tb, width), F32), pltpu.VMEM((tb, width), F32),
                        pltpu.VMEM((SUBLANES, width), F32)],
        compiler_params=_params(("parallel", "arbitrary"), est),
        name="rglru_bwd" if reverse else "rglru_fwd",
    )(*ins)
    return r, h_last


def _s5_direction(u, bb_re, bb_im, c_re, c_im, pw, h0, yprev, *, nb, seq, reverse):
    d_model = u.shape[1]
    lbs = d_model // LANES
    sw = bb_re.shape[2]
    tb = _pick(seq, (512, 256, 128))
    nt = seq // tb
    has_prev = yprev is not None

    def body(*refs):
        it = iter(refs)
        u_ref, br_ref, bi_ref, cr_ref, ci_ref, pw_ref, h0_ref = (next(it) for _ in range(7))
        yp_ref = next(it) if has_prev else None
        y_ref, hout_ref, r_scr, i_scr, h_scr = next(it), next(it), next(it), next(it), next(it)
        tstep = pl.program_id(2)

        @pl.when(tstep == 0)
        def _():
            h_scr[0] = jnp.broadcast_to(h0_ref[0:1, :], (SUBLANES, sw))
            h_scr[1] = jnp.broadcast_to(h0_ref[1:2, :], (SUBLANES, sw))

        ub = u_ref[...].astype(BF16)
        r_scr[...] = jnp.dot(ub, br_ref[...], preferred_element_type=F32)
        i_scr[...] = jnp.dot(ub, bi_ref[...], preferred_element_type=F32)

        def step(i, carry):
            hr, hi = carry
            grp = (tb // SUBLANES - 1 - i) if reverse else i
            r0 = pl.multiple_of(grp * SUBLANES, SUBLANES)
            xr = r_scr[pl.ds(r0, SUBLANES), :]
            xi = i_scr[pl.ds(r0, SUBLANES), :]
            for n_step, dist in enumerate((1, 2, 4)):
                shift = (SUBLANES - dist) if reverse else dist
                pr = pw_ref[2 * n_step]
                pi = pw_ref[2 * n_step + 1]
                sr = pltpu.roll(xr, shift, 0)
                si = pltpu.roll(xi, shift, 0)
                xr, xi = xr + (pr * sr - pi * si), xi + (pr * si + pi * sr)
            ar = pw_ref[6]
            ai = pw_ref[7]
            nr = xr + (ar * hr - ai * hi)
            ni = xi + (ar * hi + ai * hr)
            r_scr[pl.ds(r0, SUBLANES), :] = nr
            i_scr[pl.ds(r0, SUBLANES), :] = ni
            e = 0 if reverse else SUBLANES - 1
            return (jnp.broadcast_to(nr[e:e + 1, :], (SUBLANES, sw)),
                    jnp.broadcast_to(ni[e:e + 1, :], (SUBLANES, sw)))

        hr, hi = lax.fori_loop(0, tb // SUBLANES, step, (h_scr[0], h_scr[1]))
        h_scr[0] = hr
        h_scr[1] = hi
        y = (jnp.dot(r_scr[...].astype(BF16), cr_ref[...], preferred_element_type=F32)
             - jnp.dot(i_scr[...].astype(BF16), ci_ref[...], preferred_element_type=F32))
        if has_prev:
            y = y + yp_ref[...]
        y_ref[...] = y

        @pl.when(tstep == nt - 1)
        def _():
            hout_ref[0:1, :] = hr[0:1, :]
            hout_ref[1:2, :] = hi[0:1, :]

    def rb(bi, ti):
        return bi * nt + ((nt - 1 - ti) if reverse else ti)

    ins = [u, bb_re, bb_im, c_re, c_im, pw, h0]
    in_specs = [pl.BlockSpec((tb, LANES), lambda bi, li, ti: (rb(bi, ti), li)),
                pl.BlockSpec((None, LANES, sw), lambda bi, li, ti: (li, 0, 0)),
                pl.BlockSpec((None, LANES, sw), lambda bi, li, ti: (li, 0, 0)),
                pl.BlockSpec((None, sw, LANES), lambda bi, li, ti: (li, 0, 0)),
                pl.BlockSpec((None, sw, LANES), lambda bi, li, ti: (li, 0, 0)),
                pl.BlockSpec((None, 8, SUBLANES, sw), lambda bi, li, ti: (li, 0, 0, 0)),
                pl.BlockSpec((None, None, 2, sw), lambda bi, li, ti: (bi, li, 0, 0))]
    if has_prev:
        ins.append(yprev)
        in_specs.append(pl.BlockSpec((tb, LANES), lambda bi, li, ti: (rb(bi, ti), li)))
    est = 6 * tb * sw * 4 + 6 * tb * LANES * 4 + 8 * LANES * sw * 2
    y, h_last = pl.pallas_call(
        body,
        out_shape=[jax.ShapeDtypeStruct((nb * seq, d_model), F32),
                   jax.ShapeDtypeStruct((nb, lbs, 2, sw), F32)],
        grid=(nb, lbs, nt),
        in_specs=in_specs,
        out_specs=[pl.BlockSpec((tb, LANES), lambda bi, li, ti: (rb(bi, ti), li)),
                   pl.BlockSpec((None, None, 2, sw), lambda bi, li, ti: (bi, li, 0, 0))],
        scratch_shapes=[pltpu.VMEM((tb, sw), F32), pltpu.VMEM((tb, sw), F32),
                        pltpu.VMEM((2, SUBLANES, sw), F32)],
        compiler_params=_params(("parallel", "parallel", "arbitrary"), est),
        name="s5_bwd" if reverse else "s5_fwd",
    )(*ins)
    return y, h_last


def _attention(q, kv_parts, *, nb, lq, scale):
    heads = q.shape[1] // HEAD_PAD
    tq = _pick(lq, (1024, 512, 256, 128))
    tsub = _pick(tq, (256, 128))
    nq = lq // tq
    c = scale * math.log2(math.e)
    chunks = [_pick(length, (1024, 512, 256, 128)) for (_, _, length) in kv_parts]
    n_parts = len(kv_parts)

    def body(*refs):
        q_ref = refs[0]
        kv_refs = refs[1:1 + 2 * n_parts]
        o_ref = refs[1 + 2 * n_parts]
        s_a, s_b = refs[2 + 2 * n_parts], refs[3 + 2 * n_parts]

        def q_tile(t, carry):
            r0 = pl.multiple_of(t * tsub, tsub)
            qt = q_ref[pl.ds(r0, tsub), :]

            def scores(k):
                return lax.dot_general(qt, k, (((1,), (1,)), ((), ())), preferred_element_type=F32)

            def absorb(s, v, m, acc):
                m_new = jnp.maximum(m, jnp.max(s, axis=1, keepdims=True))
                p = jnp.exp2((s - m_new) * c)
                alpha = jnp.exp2((m - m_new) * c)
                acc = alpha * acc + jnp.dot(p.astype(BF16), v, preferred_element_type=F32)
                return m_new, acc

            m = jnp.full((tsub, 1), -jnp.inf, F32)
            acc = jnp.zeros((tsub, HEAD_PAD), F32)
            for pi, (_, _, length) in enumerate(kv_parts):
                k_ref, v_ref = kv_refs[2 * pi], kv_refs[2 * pi + 1]
                tk = chunks[pi]
                n_ch = length // tk
                if n_ch == 1:
                    m, acc = absorb(scores(k_ref[...]), v_ref[...], m, acc)
                    continue

                def rows(ref, i, tk=tk):
                    return ref[pl.ds(pl.multiple_of(i * tk, tk), tk), :]

                s_a[:, 0:tk] = scores(rows(k_ref, 0))

                def pair(j, mc, k_ref=k_ref, v_ref=v_ref, tk=tk):
                    m, acc = mc
                    s_b[:, 0:tk] = scores(rows(k_ref, 2 * j + 1))
                    m, acc = absorb(s_a[:, 0:tk], rows(v_ref, 2 * j), m, acc)
                    s_a[:, 0:tk] = scores(rows(k_ref, 2 * j + 2))
                    return absorb(s_b[:, 0:tk], rows(v_ref, 2 * j + 1), m, acc)

                n_pairs = (n_ch - 1) // 2
                m, acc = lax.fori_loop(0, n_pairs, pair, (m, acc))
                if n_ch - 2 * n_pairs == 2:
                    s_b[:, 0:tk] = scores(rows(k_ref, n_ch - 1))
                    m, acc = absorb(s_a[:, 0:tk], rows(v_ref, n_ch - 2), m, acc)
                    m, acc = absorb(s_b[:, 0:tk], rows(v_ref, n_ch - 1), m, acc)
                else:
                    m, acc = absorb(s_a[:, 0:tk], rows(v_ref, n_ch - 1), m, acc)
            o_ref[pl.ds(r0, tsub), :] = (acc / acc[:, V_ONES_LANE:V_ONES_LANE + 1]).astype(o_ref.dtype)
            return carry

        lax.fori_loop(0, tq // tsub, q_tile, 0)

    ins = [q]
    in_specs = [pl.BlockSpec((tq, HEAD_PAD), lambda b, h, qi: (b * nq + qi, h))]
    est = tq * HEAD_PAD * 2 * 4 + 6 * tsub * max(chunks) * 4
    for (k, v, length) in kv_parts:
        ins += [k, v]
        in_specs += [pl.BlockSpec((length, HEAD_PAD), lambda b, h, qi: (b, h))] * 2
        est += 2 * length * HEAD_PAD * 2 * 2
    return pl.pallas_call(
        body,
        out_shape=jax.ShapeDtypeStruct(q.shape, BF16),
        grid=(nb, heads, nq),
        in_specs=in_specs,
        out_specs=pl.BlockSpec((tq, HEAD_PAD), lambda b, h, qi: (b * nq + qi, h)),
        scratch_shapes=[pltpu.VMEM((tsub, max(chunks)), F32), pltpu.VMEM((tsub, max(chunks)), F32)],
        compiler_params=_params(("parallel", "parallel", "arbitrary"), est),
        name="attention",
    )(*ins)


class _Stream:
    def __init__(self, x, nb, seq, mods):
        self.x = x
        self.nb = nb
        self.seq = seq
        self.tm = _pick(seq, (512, 256, 128))
        self.tpb = seq // self.tm
        self.mods = mods
        self.rows = nb * seq


def _in_proj(st, norm_w, sc, sh, weights, n, tn, out_dtypes=None):
    d = st.x.shape[1]
    pro = [_rows_in(st.x, st.tm, d), _const_in(norm_w, d), _batch_in(sc, st.tpb, d), _batch_in(sh, st.tpb, d)]
    return _fused_matmul(pro, weights, [], _norm_mod, lambda accs: accs, rows=st.rows, tm=st.tm, k=d, n=n,
                         tn=tn, out_dtypes=out_dtypes or [F32] * len(weights))


def _out_proj(st, pro_ins, prologue, weight, gate, k):
    d = st.x.shape[1]
    epi = [_tile_in(st.x, st.tm, d), _batch_tile_in(gate, st.tpb, d)]
    (out,) = _fused_matmul(pro_ins, [weight], epi, prologue, lambda accs, x, g: [x + g * accs[0]],
                           rows=st.rows, tm=st.tm, k=k, n=d, tn=d, out_dtypes=[F32])
    return out


def _ffn(st, norm_w, sc, sh, gate, w1, w3, w2):
    d = st.x.shape[1]
    hidden = w1.shape[1]
    tn = _pick(hidden, (1408, 1024, 512, 256, 128))
    pro = [_rows_in(st.x, st.tm, d), _const_in(norm_w, d), _batch_in(sc, st.tpb, d), _batch_in(sh, st.tpb, d)]
    (act,) = _fused_matmul(pro, [w1, w3], [], _norm_mod, lambda accs: [_silu(accs[0]) * accs[1]],
                           rows=st.rows, tm=st.tm, k=d, n=hidden, tn=tn, out_dtypes=[BF16])
    return _out_proj(st, [_rows_in(act, st.tm, hidden)], lambda a: a, w2, gate, hidden)


def _mamba_layer(lat, ctx, p, ctx_out):
    inner = p["inner"]
    conv_ch = p["conv_ch"]
    heads = inner // M2_HEADDIM
    gw = inner // M2_GROUPS
    outs = {}
    acts = {}
    for name, st in (("ctx", ctx), ("lat", lat)):
        m = st.mods
        (zx,) = _in_proj(st, p["norm1"], m["sc1"], m["sh1"], [p["w_zx"]], inner + conv_ch,
                         _pick(inner + conv_ch, (1024, 512, 256, 128)))
        dt0, dt1 = _in_proj(st, p["norm1"], m["sc1"], m["sh1"], [p["w_dt0"], p["w_dt1"]], LANES, LANES)
        xbc = _dwconv(zx, p["conv_w"], p["conv_b"], nb=st.nb, seq=st.seq, col_off=inner // LANES,
                      channels=conv_ch, act=True)
        acts[name] = (zx, (dt0, dt1), xbc)
    ys = {}
    for dirn, reverse in ((0, False), (1, True)):
        consts = _ssd_consts(inner, reverse)
        h = jnp.zeros((ctx.nb, M2_GROUPS, M2_STATE, gw), F32)
        for name, st in (("ctx", ctx), ("lat", lat)):
            zx, dts, xbc = acts[name]
            y, h = _ssd_direction(xbc, dts[dirn], p["par"][dirn], p["dskip"] if dirn == 0 else None, h,
                                  ys.get(name), consts, nb=st.nb, seq=st.seq, reverse=reverse)
            ys[name] = y

    def gated_norm(y, z, w):
        return _rms(y * _silu(z), w)

    for name, st in (("ctx", ctx), ("lat", lat)):
        if name == "ctx" and not ctx_out:
            continue
        zx = acts[name][0]
        pro = [_rows_in(ys[name], st.tm, inner), _rows_in(zx, st.tm, inner), _const_in(p["norm_w"], inner)]
        outs[name] = _out_proj(st, pro, gated_norm, p["w_out"], st.mods["g1"], inner)
    return outs


def _rglru_layer(lat, ctx, p, ctx_out):
    width = p["width"]
    outs = {}
    acts = {}
    for name, st in (("ctx", ctx), ("lat", lat)):
        m = st.mods
        (yx,) = _in_proj(st, p["norm1"], m["sc1"], m["sh1"], [p["w_in"]], 2 * width,
                         _pick(2 * width, (1280, 1024, 512, 256, 128)))
        xc = _dwconv(yx, p["conv_w"], p["conv_b"], nb=st.nb, seq=st.seq, col_off=width // LANES,
                     channels=width, act=False)
        acts[name] = (yx, xc)
    rs = {}
    for dirn, reverse in ((0, False), (1, True)):
        h = jnp.zeros((ctx.nb, 1, width), F32)
        for name, st in (("ctx", ctx), ("lat", lat)):
            r, h = _rglru_direction(acts[name][1], p["gate_w"][dirn], p["gate_b"][dirn], p["a_param"][dirn], h,
                                    rs.get(name), nb=st.nb, seq=st.seq, reverse=reverse)
            rs[name] = r
    for name, st in (("ctx", ctx), ("lat", lat)):
        if name == "ctx" and not ctx_out:
            continue
        pro = [_rows_in(acts[name][0], st.tm, width), _rows_in(rs[name], st.tm, width)]
        outs[name] = _out_proj(st, pro, lambda y, r: _gelu_tanh(y) * r, p["w_out"], st.mods["g1"], width)
    return outs


def _s5_layer(lat, ctx, p, ctx_out):
    d = lat.x.shape[1]
    lbs = d // LANES
    sw = p["bb_re"][0].shape[2]
    us = {}
    for name, st in (("ctx", ctx), ("lat", lat)):
        m = st.mods
        us[name] = _norm_mod_rows(st.x, p["norm1"], m["sc1"], m["sh1"], tm=st.tm, tpb=st.tpb)
    ys = {}
    for dirn, reverse in ((0, False), (1, True)):
        h = jnp.zeros((ctx.nb, lbs, 2, sw), F32)
        for name, st in (("ctx", ctx), ("lat", lat)):
            y, h = _s5_direction(us[name], p["bb_re"][dirn], p["bb_im"][dirn], p["c_re"][dirn], p["c_im"][dirn],
                                 p["pw"][dirn], h, ys.get(name), nb=st.nb, seq=st.seq, reverse=reverse)
            ys[name] = y
    outs = {}
    for name, st in (("ctx", ctx), ("lat", lat)):
        if name == "ctx" and not ctx_out:
            continue
        pro = [_rows_in(ys[name], st.tm, d), _rows_in(us[name], st.tm, d), _const_in(p["dskip"], d)]
        epi = [_const_tile_in(p["glu_ba"], d), _const_tile_in(p["glu_bg"], d),
               _tile_in(st.x, st.tm, d), _batch_tile_in(st.mods["g1"], st.tpb, d)]
        (out,) = _fused_matmul(
            pro, [p["glu_wa"], p["glu_wg"]], epi,
            lambda y, u, dsk: _gelu_tanh(y + dsk * u),
            lambda accs, ba, bg, x, g: [x + g * ((accs[0] + ba) * _sigmoid(accs[1] + bg))],
            rows=st.rows, tm=st.tm, k=d, n=d, tn=d, out_dtypes=[F32])
        outs[name] = out
    return outs


def _mla_layer(lat, ctx, p, ctx_out):
    d = lat.x.shape[1]
    qr, kvr = p["q_rank"], p["kv_rank"]
    hp = MLA_HEADS * HEAD_PAD
    tn_h = _pick(hp, (512, 256, 128))
    reps = tn_h // HEAD_PAD
    qkv = {}
    for name, st in (("ctx", ctx), ("lat", lat)):
        m = st.mods
        n_in = p["w_in"].shape[1]
        (lat_all,) = _in_proj(st, p["norm1"], m["sc1"], m["sh1"], [p["w_in"]], n_in, n_in)
        cos_t, sin_t = p["tables"][name]
        tab_spec = lambda a, st=st: (a, (st.tm, HEAD_PAD), lambda i, j: (i % st.tpb, 0))

        def rope_epi(accs, cos, sin, reps=reps):
            cos_r = jnp.concatenate([cos] * reps, axis=1)
            sin_r = jnp.concatenate([sin] * reps, axis=1)
            return [accs[0] * cos_r + accs[1] * sin_r]

        want_q = name == "lat" or ctx_out
        q = None
        if want_q:
            (q,) = _fused_matmul([_rows_in(lat_all, st.tm, qr, 0), _const_in(p["q_norm"], qr)],
                                 [p["w_qa"], p["w_qb"]], [tab_spec(cos_t), tab_spec(sin_t)],
                                 _rms, rope_epi, rows=st.rows, tm=st.tm, k=qr, n=hp, tn=tn_h, out_dtypes=[BF16])

        def kv_epi(accs, kra, krb, cos, sin, ones, reps=reps):
            kr = kra * cos + krb * sin
            return [accs[0] + jnp.concatenate([kr] * reps, axis=1), accs[1] + ones]

        kra_blk = (qr // LANES)
        kv_blk = (qr + LANES) // kvr
        krb_blk = (qr + LANES + kvr) // LANES
        k, v = _fused_matmul(
            [_rows_in(lat_all, st.tm, kvr, kv_blk), _const_in(p["kv_norm"], kvr)],
            [p["w_k"], p["w_v"]],
            [(lat_all, (st.tm, LANES), lambda i, j: (i, kra_blk)),
             (lat_all, (st.tm, LANES), lambda i, j: (i, krb_blk)),
             tab_spec(cos_t), tab_spec(sin_t), _const_tile_in(p["v_ones"], tn_h)],
            _rms, kv_epi, rows=st.rows, tm=st.tm, k=kvr, n=hp, tn=tn_h, out_dtypes=[BF16, BF16])
        qkv[name] = (q, k, v)
    scale = (MLA_NOPE + MLA_ROPE) ** -0.5
    nb = lat.nb
    kv_c = (qkv["ctx"][1], qkv["ctx"][2], ctx.seq)
    kv_l = (qkv["lat"][1], qkv["lat"][2], lat.seq)
    outs = {}
    o_l = _attention(qkv["lat"][0], [kv_c, kv_l], nb=nb, lq=lat.seq, scale=scale)
    outs["lat"] = _out_proj(lat, [_rows_in(o_l, lat.tm, hp)], lambda a: a, p["w_out"], lat.mods["g1"], hp)
    if ctx_out:
        o_c = _attention(qkv["ctx"][0], [kv_c], nb=nb, lq=ctx.seq, scale=scale)
        outs["ctx"] = _out_proj(ctx, [_rows_in(o_c, ctx.tm, hp)], lambda a: a, p["w_out"], ctx.mods["g1"], hp)
    return outs


def _pad_cols(w, n):
    return jnp.pad(w, ((0, 0), (0, n - w.shape[1])))


def _mamba_params(in_w, conv_w, conv_b, dt_bias, a_log, d_skip, norm_w, out_w):
    d, proj = in_w.shape
    heads = dt_bias.shape[1]
    inner = heads * M2_HEADDIM
    conv_ch = inner + 2 * M2_GROUPS * M2_STATE
    par = []
    for dirn in range(2):
        rows = jnp.zeros((SUBLANES, LANES), F32)
        rows = rows.at[0, :heads].set(dt_bias[dirn].astype(F32)).at[1, :heads].set(a_log[dirn].astype(F32))
        par.append(rows)
    off = inner + conv_ch
    return dict(
        inner=inner, conv_ch=conv_ch,
        w_zx=in_w[:, :off].astype(BF16),
        w_dt0=_pad_cols(in_w[:, off:off + heads], LANES).astype(BF16),
        w_dt1=_pad_cols(in_w[:, off + heads:off + 2 * heads], LANES).astype(BF16),
        conv_w=conv_w.astype(F32), conv_b=conv_b.astype(F32)[None, :],
        par=par, dskip=jnp.repeat(d_skip.astype(F32), M2_HEADDIM)[None, :],
        norm_w=norm_w.astype(F32)[None, :], w_out=out_w.astype(BF16))


def _rglru_params(in_w, conv_w, conv_b, gate_w, gate_b, a_param, out_w):
    width = conv_w.shape[1]
    return dict(
        width=width, w_in=in_w.astype(BF16), conv_w=conv_w.astype(F32), conv_b=conv_b.astype(F32)[None, :],
        gate_w=[gate_w[dirn].astype(BF16) for dirn in range(2)],
        gate_b=[gate_b[dirn].astype(F32)[:, None, :] for dirn in range(2)],
        a_param=[a_param[dirn].astype(F32)[None, :] for dirn in range(2)],
        w_out=out_w.astype(BF16))


def _block_diag(m, per):
    g, r, c = m.shape
    m = m.reshape(g // per, per, r, c)
    eye = jnp.eye(per, dtype=m.dtype)
    return jnp.einsum("qarc,ab->qarbc", m, eye).reshape(g // per, per * r, per * c)


def _s5_params(lam_re, lam_im, log_step, b_re, b_im, c_re, c_im, d_skip, glu_w, glu_b):
    per = LANES // S5_GROUP
    br, bi = b_re.astype(F32), b_im.astype(F32)
    out = dict(bb_re=[], bb_im=[], c_re=[], c_im=[], pw=[])
    for dirn in range(2):
        lr = jnp.minimum(lam_re[dirn].astype(F32), -1e-4)
        li = lam_im[dirn].astype(F32)
        step = jnp.exp(log_step[dirn].astype(F32))[:, None]
        mag = jnp.exp(lr * step)
        abr, abi = mag * jnp.cos(li * step), mag * jnp.sin(li * step)
        den = lr * lr + li * li
        zr = ((abr - 1.0) * lr + abi * li) / den
        zi = (abi * lr - (abr - 1.0) * li) / den
        bbr = zr[..., None] * br - zi[..., None] * bi
        bbi = zr[..., None] * bi + zi[..., None] * br
        out["bb_re"].append(_block_diag(jnp.swapaxes(bbr, 1, 2), per).astype(BF16))
        out["bb_im"].append(_block_diag(jnp.swapaxes(bbi, 1, 2), per).astype(BF16))
        out["c_re"].append(_block_diag(jnp.swapaxes(c_re[dirn].astype(F32), 1, 2), per).astype(BF16))
        out["c_im"].append(_block_diag(jnp.swapaxes(c_im[dirn].astype(F32), 1, 2), per).astype(BF16))
        pr, pi = [abr], [abi]
        for _ in range(SUBLANES - 1):
            pr, pi = pr + [pr[-1] * abr - pi[-1] * abi], pi + [pr[-1] * abi + pi[-1] * abr]
        g = abr.shape[0]
        flat = lambda a: a.reshape(g // per, per * a.shape[1])
        t = jnp.arange(SUBLANES)[None, :, None]
        tiles = []
        for dist in (1, 2, 4):
            valid = (t < SUBLANES - dist) if dirn == 1 else (t >= dist)
            tiles.append(jnp.where(valid, flat(pr[dist - 1])[:, None, :], 0.0))
            tiles.append(jnp.where(valid, flat(pi[dist - 1])[:, None, :], 0.0))
        order = list(range(SUBLANES))[::-1] if dirn == 1 else list(range(SUBLANES))
        tiles.append(jnp.stack([flat(pr[o]) for o in order], axis=1))
        tiles.append(jnp.stack([flat(pi[o]) for o in order], axis=1))
        out["pw"].append(jnp.stack(tiles, axis=1).astype(F32))
    d = glu_w.shape[0]
    out.update(dskip=d_skip.astype(F32)[None, :],
               glu_wa=glu_w[:, :d].astype(BF16), glu_wg=glu_w[:, d:].astype(BF16),
               glu_ba=glu_b[:d].astype(F32)[None, :], glu_bg=glu_b[d:].astype(F32)[None, :])
    return out


def _head_pad_cols(w, per_head, take):
    kdim = w.shape[0]
    w = w.reshape(kdim, MLA_HEADS, per_head)[:, :, take]
    w = jnp.pad(w, ((0, 0), (0, 0), (0, HEAD_PAD - w.shape[2])))
    return w.reshape(kdim, MLA_HEADS * HEAD_PAD)


def _rope_swap(w):
    idx = jnp.arange(MLA_ROPE)
    blk, pos = idx // (2 * ROPE_FREQ), idx % (2 * ROPE_FREQ)
    return w[..., blk * 2 * ROPE_FREQ + (pos + ROPE_FREQ) % (2 * ROPE_FREQ)]


def _mla_params(in_w, q_norm_w, kv_norm_w, qb_w, kvb_w, out_w, seq, ctx_len):
    d = in_w.shape[0]
    qr, kvr = q_norm_w.shape[0], kv_norm_w.shape[0]
    dq = MLA_NOPE + MLA_ROPE
    w_q, w_kv, w_kr = in_w[:, :qr], in_w[:, qr:qr + kvr], in_w[:, qr + kvr:]

    def place(w):
        return jnp.pad(w, ((0, 0), (MLA_NOPE, HEAD_PAD - dq)))

    w_in = jnp.concatenate([w_q, place(w_kr), w_kv, place(_rope_swap(w_kr))], axis=1).astype(BF16)
    qb = qb_w.reshape(qr, MLA_HEADS, dq)
    qa = jnp.pad(qb, ((0, 0), (0, 0), (0, HEAD_PAD - dq))).reshape(qr, -1)
    qb_sw = jnp.pad(_rope_swap(qb[:, :, MLA_NOPE:]), ((0, 0), (0, 0), (MLA_NOPE, HEAD_PAD - dq))).reshape(qr, -1)
    per = MLA_NOPE + MLA_V
    w_k = _head_pad_cols(kvb_w, per, slice(0, MLA_NOPE))
    w_v = _head_pad_cols(kvb_w, per, slice(MLA_NOPE, per))
    w_out = jnp.pad(out_w.reshape(MLA_HEADS, MLA_V, d), ((0, 0), (0, HEAD_PAD - MLA_V), (0, 0))).reshape(-1, d)
    rows = seq // GRID_W
    row = jnp.repeat(jnp.arange(rows, dtype=F32), GRID_W)
    col = jnp.tile(jnp.arange(GRID_W, dtype=F32), rows)
    inv_freq = ROPE_BASE ** (-jnp.arange(ROPE_FREQ, dtype=F32) / ROPE_FREQ)
    ang = jnp.stack([row[:, None] * inv_freq, col[:, None] * inv_freq], axis=1)
    cos, sin = jnp.cos(ang), jnp.sin(ang)
    cos32 = jnp.concatenate([cos, cos], axis=2).reshape(seq, MLA_ROPE)
    sin32 = jnp.concatenate([-sin, sin], axis=2).reshape(seq, MLA_ROPE)
    ones = jnp.ones((seq, MLA_NOPE), F32)
    cos_l = jnp.pad(jnp.concatenate([ones, cos32], axis=1), ((0, 0), (0, HEAD_PAD - dq)))
    sin_l = jnp.pad(sin32, ((0, 0), (MLA_NOPE, HEAD_PAD - dq)))
    cos_c = jnp.pad(jnp.ones((ctx_len, dq), F32), ((0, 0), (0, HEAD_PAD - dq)))
    sin_c = jnp.zeros((ctx_len, HEAD_PAD), F32)
    return dict(q_rank=qr, kv_rank=kvr, w_in=w_in, q_norm=q_norm_w.astype(F32)[None, :],
                kv_norm=kv_norm_w.astype(F32)[None, :], w_qa=qa.astype(BF16), w_qb=qb_sw.astype(BF16),
                w_k=w_k.astype(BF16), w_v=w_v.astype(BF16), w_out=w_out.astype(BF16),
                v_ones=jnp.tile((jnp.arange(HEAD_PAD) == V_ONES_LANE).astype(F32), MLA_HEADS)[None, :],
                tables=dict(lat=(cos_l, sin_l), ctx=(cos_c, sin_c)))


def _modulation(c, c_ctx, ada_w, ada_b):
    nb, d = c.shape
    rows = jnp.zeros((SUBLANES, d), F32).at[:nb].set(c).at[nb].set(c_ctx)
    n = ada_w.shape[1]
    (mod,) = _fused_matmul([(rows, (SUBLANES, d), lambda i, j: (0, 0))], [ada_w.astype(BF16)],
                           [_const_tile_in(ada_b.astype(F32)[None, :], _pick(n, (1024, 512, 256, 128)))],
                           _silu, lambda accs, b: [accs[0] + b], rows=SUBLANES, tm=SUBLANES, k=d, n=n,
                           tn=_pick(n, (1024, 512, 256, 128)), out_dtypes=[F32])
    names = ("sh1", "sc1", "g1", "sh2", "sc2", "g2")
    parts = jnp.split(mod, 6, axis=1)
    lat = {nm: pt[:nb][:, None, :] for nm, pt in zip(names, parts)}
    ctx = {nm: pt[nb:nb + 1][:, None, :] for nm, pt in zip(names, parts)}
    return lat, ctx


def kernel(x, c, ctx, c_ctx, ada_w, ada_b, norm1_w, norm2_w, ffn_w13, ffn_w2, m2_in_w, m2_conv_w, m2_conv_b, m2_dt_bias, m2_a_log, m2_d, m2_norm_w, m2_out_w, lru_in_w, lru_conv_w, lru_conv_b, lru_gate_w, lru_gate_b, lru_a_param, lru_out_w, s5_lambda_re, s5_lambda_im, s5_log_step, s5_b_re, s5_b_im, s5_c_re, s5_c_im, s5_d, s5_glu_w, s5_glu_b, mla_in_w, mla_q_norm_w, mla_kv_norm_w, mla_qb_w, mla_kvb_w, mla_out_w, final_norm_w):
    nb, seq, d = x.shape
    ctx_len = ctx.shape[1]
    depth = ada_w.shape[0]
    hidden = ffn_w2.shape[1]
    xl = x.reshape(nb * seq, d).astype(F32)
    xc = ctx.reshape(nb * ctx_len, d).astype(F32)
    for i in range(depth):
        kind, j = i % N_MIXERS, i // N_MIXERS
        ctx_out = i < depth - 1
        mods_l, mods_c = _modulation(c.astype(F32), c_ctx.astype(F32), ada_w[i], ada_b[i])
        lat_s = _Stream(xl, nb, seq, mods_l)
        ctx_s = _Stream(xc, nb, ctx_len, mods_c)
        n1 = norm1_w[i].astype(F32)[None, :]
        if kind == 0:
            p = _mamba_params(m2_in_w[j], m2_conv_w[j], m2_conv_b[j], m2_dt_bias[j], m2_a_log[j], m2_d[j],
                              m2_norm_w[j], m2_out_w[j])
            p["norm1"] = n1
            outs = _mamba_layer(lat_s, ctx_s, p, ctx_out)
        elif kind == 1:
            p = _rglru_params(lru_in_w[j], lru_conv_w[j], lru_conv_b[j], lru_gate_w[j], lru_gate_b[j],
                              lru_a_param[j], lru_out_w[j])
            p["norm1"] = n1
            outs = _rglru_layer(lat_s, ctx_s, p, ctx_out)
        elif kind == 2:
            p = _s5_params(s5_lambda_re[j], s5_lambda_im[j], s5_log_step[j], s5_b_re[j], s5_b_im[j],
                           s5_c_re[j], s5_c_im[j], s5_d[j], s5_glu_w[j], s5_glu_b[j])
            p["norm1"] = n1
            outs = _s5_layer(lat_s, ctx_s, p, ctx_out)
        else:
            p = _mla_params(mla_in_w[j], mla_q_norm_w[j], mla_kv_norm_w[j], mla_qb_w[j], mla_kvb_w[j],
                            mla_out_w[j], seq, ctx_len)
            p["norm1"] = n1
            outs = _mla_layer(lat_s, ctx_s, p, ctx_out)
        n2 = norm2_w[i].astype(F32)[None, :]
        w1 = ffn_w13[i][:, :hidden].astype(BF16)
        w3 = ffn_w13[i][:, hidden:].astype(BF16)
        w2 = ffn_w2[i].astype(BF16)
        lat_s = _Stream(outs["lat"], nb, seq, mods_l)
        xl = _ffn(lat_s, n2, mods_l["sc2"], mods_l["sh2"], mods_l["g2"], w1, w3, w2)
        if ctx_out:
            ctx_s = _Stream(outs["ctx"], nb, ctx_len, mods_c)
            xc = _ffn(ctx_s, n2, mods_c["sc2"], mods_c["sh2"], mods_c["g2"], w1, w3, w2)
    zero = jnp.zeros((1, 1, d), F32)
    lat_s = _Stream(xl, nb, seq, None)
    out = _norm_mod_rows(xl, final_norm_w.astype(F32)[None, :], zero, zero, tm=lat_s.tm, tpb=lat_s.tpb)
    return out.reshape(nb, seq, d).astype(x.dtype)
```

```python
import functools
import math

import jax
import jax.numpy as jnp
from jax import lax
from jax.experimental import pallas as pl
from jax.experimental.pallas import tpu as pltpu

F32 = jnp.float32
BF16 = jnp.bfloat16

LANES = 128
SUBLANES = 8
VMEM_BUDGET_BYTES = 56 * 2**20

GRID_W = 64
N_MIXERS = 4
NORM_EPS = 1e-6
CONV_WIDTH = 4
CONV_PAD_LEFT = 2
M2_HEADDIM = 64
M2_GROUPS = 4
M2_STATE = 128
SSD_CHUNK = 128
LRU_BLOCK = 128
LRU_C = 8.0
S5_GROUP = 16
S5_STATE = 64
S5_TILE = 8
MLA_HEADS = 16
MLA_NOPE = 64
MLA_ROPE = 32
MLA_V = 64
ROPE_FREQ = MLA_ROPE // 4
ROPE_BASE = 10000.0
HEAD_PAD = 128
MAX_FOLD_ROWS = 64
V_ONES_LANE = MLA_V


def _pick(n, cands):
    for c in cands:
        if n % c == 0:
            return c
    raise ValueError(f"no tile in {cands} divides {n}")


def _params(sem, est_bytes):
    limit = int(min(max(2 * est_bytes, 32 * 2**20), VMEM_BUDGET_BYTES))
    return pltpu.CompilerParams(dimension_semantics=sem, vmem_limit_bytes=limit)


def _nbytes(shape, dtype):
    return math.prod(s for s in shape if s is not None) * jnp.dtype(dtype).itemsize


def _sigmoid(x):
    return 1.0 / (1.0 + jnp.exp(-x))


def _silu(x):
    return x * _sigmoid(x)


def _softplus(x):
    return jnp.maximum(x, 0.0) + jnp.log1p(jnp.exp(-jnp.abs(x)))


def _gelu_tanh(x):
    return 0.5 * x * (1.0 + jnp.tanh(math.sqrt(2.0 / math.pi) * (x + 0.044715 * (x * x * x))))


def _rms(x, w):
    return x * lax.rsqrt(jnp.mean(x * x, axis=-1, keepdims=True) + NORM_EPS) * w


def _norm_mod(x, w, sc, sh):
    return _rms(x, w) * (1.0 + sc) + sh


def _split3(q):
    q1 = q.astype(BF16)
    r1 = q - q1.astype(F32)
    q2 = r1.astype(BF16)
    q3 = (r1 - q2.astype(F32)).astype(BF16)
    return q1, q2, q3


def _select_cols(q, e, parts=3):
    return sum(jnp.dot(p, e, preferred_element_type=F32) for p in _split3(q)[:parts])


def _select_rows(t, q):
    return sum(jnp.dot(t, p, preferred_element_type=F32) for p in _split3(q))


def _fused_matmul(pro_ins, weights, epi_ins, prologue, epilogue, *, rows, tm, k, n, tn, out_dtypes):
    n_p, n_w, n_e, n_o = len(pro_ins), len(weights), len(epi_ins), len(out_dtypes)

    def body(*refs):
        p = refs[:n_p]
        w = refs[n_p:n_p + n_w]
        e = refs[n_p + n_w:n_p + n_w + n_e]
        o = refs[n_p + n_w + n_e:n_p + n_w + n_e + n_o]
        a_scr = refs[-1]

        @pl.when(pl.program_id(1) == 0)
        def _():
            a_scr[...] = prologue(*[r[...] for r in p]).astype(BF16)

        a = a_scr[...]
        accs = [jnp.dot(a, wr[...], preferred_element_type=F32) for wr in w]
        outs = epilogue(accs, *[r[...] for r in e])
        for o_ref, val in zip(o, outs):
            o_ref[...] = val.astype(o_ref.dtype)

    in_specs = ([pl.BlockSpec(bs, im) for (_, bs, im) in pro_ins]
                + [pl.BlockSpec((k, tn), lambda i, j: (0, j)) for _ in weights]
                + [pl.BlockSpec(bs, im) for (_, bs, im) in epi_ins])
    out_specs = [pl.BlockSpec((tm, tn), lambda i, j: (i, j)) for _ in out_dtypes]
    est = (sum(_nbytes(bs, a.dtype) for (a, bs, _) in pro_ins + epi_ins) * 2
           + n_w * k * tn * 2 * 2 + sum(tm * tn * jnp.dtype(d).itemsize for d in out_dtypes) * 2
           + tm * k * 2 + (n_w + 2) * tm * tn * 4 + tm * k * 8)
    outs = pl.pallas_call(
        body,
        out_shape=[jax.ShapeDtypeStruct((rows, n), d) for d in out_dtypes],
        grid=(rows // tm, n // tn),
        in_specs=in_specs,
        out_specs=out_specs,
        scratch_shapes=[pltpu.VMEM((tm, k), BF16)],
        compiler_params=_params(("parallel", "arbitrary"), est),
        name=f"mm_r{rows}_k{k}_n{n}x{n_w}",
    )(*[a for (a, _, _) in pro_ins], *weights, *[a for (a, _, _) in epi_ins])
    return outs


def _rows_in(a, tm, width, col_block=0):
    return (a, (tm, width), lambda i, j: (i, col_block))


def _batch_in(a, tpb, width):
    if a.shape[0] == 1:
        return (a, (None, 1, width), lambda i, j: (0, 0, 0))
    return (a, (None, 1, width), lambda i, j: (i // tpb, 0, 0))


def _const_in(a, width):
    return (a, (1, width), lambda i, j: (0, 0))


def _tile_in(a, tm, tn):
    return (a, (tm, tn), lambda i, j: (i, j))


def _batch_tile_in(a, tpb, tn):
    if a.shape[0] == 1:
        return (a, (None, 1, tn), lambda i, j: (0, 0, j))
    return (a, (None, 1, tn), lambda i, j: (i // tpb, 0, j))


def _const_tile_in(a, tn):
    return (a, (1, tn), lambda i, j: (0, j))


def _norm_mod_rows(x, w, sc, sh, *, tm, tpb):
    rows, d = x.shape

    def body(x_ref, w_ref, sc_ref, sh_ref, o_ref):
        o_ref[...] = _norm_mod(x_ref[...], w_ref[...], sc_ref[...], sh_ref[...])

    def bidx(a):
        if a.shape[0] == 1:
            return lambda i: (0, 0, 0)
        return lambda i: (i // tpb, 0, 0)

    return pl.pallas_call(
        body,
        out_shape=jax.ShapeDtypeStruct((rows, d), F32),
        grid=(rows // tm,),
        in_specs=[pl.BlockSpec((tm, d), lambda i: (i, 0)),
                  pl.BlockSpec((1, d), lambda i: (0, 0)),
                  pl.BlockSpec((None, 1, d), bidx(sc)),
                  pl.BlockSpec((None, 1, d), bidx(sh))],
        out_specs=pl.BlockSpec((tm, d), lambda i: (i, 0)),
        compiler_params=_params(("parallel",), 6 * tm * d * 4),
        name="norm_mod",
    )(x, w, sc, sh)


def _dwconv(x, w, b, *, nb, seq, col_off, channels, act):
    cb = channels // LANES
    rc = _pick(seq, (512, 256, 128))
    pad = SUBLANES

    def body(x_ref, w_ref, b_ref, o_ref, p_scr):
        p_scr[0:pad, :] = jnp.zeros((pad, LANES), F32)
        p_scr[pad + seq:pad + seq + pad, :] = jnp.zeros((pad, LANES), F32)
        p_scr[pad:pad + seq, :] = x_ref[...]
        wv = w_ref[...]
        bv = b_ref[...]

        def step(i, carry):
            r0 = pl.multiple_of(i * rc, rc)
            y = bv
            for t in range(CONV_WIDTH):
                y = y + wv[t:t + 1, :] * p_scr[pl.ds(r0 + pad - CONV_PAD_LEFT + t, rc), :]
            if act:
                y = _silu(y)
            o_ref[pl.ds(r0, rc), :] = y
            return carry

        lax.fori_loop(0, seq // rc, step, 0)

    return pl.pallas_call(
        body,
        out_shape=jax.ShapeDtypeStruct((nb * seq, channels), F32),
        grid=(nb, cb),
        in_specs=[pl.BlockSpec((seq, LANES), lambda bi, ci: (bi, col_off + ci)),
                  pl.BlockSpec((CONV_WIDTH, LANES), lambda bi, ci: (0, ci)),
                  pl.BlockSpec((1, LANES), lambda bi, ci: (0, ci))],
        out_specs=pl.BlockSpec((seq, LANES), lambda bi, ci: (bi, ci)),
        scratch_shapes=[pltpu.VMEM((seq + 2 * pad, LANES), F32)],
        compiler_params=_params(("parallel", "parallel"), 5 * seq * LANES * 4),
        name="dwconv",
    )(x, w, b)


def _ssd_direction(xbc, dt, par, dskip, h0, yprev, consts, *, nb, seq, reverse):
    tri, e_head, e_wide = consts
    q = SSD_CHUNK
    nc = seq // q
    inner = xbc.shape[1] - 2 * M2_GROUPS * M2_STATE
    gw = inner // M2_GROUPS
    heads = inner // M2_HEADDIM
    hpg = heads // M2_GROUPS
    has_prev = yprev is not None
    has_skip = dskip is not None

    def body(*refs):
        it = iter(refs)
        xbc_ref, dt_ref, par_ref = next(it), next(it), next(it)
        dsk_ref = next(it) if has_skip else None
        h0_ref = next(it)
        yp_ref = next(it) if has_prev else None
        tri_ref, eh_ref, ew_ref = next(it), next(it), next(it)
        y_ref, hout_ref, h_scr = next(it), next(it), next(it)
        c = pl.program_id(1)

        @pl.when(c == 0)
        def _():
            h_scr[...] = h0_ref[...]

        dtv = _softplus(dt_ref[...] + par_ref[0:1, :])
        da = dtv * (-jnp.exp(par_ref[1:2, :]))
        acs = _select_rows(tri_ref[...], da)
        acs_t = acs.T
        dt_t = dtv.T
        last = acs[0:1, :] if reverse else acs[q - 1:q, :]
        eh = eh_ref[...]
        w_e = _select_cols(dtv * jnp.exp(last - acs), eh, parts=2)
        cdec_e = _select_cols(jnp.broadcast_to(jnp.exp(last), (SUBLANES, LANES)), eh)[0:1, :]
        acol = _select_cols(acs, ew_ref[...])
        xs = xbc_ref[:, 0:inner]
        x_b = xs.astype(BF16)
        x_d = (xs * w_e).astype(BF16)
        row = lax.broadcasted_iota(jnp.int32, (q, q), 0)
        col = lax.broadcasted_iota(jnp.int32, (q, q), 1)
        mask = (col >= row) if reverse else (col <= row)
        lane = lax.broadcasted_iota(jnp.int32, (q, LANES), 1)
        for g in range(M2_GROUPS):
            bm = xbc_ref[:, inner + g * M2_STATE:inner + (g + 1) * M2_STATE]
            cm = xbc_ref[:, inner + (M2_GROUPS + g) * M2_STATE:inner + (M2_GROUPS + g + 1) * M2_STATE]
            scores = lax.dot_general(cm.astype(BF16), bm.astype(BF16), (((1,), (1,)), ((), ())),
                                     preferred_element_type=F32)
            h_t = h_scr[g]
            h_b = h_t.astype(BF16)
            s_t = jnp.dot(bm.T.astype(BF16), x_d[:, g * gw:(g + 1) * gw], preferred_element_type=F32)
            h_scr[g] = h_t * cdec_e[:, g * gw:(g + 1) * gw] + s_t
            for pr in range(hpg // 2):
                k0 = g * hpg + 2 * pr
                c0 = k0 * M2_HEADDIM
                rhs = jnp.concatenate([x_b[:, c0:c0 + LANES], h_b[:, 2 * pr * M2_HEADDIM:2 * pr * M2_HEADDIM + LANES]],
                                      axis=0)
                ys = []
                for k in (k0, k0 + 1):
                    a_col = acol[:, k * LANES:(k + 1) * LANES]
                    decay = jnp.exp(jnp.where(mask, a_col - acs_t[k:k + 1, :], -jnp.inf))
                    m = (scores * decay * dt_t[k:k + 1, :]).astype(BF16)
                    c_e = (cm * jnp.exp(a_col)).astype(BF16)
                    ys.append(jnp.dot(jnp.concatenate([m, c_e], axis=1), rhs, preferred_element_type=F32))
                y = jnp.where(lane < M2_HEADDIM, ys[0], ys[1])
                if has_skip:
                    y = y + dsk_ref[:, c0:c0 + LANES] * xs[:, c0:c0 + LANES]
                if has_prev:
                    y = y + yp_ref[:, c0:c0 + LANES]
                y_ref[:, c0:c0 + LANES] = y

        @pl.when(c == nc - 1)
        def _():
            hout_ref[...] = h_scr[...]

    def rb(bi, ci):
        return bi * nc + ((nc - 1 - ci) if reverse else ci)

    width = xbc.shape[1]
    st_shape = (M2_GROUPS, M2_STATE, gw)
    ins = [xbc, dt, par]
    in_specs = [pl.BlockSpec((q, width), lambda bi, ci: (rb(bi, ci), 0)),
                pl.BlockSpec((q, LANES), lambda bi, ci: (rb(bi, ci), 0)),
                pl.BlockSpec((SUBLANES, LANES), lambda bi, ci: (0, 0))]
    if has_skip:
        ins.append(dskip)
        in_specs.append(pl.BlockSpec((1, inner), lambda bi, ci: (0, 0)))
    ins.append(h0)
    in_specs.append(pl.BlockSpec((None,) + st_shape, lambda bi, ci: (bi, 0, 0, 0)))
    if has_prev:
        ins.append(yprev)
        in_specs.append(pl.BlockSpec((q, inner), lambda bi, ci: (rb(bi, ci), 0)))
    ins += [tri, e_head, e_wide]
    in_specs += [pl.BlockSpec(tri.shape, lambda bi, ci: (0, 0)),
                 pl.BlockSpec(e_head.shape, lambda bi, ci: (0, 0)),
                 pl.BlockSpec(e_wide.shape, lambda bi, ci: (0, 0))]
    est = (q * width * 4 * 2 + q * inner * 4 * 4 + 3 * math.prod(st_shape) * 4 * 2
           + (e_head.size + e_wide.size) * 2 * 2 + 12 * q * inner * 4 + q * heads * LANES * 4)
    y, h_last = pl.pallas_call(
        body,
        out_shape=[jax.ShapeDtypeStruct((nb * seq, inner), F32),
                   jax.ShapeDtypeStruct((nb,) + st_shape, F32)],
        grid=(nb, nc),
        in_specs=in_specs,
        out_specs=[pl.BlockSpec((q, inner), lambda bi, ci: (rb(bi, ci), 0)),
                   pl.BlockSpec((None,) + st_shape, lambda bi, ci: (bi, 0, 0, 0))],
        scratch_shapes=[pltpu.VMEM(st_shape, F32)],
        compiler_params=_params(("parallel", "arbitrary"), est),
        name="ssd_bwd" if reverse else "ssd_fwd",
    )(*ins)
    return y, h_last


def _ssd_consts(inner, reverse):
    heads = inner // M2_HEADDIM
    r = jnp.arange(SSD_CHUNK)
    tri = (r[None, :] >= r[:, None]) if reverse else (r[None, :] <= r[:, None])
    hk = jnp.arange(LANES)[:, None]
    e_head = (hk == (jnp.arange(inner)[None, :] // M2_HEADDIM))
    e_wide = (hk == (jnp.arange(heads * LANES)[None, :] // LANES))
    return tri.astype(BF16), e_head.astype(BF16), e_wide.astype(BF16)


def _tile_scan(a, u, reverse):
    row = lax.broadcasted_iota(jnp.int32, a.shape, 0)
    for d in (1, 2, 4):
        shift = (SUBLANES - d) if reverse else d
        valid = (row < SUBLANES - d) if reverse else (row >= d)
        a_s = jnp.where(valid, pltpu.roll(a, shift, 0), 1.0)
        u_s = jnp.where(valid, pltpu.roll(u, shift, 0), 0.0)
        u = a * u_s + u
        a = a * a_s
    return a, u


def _rglru_direction(xc, gate_w, gate_b, a_param, h0, rprev, *, nb, seq, reverse):
    width = xc.shape[1]
    nblk = width // LRU_BLOCK
    tb = _pick(seq, (256, 128))
    nt = seq // tb
    has_prev = rprev is not None

    def body(*refs):
        it = iter(refs)
        x_ref, gw_ref, gb_ref, ap_ref, h0_ref = next(it), next(it), next(it), next(it), next(it)
        rp_ref = next(it) if has_prev else None
        r_ref, hout_ref, a_scr, u_scr, h_scr = next(it), next(it), next(it), next(it), next(it)
        tstep = pl.program_id(1)

        @pl.when(tstep == 0)
        def _():
            h_scr[...] = jnp.broadcast_to(h0_ref[...], (SUBLANES, width))

        log_base = -LRU_C * _softplus(-ap_ref[...])
        for nbk in range(nblk):
            cs = slice(nbk * LRU_BLOCK, (nbk + 1) * LRU_BLOCK)
            xb = x_ref[:, cs]
            g = jnp.dot(xb.astype(BF16), gw_ref[nbk], preferred_element_type=F32) + gb_ref[nbk]
            g = _sigmoid(g)
            log_a = g[:, :LRU_BLOCK] * log_base[:, cs]
            a = jnp.exp(log_a)
            mult = jnp.sqrt(jnp.maximum(1.0 - a * a, 0.0))
            a_scr[:, cs] = a
            u_scr[:, cs] = xb * g[:, LRU_BLOCK:] * mult

        def step(i, h):
            grp = (tb // SUBLANES - 1 - i) if reverse else i
            r0 = pl.multiple_of(grp * SUBLANES, SUBLANES)
            a_c, h_loc = _tile_scan(a_scr[pl.ds(r0, SUBLANES), :], u_scr[pl.ds(r0, SUBLANES), :], reverse)
            h_new = a_c * h + h_loc
            out = h_new
            if has_prev:
                out = out + rp_ref[pl.ds(r0, SUBLANES), :]
            r_ref[pl.ds(r0, SUBLANES), :] = out
            edge = h_new[0:1, :] if reverse else h_new[SUBLANES - 1:SUBLANES, :]
            return jnp.broadcast_to(edge, (SUBLANES, width))

        h_fin = lax.fori_loop(0, tb // SUBLANES, step, h_scr[...])
        h_scr[...] = h_fin

        @pl.when(tstep == nt - 1)
        def _():
            hout_ref[...] = h_fin[0:1, :]

    def rb(bi, ti):
        return bi * nt + ((nt - 1 - ti) if reverse else ti)

    ins = [xc, gate_w, gate_b, a_param, h0]
    in_specs = [pl.BlockSpec((tb, width), lambda bi, ti: (rb(bi, ti), 0)),
                pl.BlockSpec(gate_w.shape, lambda bi, ti: (0, 0, 0)),
                pl.BlockSpec(gate_b.shape, lambda bi, ti: (0, 0, 0)),
                pl.BlockSpec((1, width), lambda bi, ti: (0, 0)),
                pl.BlockSpec((None, 1, width), lambda bi, ti: (bi, 0, 0))]
    if has_prev:
        ins.append(rprev)
        in_specs.append(pl.BlockSpec((tb, width), lambda bi, ti: (rb(bi, ti), 0)))
    est = 10 * tb * width * 4 + gate_w.size * 2 * 2
    r, h_last = pl.pallas_call(
        body,
        out_shape=[jax.ShapeDtypeStruct((nb * seq, width), F32),
                   jax.ShapeDtypeStruct((nb, 1, width), F32)],
        grid=(nb, nt),
        in_specs=in_specs,
        out_specs=[pl.BlockSpec((tb, width), lambda bi, ti: (rb(bi, ti), 0)),
                   pl.BlockSpec((None, 1, width), lambda bi, ti: (bi, 0, 0))],
        scratch_shapes=[pltpu.VMEM((tb, width), F32), pltpu.VMEM((tb, width), F32),
                        pltpu.VMEM((SUBLANES, width), F32)],
        compiler_params=_params(("parallel", "arbitrary"), est),
        name="rglru_bwd" if reverse else "rglru_fwd",
    )(*ins)
    return r, h_last


def _s5_bidir(u, w, pw_f, pw_b, h_in, *, nb, seq):
    d_model = u.shape[1]
    lbs = d_model // LANES
    sw = pw_f.shape[3]
    m_t = seq // S5_TILE
    wide = S5_TILE * LANES

    def scan(s_scr, pw_ref, c_re, c_im, reverse):
        row = lax.broadcasted_iota(jnp.int32, (SUBLANES, sw), 0)

        def step(i, carry):
            hr, hi = carry
            grp = (m_t // SUBLANES - 1 - i) if reverse else i
            r0 = pl.multiple_of(grp * SUBLANES, SUBLANES)
            xr = s_scr[pl.ds(r0, SUBLANES), 0:sw]
            xi = s_scr[pl.ds(r0, SUBLANES), sw:2 * sw]
            for n_step, dist in enumerate((1, 2, 4)):
                shift = (SUBLANES - dist) if reverse else dist
                pr = pw_ref[2 * n_step]
                pi = pw_ref[2 * n_step + 1]
                sr = pltpu.roll(xr, shift, 0)
                si = pltpu.roll(xi, shift, 0)
                xr, xi = xr + (pr * sr - pi * si), xi + (pr * si + pi * sr)
            nr = xr + (pw_ref[6] * hr - pw_ref[7] * hi)
            ni = xi + (pw_ref[6] * hi + pw_ref[7] * hr)
            edge = (SUBLANES - 1) if reverse else 0
            back = 1 if not reverse else SUBLANES - 1
            s_scr[pl.ds(r0, SUBLANES), 0:sw] = jnp.where(row == edge, hr, pltpu.roll(nr, back, 0))
            s_scr[pl.ds(r0, SUBLANES), sw:2 * sw] = jnp.where(row == edge, hi, pltpu.roll(ni, back, 0))
            e = 0 if reverse else SUBLANES - 1
            return (jnp.broadcast_to(nr[e:e + 1, :], (SUBLANES, sw)),
                    jnp.broadcast_to(ni[e:e + 1, :], (SUBLANES, sw)))

        init = (jnp.broadcast_to(c_re, (SUBLANES, sw)), jnp.broadcast_to(c_im, (SUBLANES, sw)))
        return lax.fori_loop(0, m_t // SUBLANES, step, init)

    def body(u_ref, t_ref, pf_ref, pb_ref, of_ref, ob_ref, pwf_ref, pwb_ref, hin_ref,
             y_ref, hout_ref, x_scr, f_scr, g_scr):
        for l in range(S5_TILE):
            x_scr[:, l * LANES:(l + 1) * LANES] = u_ref[pl.ds(l, m_t, stride=S5_TILE), :].astype(BF16)
        x = x_scr[...]
        f_scr[...] = jnp.dot(x, pf_ref[...], preferred_element_type=F32)
        g_scr[...] = jnp.dot(x, pb_ref[...], preferred_element_type=F32)
        fr, fi = scan(f_scr, pwf_ref, hin_ref[0:1, :], hin_ref[1:2, :], False)
        gr, gi = scan(g_scr, pwb_ref, hin_ref[2:3, :], hin_ref[3:4, :], True)
        hout_ref[0:1, :] = fr[0:1, :]
        hout_ref[1:2, :] = fi[0:1, :]
        hout_ref[2:3, :] = gr[0:1, :]
        hout_ref[3:4, :] = gi[0:1, :]
        y = (jnp.dot(x, t_ref[...], preferred_element_type=F32)
             + jnp.dot(f_scr[...].astype(BF16), of_ref[...], preferred_element_type=F32)
             + jnp.dot(g_scr[...].astype(BF16), ob_ref[...], preferred_element_type=F32))
        for l in range(S5_TILE):
            y_ref[pl.ds(l, m_t, stride=S5_TILE), :] = y[:, l * LANES:(l + 1) * LANES]

    def wspec():
        return pl.BlockSpec((None, wide, wide), lambda li, bi: (li, 0, 0), pipeline_mode=pl.Buffered(1))

    est = (4 * seq * LANES * 4 + 5 * wide * wide * 2 + m_t * wide * 2 + 2 * m_t * 2 * sw * 4
           + 3 * m_t * wide * 4)
    y, h_out = pl.pallas_call(
        body,
        out_shape=[jax.ShapeDtypeStruct((nb * seq, d_model), F32),
                   jax.ShapeDtypeStruct((nb, lbs, 4, sw), F32)],
        grid=(lbs, nb),
        in_specs=[pl.BlockSpec((seq, LANES), lambda li, bi: (bi, li)),
                  wspec(), wspec(), wspec(), wspec(), wspec(),
                  pl.BlockSpec((None, 8, SUBLANES, sw), lambda li, bi: (li, 0, 0, 0)),
                  pl.BlockSpec((None, 8, SUBLANES, sw), lambda li, bi: (li, 0, 0, 0)),
                  pl.BlockSpec((None, None, 4, sw), lambda li, bi: (bi, li, 0, 0))],
        out_specs=[pl.BlockSpec((seq, LANES), lambda li, bi: (bi, li)),
                   pl.BlockSpec((None, None, 4, sw), lambda li, bi: (bi, li, 0, 0))],
        scratch_shapes=[pltpu.VMEM((m_t, wide), BF16), pltpu.VMEM((m_t, 2 * sw), F32),
                        pltpu.VMEM((m_t, 2 * sw), F32)],
        compiler_params=_params(("parallel", "parallel"), est),
        name="s5_bidir",
    )(u, w["t"], w["pf"], w["pb"], w["of"], w["ob"], pw_f, pw_b, h_in)
    return y, h_out


def _attention(q, kv_parts, *, nb, lq, scale):
    heads = q.shape[1] // HEAD_PAD
    tq = _pick(lq, (1024, 512, 256, 128))
    tsub = _pick(tq, (256, 128))
    nq = lq // tq
    c = scale * math.log2(math.e)
    chunks = [_pick(length, (1024, 512, 256, 128)) for (_, _, length) in kv_parts]
    n_parts = len(kv_parts)

    def body(*refs):
        q_ref = refs[0]
        kv_refs = refs[1:1 + 2 * n_parts]
        o_ref = refs[1 + 2 * n_parts]
        s_a, s_b = refs[2 + 2 * n_parts], refs[3 + 2 * n_parts]

        def q_tile(t, carry):
            r0 = pl.multiple_of(t * tsub, tsub)
            qt = q_ref[pl.ds(r0, tsub), :]

            def halves(n):
                return ((0, n // 2), (n // 2, n)) if n >= 2 * MAX_FOLD_ROWS else ((0, n),)

            def scores(k):
                return jnp.concatenate(
                    [lax.dot_general(k[a:b], qt, (((1,), (1,)), ((), ())), preferred_element_type=F32)
                     for a, b in halves(k.shape[0])], axis=0)

            def absorb(s_t, v_t, m, acc):
                part = jnp.max(s_t.reshape(-1, MAX_FOLD_ROWS, tsub), axis=0)
                m_new = jnp.maximum(m, jnp.max(part, axis=0, keepdims=True))
                p_t = jnp.exp2((s_t - m_new) * c).astype(BF16)
                alpha = jnp.exp2((m - m_new) * c)
                acc = alpha * acc
                for a, b in halves(s_t.shape[0]):
                    acc = acc + jnp.dot(v_t[:, a:b], p_t[a:b], preferred_element_type=F32)
                return m_new, acc

            m = jnp.full((1, tsub), -jnp.inf, F32)
            acc = jnp.zeros((HEAD_PAD, tsub), F32)
            sched = [(kv_refs[2 * pi], kv_refs[2 * pi + 1], ci, chunks[pi])
                     for pi, (_, _, length) in enumerate(kv_parts) for ci in range(length // chunks[pi])]
            bufs = (s_a, s_b)

            def issue(i):
                k_ref, _, ci, tk = sched[i]
                bufs[i % 2][0:tk, :] = scores(k_ref[ci * tk:(ci + 1) * tk, :])

            issue(0)
            for i, (_, v_ref, ci, tk) in enumerate(sched):
                if i + 1 < len(sched):
                    issue(i + 1)
                m, acc = absorb(bufs[i % 2][0:tk, :], v_ref[ci], m, acc)
            o_t = acc / acc[V_ONES_LANE:V_ONES_LANE + 1, :]
            o_ref[pl.ds(r0, tsub), :] = o_t.T.astype(o_ref.dtype)
            return carry

        lax.fori_loop(0, tq // tsub, q_tile, 0)

    ins = [q]
    in_specs = [pl.BlockSpec((tq, HEAD_PAD), lambda b, h, qi: (b * nq + qi, h))]
    est = tq * HEAD_PAD * 2 * 4 + 6 * tsub * max(chunks) * 4
    for (k, v, length), tk in zip(kv_parts, chunks):
        n_ch = length // tk
        v_t = v.reshape(nb, n_ch, tk, heads, HEAD_PAD).transpose(0, 3, 1, 4, 2)
        ins += [k, v_t]
        in_specs += [pl.BlockSpec((length, HEAD_PAD), lambda b, h, qi: (b, h)),
                     pl.BlockSpec((None, None, n_ch, HEAD_PAD, tk), lambda b, h, qi: (b, h, 0, 0, 0))]
        est += 2 * length * HEAD_PAD * 2 * 2
    return pl.pallas_call(
        body,
        out_shape=jax.ShapeDtypeStruct(q.shape, BF16),
        grid=(nb, heads, nq),
        in_specs=in_specs,
        out_specs=pl.BlockSpec((tq, HEAD_PAD), lambda b, h, qi: (b * nq + qi, h)),
        scratch_shapes=[pltpu.VMEM((max(chunks), tsub), F32), pltpu.VMEM((max(chunks), tsub), F32)],
        compiler_params=_params(("parallel", "parallel", "arbitrary"), est),
        name="attention",
    )(*ins)


class _Stream:
    def __init__(self, x, nb, seq, mods):
        self.x = x
        self.nb = nb
        self.seq = seq
        self.tm = _pick(seq, (512, 256, 128))
        self.tpb = seq // self.tm
        self.mods = mods
        self.rows = nb * seq


def _in_proj(st, norm_w, sc, sh, weights, n, tn, out_dtypes=None):
    d = st.x.shape[1]
    pro = [_rows_in(st.x, st.tm, d), _const_in(norm_w, d), _batch_in(sc, st.tpb, d), _batch_in(sh, st.tpb, d)]
    return _fused_matmul(pro, weights, [], _norm_mod, lambda accs: accs, rows=st.rows, tm=st.tm, k=d, n=n,
                         tn=tn, out_dtypes=out_dtypes or [F32] * len(weights))


def _out_proj(st, pro_ins, prologue, weight, gate, k):
    d = st.x.shape[1]
    epi = [_tile_in(st.x, st.tm, d), _batch_tile_in(gate, st.tpb, d)]
    (out,) = _fused_matmul(pro_ins, [weight], epi, prologue, lambda accs, x, g: [x + g * accs[0]],
                           rows=st.rows, tm=st.tm, k=k, n=d, tn=d, out_dtypes=[F32])
    return out


def _ffn(st, norm_w, sc, sh, gate, w1, w3, w2):
    d = st.x.shape[1]
    hidden = w1.shape[1]
    tn = _pick(hidden, (1408, 1024, 512, 256, 128))
    pro = [_rows_in(st.x, st.tm, d), _const_in(norm_w, d), _batch_in(sc, st.tpb, d), _batch_in(sh, st.tpb, d)]
    (act,) = _fused_matmul(pro, [w1, w3], [], _norm_mod, lambda accs: [_silu(accs[0]) * accs[1]],
                           rows=st.rows, tm=st.tm, k=d, n=hidden, tn=tn, out_dtypes=[BF16])
    return _out_proj(st, [_rows_in(act, st.tm, hidden)], lambda a: a, w2, gate, hidden)


def _mamba_layer(lat, ctx, p, ctx_out):
    inner = p["inner"]
    conv_ch = p["conv_ch"]
    heads = inner // M2_HEADDIM
    gw = inner // M2_GROUPS
    outs = {}
    acts = {}
    for name, st in (("ctx", ctx), ("lat", lat)):
        m = st.mods
        (zx,) = _in_proj(st, p["norm1"], m["sc1"], m["sh1"], [p["w_zx"]], inner + conv_ch,
                         _pick(inner + conv_ch, (1024, 512, 256, 128)))
        dt0, dt1 = _in_proj(st, p["norm1"], m["sc1"], m["sh1"], [p["w_dt0"], p["w_dt1"]], LANES, LANES)
        xbc = _dwconv(zx, p["conv_w"], p["conv_b"], nb=st.nb, seq=st.seq, col_off=inner // LANES,
                      channels=conv_ch, act=True)
        acts[name] = (zx, (dt0, dt1), xbc)
    ys = {}
    for dirn, reverse in ((0, False), (1, True)):
        consts = _ssd_consts(inner, reverse)
        h = jnp.zeros((ctx.nb, M2_GROUPS, M2_STATE, gw), F32)
        for name, st in (("ctx", ctx), ("lat", lat)):
            zx, dts, xbc = acts[name]
            y, h = _ssd_direction(xbc, dts[dirn], p["par"][dirn], p["dskip"] if dirn == 0 else None, h,
                                  ys.get(name), consts, nb=st.nb, seq=st.seq, reverse=reverse)
            ys[name] = y

    def gated_norm(y, z, w):
        return _rms(y * _silu(z), w)

    for name, st in (("ctx", ctx), ("lat", lat)):
        if name == "ctx" and not ctx_out:
            continue
        zx = acts[name][0]
        pro = [_rows_in(ys[name], st.tm, inner), _rows_in(zx, st.tm, inner), _const_in(p["norm_w"], inner)]
        outs[name] = _out_proj(st, pro, gated_norm, p["w_out"], st.mods["g1"], inner)
    return outs


def _rglru_layer(lat, ctx, p, ctx_out):
    width = p["width"]
    outs = {}
    acts = {}
    for name, st in (("ctx", ctx), ("lat", lat)):
        m = st.mods
        (yx,) = _in_proj(st, p["norm1"], m["sc1"], m["sh1"], [p["w_in"]], 2 * width,
                         _pick(2 * width, (1280, 1024, 512, 256, 128)))
        xc = _dwconv(yx, p["conv_w"], p["conv_b"], nb=st.nb, seq=st.seq, col_off=width // LANES,
                     channels=width, act=False)
        acts[name] = (yx, xc)
    rs = {}
    for dirn, reverse in ((0, False), (1, True)):
        h = jnp.zeros((ctx.nb, 1, width), F32)
        for name, st in (("ctx", ctx), ("lat", lat)):
            r, h = _rglru_direction(acts[name][1], p["gate_w"][dirn], p["gate_b"][dirn], p["a_param"][dirn], h,
                                    rs.get(name), nb=st.nb, seq=st.seq, reverse=reverse)
            rs[name] = r
    for name, st in (("ctx", ctx), ("lat", lat)):
        if name == "ctx" and not ctx_out:
            continue
        pro = [_rows_in(acts[name][0], st.tm, width), _rows_in(rs[name], st.tm, width)]
        outs[name] = _out_proj(st, pro, lambda y, r: _gelu_tanh(y) * r, p["w_out"], st.mods["g1"], width)
    return outs


def _s5_layer(lat, ctx, p, ctx_out):
    d = lat.x.shape[1]
    lbs = d // LANES
    sw = p["pw"][0].shape[3]
    us = {}
    for name, st in (("ctx", ctx), ("lat", lat)):
        m = st.mods
        us[name] = _norm_mod_rows(st.x, p["norm1"], m["sc1"], m["sh1"], tm=st.tm, tpb=st.tpb)
    ys = {}
    h = jnp.zeros((ctx.nb, lbs, 4, sw), F32)
    for name, st in (("ctx", ctx), ("lat", lat)):
        ys[name], h = _s5_bidir(us[name], p["w"], p["pw"][0], p["pw"][1], h, nb=st.nb, seq=st.seq)
    outs = {}
    for name, st in (("ctx", ctx), ("lat", lat)):
        if name == "ctx" and not ctx_out:
            continue
        pro = [_rows_in(ys[name], st.tm, d), _rows_in(us[name], st.tm, d), _const_in(p["dskip"], d)]
        epi = [_const_tile_in(p["glu_ba"], d), _const_tile_in(p["glu_bg"], d),
               _tile_in(st.x, st.tm, d), _batch_tile_in(st.mods["g1"], st.tpb, d)]
        (out,) = _fused_matmul(
            pro, [p["glu_wa"], p["glu_wg"]], epi,
            lambda y, u, dsk: _gelu_tanh(y + dsk * u),
            lambda accs, ba, bg, x, g: [x + g * ((accs[0] + ba) * _sigmoid(accs[1] + bg))],
            rows=st.rows, tm=st.tm, k=d, n=d, tn=d, out_dtypes=[F32])
        outs[name] = out
    return outs


def _mla_layer(lat, ctx, p, ctx_out):
    d = lat.x.shape[1]
    qr, kvr = p["q_rank"], p["kv_rank"]
    hp = MLA_HEADS * HEAD_PAD
    tn_h = _pick(hp, (512, 256, 128))
    reps = tn_h // HEAD_PAD
    qkv = {}
    for name, st in (("ctx", ctx), ("lat", lat)):
        m = st.mods
        n_in = p["w_in"].shape[1]
        (lat_all,) = _in_proj(st, p["norm1"], m["sc1"], m["sh1"], [p["w_in"]], n_in, n_in)
        cos_t, sin_t = p["tables"][name]
        tab_spec = lambda a, st=st: (a, (st.tm, HEAD_PAD), lambda i, j: (i % st.tpb, 0))

        def rope_epi(accs, cos, sin, reps=reps):
            cos_r = jnp.concatenate([cos] * reps, axis=1)
            sin_r = jnp.concatenate([sin] * reps, axis=1)
            return [accs[0] * cos_r + accs[1] * sin_r]

        want_q = name == "lat" or ctx_out
        q = None
        if want_q:
            (q,) = _fused_matmul([_rows_in(lat_all, st.tm, qr, 0), _const_in(p["q_norm"], qr)],
                                 [p["w_qa"], p["w_qb"]], [tab_spec(cos_t), tab_spec(sin_t)],
                                 _rms, rope_epi, rows=st.rows, tm=st.tm, k=qr, n=hp, tn=tn_h, out_dtypes=[BF16])

        def kv_epi(accs, kra, krb, cos, sin, ones, reps=reps):
            kr = kra * cos + krb * sin
            return [accs[0] + jnp.concatenate([kr] * reps, axis=1), accs[1] + ones]

        kra_blk = (qr // LANES)
        kv_blk = (qr + LANES) // kvr
        krb_blk = (qr + LANES + kvr) // LANES
        k, v = _fused_matmul(
            [_rows_in(lat_all, st.tm, kvr, kv_blk), _const_in(p["kv_norm"], kvr)],
            [p["w_k"], p["w_v"]],
            [(lat_all, (st.tm, LANES), lambda i, j: (i, kra_blk)),
             (lat_all, (st.tm, LANES), lambda i, j: (i, krb_blk)),
             tab_spec(cos_t), tab_spec(sin_t), _const_tile_in(p["v_ones"], tn_h)],
            _rms, kv_epi, rows=st.rows, tm=st.tm, k=kvr, n=hp, tn=tn_h, out_dtypes=[BF16, BF16])
        qkv[name] = (q, k, v)
    scale = (MLA_NOPE + MLA_ROPE) ** -0.5
    nb = lat.nb
    kv_c = (qkv["ctx"][1], qkv["ctx"][2], ctx.seq)
    kv_l = (qkv["lat"][1], qkv["lat"][2], lat.seq)
    outs = {}
    o_l = _attention(qkv["lat"][0], [kv_c, kv_l], nb=nb, lq=lat.seq, scale=scale)
    outs["lat"] = _out_proj(lat, [_rows_in(o_l, lat.tm, hp)], lambda a: a, p["w_out"], lat.mods["g1"], hp)
    if ctx_out:
        o_c = _attention(qkv["ctx"][0], [kv_c], nb=nb, lq=ctx.seq, scale=scale)
        outs["ctx"] = _out_proj(ctx, [_rows_in(o_c, ctx.tm, hp)], lambda a: a, p["w_out"], ctx.mods["g1"], hp)
    return outs


def _pad_cols(w, n):
    return jnp.pad(w, ((0, 0), (0, n - w.shape[1])))


def _mamba_params(in_w, conv_w, conv_b, dt_bias, a_log, d_skip, norm_w, out_w):
    d, proj = in_w.shape
    heads = dt_bias.shape[1]
    inner = heads * M2_HEADDIM
    conv_ch = inner + 2 * M2_GROUPS * M2_STATE
    par = []
    for dirn in range(2):
        rows = jnp.zeros((SUBLANES, LANES), F32)
        rows = rows.at[0, :heads].set(dt_bias[dirn].astype(F32)).at[1, :heads].set(a_log[dirn].astype(F32))
        par.append(rows)
    off = inner + conv_ch
    return dict(
        inner=inner, conv_ch=conv_ch,
        w_zx=in_w[:, :off].astype(BF16),
        w_dt0=_pad_cols(in_w[:, off:off + heads], LANES).astype(BF16),
        w_dt1=_pad_cols(in_w[:, off + heads:off + 2 * heads], LANES).astype(BF16),
        conv_w=conv_w.astype(F32), conv_b=conv_b.astype(F32)[None, :],
        par=par, dskip=jnp.repeat(d_skip.astype(F32), M2_HEADDIM)[None, :],
        norm_w=norm_w.astype(F32)[None, :], w_out=out_w.astype(BF16))


def _rglru_params(in_w, conv_w, conv_b, gate_w, gate_b, a_param, out_w):
    width = conv_w.shape[1]
    return dict(
        width=width, w_in=in_w.astype(BF16), conv_w=conv_w.astype(F32), conv_b=conv_b.astype(F32)[None, :],
        gate_w=[gate_w[dirn].astype(BF16) for dirn in range(2)],
        gate_b=[gate_b[dirn].astype(F32)[:, None, :] for dirn in range(2)],
        a_param=[a_param[dirn].astype(F32)[None, :] for dirn in range(2)],
        w_out=out_w.astype(BF16))


def _block_diag(m, per):
    g, r, c = m.shape
    m = m.reshape(g // per, per, r, c)
    eye = jnp.eye(per, dtype=m.dtype)
    return jnp.einsum("qarc,ab->qarbc", m, eye).reshape(g // per, per * r, per * c)


def _complex_powers(ar, ai, n):
    pr, pi = [jnp.ones_like(ar)], [jnp.zeros_like(ai)]
    for _ in range(n):
        pr, pi = pr + [pr[-1] * ar - pi[-1] * ai], pi + [pr[-1] * ai + pi[-1] * ar]
    return pr, pi


def _s5_params(lam_re, lam_im, log_step, b_re, b_im, c_re, c_im, d_skip, glu_w, glu_b):
    per = LANES // S5_GROUP
    hp = lax.Precision.HIGHEST
    br, bi = b_re.astype(F32), b_im.astype(F32)
    out = dict(pw=[])
    w = {}
    for dirn in range(2):
        reverse = dirn == 1
        lr = jnp.minimum(lam_re[dirn].astype(F32), -1e-4)
        li = lam_im[dirn].astype(F32)
        step = jnp.exp(log_step[dirn].astype(F32))[:, None]
        mag = jnp.exp(lr * step)
        abr, abi = mag * jnp.cos(li * step), mag * jnp.sin(li * step)
        den = lr * lr + li * li
        zr = ((abr - 1.0) * lr + abi * li) / den
        zi = (abi * lr - (abr - 1.0) * li) / den
        bbr = zr[..., None] * br - zi[..., None] * bi
        bbi = zr[..., None] * bi + zi[..., None] * br
        cr, ci = c_re[dirn].astype(F32), c_im[dirn].astype(F32)
        pr, pi = _complex_powers(abr, abi, S5_TILE)

        def a_bb(e):
            return (pr[e][..., None] * bbr - pi[e][..., None] * bbi,
                    pr[e][..., None] * bbi + pi[e][..., None] * bbr)

        def tap(dist):
            wr, wi = a_bb(dist)
            return (jnp.einsum("gjp,gpi->gij", cr, wr, precision=hp)
                    - jnp.einsum("gjp,gpi->gij", ci, wi, precision=hp))

        taps = [_block_diag(tap(dist), per) for dist in range(S5_TILE)]
        zero = jnp.zeros_like(taps[0])
        t_rows, p_rows, o_cols = [], [], []
        for l in range(S5_TILE):
            dists = [(l - lo) if reverse else (lo - l) for lo in range(S5_TILE)]
            t_rows.append(jnp.concatenate([taps[dd] if dd >= 0 else zero for dd in dists], axis=2))
            wr, wi = a_bb(l if reverse else S5_TILE - 1 - l)
            p_rows.append(jnp.concatenate([_block_diag(jnp.swapaxes(wr, 1, 2), per),
                                           _block_diag(jnp.swapaxes(wi, 1, 2), per)], axis=2))
            e = (S5_TILE - l) if reverse else (l + 1)
            o_re = jnp.swapaxes(cr * pr[e][:, None, :] - ci * pi[e][:, None, :], 1, 2)
            o_im = jnp.swapaxes(-(cr * pi[e][:, None, :] + ci * pr[e][:, None, :]), 1, 2)
            o_cols.append(jnp.concatenate([_block_diag(o_re, per), _block_diag(o_im, per)], axis=1))
        t_dir = jnp.concatenate(t_rows, axis=1)
        w["t"] = t_dir if dirn == 0 else w["t"] + t_dir
        w["pb" if reverse else "pf"] = jnp.concatenate(p_rows, axis=1).astype(BF16)
        w["ob" if reverse else "of"] = jnp.concatenate(o_cols, axis=2).astype(BF16)
        qr, qi = _complex_powers(pr[S5_TILE], pi[S5_TILE], SUBLANES)
        g = abr.shape[0]
        flat = lambda a: a.reshape(g // per, per * a.shape[1])
        t = jnp.arange(SUBLANES)[None, :, None]
        tiles = []
        for dist in (1, 2, 4):
            valid = (t < SUBLANES - dist) if reverse else (t >= dist)
            tiles.append(jnp.where(valid, flat(qr[dist])[:, None, :], 0.0))
            tiles.append(jnp.where(valid, flat(qi[dist])[:, None, :], 0.0))
        order = list(range(SUBLANES, 0, -1)) if reverse else list(range(1, SUBLANES + 1))
        tiles.append(jnp.stack([flat(qr[o]) for o in order], axis=1))
        tiles.append(jnp.stack([flat(qi[o]) for o in order], axis=1))
        out["pw"].append(jnp.stack(tiles, axis=1).astype(F32))
    w["t"] = w["t"].astype(BF16)
    out["w"] = w
    d = glu_w.shape[0]
    out.update(dskip=d_skip.astype(F32)[None, :],
               glu_wa=glu_w[:, :d].astype(BF16), glu_wg=glu_w[:, d:].astype(BF16),
               glu_ba=glu_b[:d].astype(F32)[None, :], glu_bg=glu_b[d:].astype(F32)[None, :])
    return out


def _head_pad_cols(w, per_head, take):
    kdim = w.shape[0]
    w = w.reshape(kdim, MLA_HEADS, per_head)[:, :, take]
    w = jnp.pad(w, ((0, 0), (0, 0), (0, HEAD_PAD - w.shape[2])))
    return w.reshape(kdim, MLA_HEADS * HEAD_PAD)


def _rope_swap(w):
    idx = jnp.arange(MLA_ROPE)
    blk, pos = idx // (2 * ROPE_FREQ), idx % (2 * ROPE_FREQ)
    return w[..., blk * 2 * ROPE_FREQ + (pos + ROPE_FREQ) % (2 * ROPE_FREQ)]


def _mla_params(in_w, q_norm_w, kv_norm_w, qb_w, kvb_w, out_w, seq, ctx_len):
    d = in_w.shape[0]
    qr, kvr = q_norm_w.shape[0], kv_norm_w.shape[0]
    dq = MLA_NOPE + MLA_ROPE
    w_q, w_kv, w_kr = in_w[:, :qr], in_w[:, qr:qr + kvr], in_w[:, qr + kvr:]

    def place(w):
        return jnp.pad(w, ((0, 0), (MLA_NOPE, HEAD_PAD - dq)))

    w_in = jnp.concatenate([w_q, place(w_kr), w_kv, place(_rope_swap(w_kr))], axis=1).astype(BF16)
    qb = qb_w.reshape(qr, MLA_HEADS, dq)
    qa = jnp.pad(qb, ((0, 0), (0, 0), (0, HEAD_PAD - dq))).reshape(qr, -1)
    qb_sw = jnp.pad(_rope_swap(qb[:, :, MLA_NOPE:]), ((0, 0), (0, 0), (MLA_NOPE, HEAD_PAD - dq))).reshape(qr, -1)
    per = MLA_NOPE + MLA_V
    w_k = _head_pad_cols(kvb_w, per, slice(0, MLA_NOPE))
    w_v = _head_pad_cols(kvb_w, per, slice(MLA_NOPE, per))
    w_out = jnp.pad(out_w.reshape(MLA_HEADS, MLA_V, d), ((0, 0), (0, HEAD_PAD - MLA_V), (0, 0))).reshape(-1, d)
    rows = seq // GRID_W
    row = jnp.repeat(jnp.arange(rows, dtype=F32), GRID_W)
    col = jnp.tile(jnp.arange(GRID_W, dtype=F32), rows)
    inv_freq = ROPE_BASE ** (-jnp.arange(ROPE_FREQ, dtype=F32) / ROPE_FREQ)
    ang = jnp.stack([row[:, None] * inv_freq, col[:, None] * inv_freq], axis=1)
    cos, sin = jnp.cos(ang), jnp.sin(ang)
    cos32 = jnp.concatenate([cos, cos], axis=2).reshape(seq, MLA_ROPE)
    sin32 = jnp.concatenate([-sin, sin], axis=2).reshape(seq, MLA_ROPE)
    ones = jnp.ones((seq, MLA_NOPE), F32)
    cos_l = jnp.pad(jnp.concatenate([ones, cos32], axis=1), ((0, 0), (0, HEAD_PAD - dq)))
    sin_l = jnp.pad(sin32, ((0, 0), (MLA_NOPE, HEAD_PAD - dq)))
    cos_c = jnp.pad(jnp.ones((ctx_len, dq), F32), ((0, 0), (0, HEAD_PAD - dq)))
    sin_c = jnp.zeros((ctx_len, HEAD_PAD), F32)
    return dict(q_rank=qr, kv_rank=kvr, w_in=w_in, q_norm=q_norm_w.astype(F32)[None, :],
                kv_norm=kv_norm_w.astype(F32)[None, :], w_qa=qa.astype(BF16), w_qb=qb_sw.astype(BF16),
                w_k=w_k.astype(BF16), w_v=w_v.astype(BF16), w_out=w_out.astype(BF16),
                v_ones=jnp.tile((jnp.arange(HEAD_PAD) == V_ONES_LANE).astype(F32), MLA_HEADS)[None, :],
                tables=dict(lat=(cos_l, sin_l), ctx=(cos_c, sin_c)))


def _modulation(c, c_ctx, ada_w, ada_b):
    nb, d = c.shape
    rows = jnp.zeros((SUBLANES, d), F32).at[:nb].set(c).at[nb].set(c_ctx)
    n = ada_w.shape[1]
    (mod,) = _fused_matmul([(rows, (SUBLANES, d), lambda i, j: (0, 0))], [ada_w.astype(BF16)],
                           [_const_tile_in(ada_b.astype(F32)[None, :], _pick(n, (1024, 512, 256, 128)))],
                           _silu, lambda accs, b: [accs[0] + b], rows=SUBLANES, tm=SUBLANES, k=d, n=n,
                           tn=_pick(n, (1024, 512, 256, 128)), out_dtypes=[F32])
    names = ("sh1", "sc1", "g1", "sh2", "sc2", "g2")
    parts = jnp.split(mod, 6, axis=1)
    lat = {nm: pt[:nb][:, None, :] for nm, pt in zip(names, parts)}
    ctx = {nm: pt[nb:nb + 1][:, None, :] for nm, pt in zip(names, parts)}
    return lat, ctx


def kernel(x, c, ctx, c_ctx, ada_w, ada_b, norm1_w, norm2_w, ffn_w13, ffn_w2, m2_in_w, m2_conv_w, m2_conv_b, m2_dt_bias, m2_a_log, m2_d, m2_norm_w, m2_out_w, lru_in_w, lru_conv_w, lru_conv_b, lru_gate_w, lru_gate_b, lru_a_param, lru_out_w, s5_lambda_re, s5_lambda_im, s5_log_step, s5_b_re, s5_b_im, s5_c_re, s5_c_im, s5_d, s5_glu_w, s5_glu_b, mla_in_w, mla_q_norm_w, mla_kv_norm_w, mla_qb_w, mla_kvb_w, mla_out_w, final_norm_w):
    nb, seq, d = x.shape
    ctx_len = ctx.shape[1]
    depth = ada_w.shape[0]
    hidden = ffn_w2.shape[1]
    xl = x.reshape(nb * seq, d).astype(F32)
    xc = ctx.reshape(nb * ctx_len, d).astype(F32)
    for i in range(depth):
        kind, j = i % N_MIXERS, i // N_MIXERS
        ctx_out = i < depth - 1
        mods_l, mods_c = _modulation(c.astype(F32), c_ctx.astype(F32), ada_w[i], ada_b[i])
        lat_s = _Stream(xl, nb, seq, mods_l)
        ctx_s = _Stream(xc, nb, ctx_len, mods_c)
        n1 = norm1_w[i].astype(F32)[None, :]
        if kind == 0:
            p = _mamba_params(m2_in_w[j], m2_conv_w[j], m2_conv_b[j], m2_dt_bias[j], m2_a_log[j], m2_d[j],
                              m2_norm_w[j], m2_out_w[j])
            p["norm1"] = n1
            outs = _mamba_layer(lat_s, ctx_s, p, ctx_out)
        elif kind == 1:
            p = _rglru_params(lru_in_w[j], lru_conv_w[j], lru_conv_b[j], lru_gate_w[j], lru_gate_b[j],
                              lru_a_param[j], lru_out_w[j])
            p["norm1"] = n1
            outs = _rglru_layer(lat_s, ctx_s, p, ctx_out)
        elif kind == 2:
            p = _s5_params(s5_lambda_re[j], s5_lambda_im[j], s5_log_step[j], s5_b_re[j], s5_b_im[j],
                           s5_c_re[j], s5_c_im[j], s5_d[j], s5_glu_w[j], s5_glu_b[j])
            p["norm1"] = n1
            outs = _s5_layer(lat_s, ctx_s, p, ctx_out)
        else:
            p = _mla_params(mla_in_w[j], mla_q_norm_w[j], mla_kv_norm_w[j], mla_qb_w[j], mla_kvb_w[j],
                            mla_out_w[j], seq, ctx_len)
            p["norm1"] = n1
            outs = _mla_layer(lat_s, ctx_s, p, ctx_out)
        n2 = norm2_w[i].astype(F32)[None, :]
        w1 = ffn_w13[i][:, :hidden].astype(BF16)
        w3 = ffn_w13[i][:, hidden:].astype(BF16)
        w2 = ffn_w2[i].astype(BF16)
        lat_s = _Stream(outs["lat"], nb, seq, mods_l)
        xl = _ffn(lat_s, n2, mods_l["sc2"], mods_l["sh2"], mods_l["g2"], w1, w3, w2)
        if ctx_out:
            ctx_s = _Stream(outs["ctx"], nb, ctx_len, mods_c)
            xc = _ffn(ctx_s, n2, mods_c["sc2"], mods_c["sh2"], mods_c["g2"], w1, w3, w2)
    zero = jnp.zeros((1, 1, d), F32)
    lat_s = _Stream(xl, nb, seq, None)
    out = _norm_mod_rows(xl, final_norm_w.astype(F32)[None, :], zero, zero, tm=lat_s.tm, tpb=lat_s.tpb)
    return out.reshape(nb, seq, d).astype(x.dtype)
```

```python
import functools
import math

import jax
import jax.numpy as jnp
from jax import lax
from jax.experimental import pallas as pl
from jax.experimental.pallas import tpu as pltpu

F32 = jnp.float32
BF16 = jnp.bfloat16

LANES = 128
SUBLANES = 8
VMEM_BUDGET_BYTES = 56 * 2**20

GRID_W = 64
N_MIXERS = 4
NORM_EPS = 1e-6
CONV_WIDTH = 4
CONV_PAD_LEFT = 2
M2_HEADDIM = 64
M2_GROUPS = 4
M2_STATE = 128
SSD_CHUNK = 128
LRU_BLOCK = 128
LRU_C = 8.0
S5_GROUP = 16
S5_STATE = 64
S5_TILE = 8
MLA_HEADS = 16
MLA_NOPE = 64
MLA_ROPE = 32
MLA_V = 64
ROPE_FREQ = MLA_ROPE // 4
ROPE_BASE = 10000.0
HEAD_PAD = 128
MAX_FOLD_ROWS = 64
V_ONES_LANE = MLA_V


def _pick(n, cands):
    for c in cands:
        if n % c == 0:
            return c
    raise ValueError(f"no tile in {cands} divides {n}")


def _params(sem, est_bytes):
    limit = int(min(max(2 * est_bytes, 32 * 2**20), VMEM_BUDGET_BYTES))
    return pltpu.CompilerParams(dimension_semantics=sem, vmem_limit_bytes=limit)


def _nbytes(shape, dtype):
    return math.prod(s for s in shape if s is not None) * jnp.dtype(dtype).itemsize


def _sigmoid(x):
    return 1.0 / (1.0 + jnp.exp(-x))


def _silu(x):
    return x * _sigmoid(x)


def _softplus(x):
    return jnp.maximum(x, 0.0) + jnp.log1p(jnp.exp(-jnp.abs(x)))


def _gelu_tanh(x):
    return 0.5 * x * (1.0 + jnp.tanh(math.sqrt(2.0 / math.pi) * (x + 0.044715 * (x * x * x))))


def _rms(x, w):
    return x * lax.rsqrt(jnp.mean(x * x, axis=-1, keepdims=True) + NORM_EPS) * w


def _norm_mod(x, w, sc, sh):
    return _rms(x, w) * (1.0 + sc) + sh


def _split3(q):
    q1 = q.astype(BF16)
    r1 = q - q1.astype(F32)
    q2 = r1.astype(BF16)
    q3 = (r1 - q2.astype(F32)).astype(BF16)
    return q1, q2, q3


def _select_cols(q, e, parts=3):
    return sum(jnp.dot(p, e, preferred_element_type=F32) for p in _split3(q)[:parts])


def _select_rows(t, q):
    return sum(jnp.dot(t, p, preferred_element_type=F32) for p in _split3(q))


def _fused_matmul(pro_ins, weights, epi_ins, prologue, epilogue, *, rows, tm, k, n, tn, out_dtypes):
    n_p, n_w, n_e, n_o = len(pro_ins), len(weights), len(epi_ins), len(out_dtypes)

    def body(*refs):
        p = refs[:n_p]
        w = refs[n_p:n_p + n_w]
        e = refs[n_p + n_w:n_p + n_w + n_e]
        o = refs[n_p + n_w + n_e:n_p + n_w + n_e + n_o]
        a_scr = refs[-1]

        @pl.when(pl.program_id(1) == 0)
        def _():
            a_scr[...] = prologue(*[r[...] for r in p]).astype(BF16)

        a = a_scr[...]
        accs = [jnp.dot(a, wr[...], preferred_element_type=F32) for wr in w]
        outs = epilogue(accs, *[r[...] for r in e])
        for o_ref, val in zip(o, outs):
            o_ref[...] = val.astype(o_ref.dtype)

    in_specs = ([pl.BlockSpec(bs, im) for (_, bs, im) in pro_ins]
                + [pl.BlockSpec((k, tn), lambda i, j: (0, j)) for _ in weights]
                + [pl.BlockSpec(bs, im) for (_, bs, im) in epi_ins])
    out_specs = [pl.BlockSpec((tm, tn), lambda i, j: (i, j)) for _ in out_dtypes]
    est = (sum(_nbytes(bs, a.dtype) for (a, bs, _) in pro_ins + epi_ins) * 2
           + n_w * k * tn * 2 * 2 + sum(tm * tn * jnp.dtype(d).itemsize for d in out_dtypes) * 2
           + tm * k * 2 + (n_w + 2) * tm * tn * 4 + tm * k * 8)
    outs = pl.pallas_call(
        body,
        out_shape=[jax.ShapeDtypeStruct((rows, n), d) for d in out_dtypes],
        grid=(rows // tm, n // tn),
        in_specs=in_specs,
        out_specs=out_specs,
        scratch_shapes=[pltpu.VMEM((tm, k), BF16)],
        compiler_params=_params(("parallel", "arbitrary"), est),
        name=f"mm_r{rows}_k{k}_n{n}x{n_w}",
    )(*[a for (a, _, _) in pro_ins], *weights, *[a for (a, _, _) in epi_ins])
    return outs


def _rows_in(a, tm, width, col_block=0):
    return (a, (tm, width), lambda i, j: (i, col_block))


def _batch_in(a, tpb, width):
    if a.shape[0] == 1:
        return (a, (None, 1, width), lambda i, j: (0, 0, 0))
    return (a, (None, 1, width), lambda i, j: (i // tpb, 0, 0))


def _const_in(a, width):
    return (a, (1, width), lambda i, j: (0, 0))


def _tile_in(a, tm, tn):
    return (a, (tm, tn), lambda i, j: (i, j))


def _batch_tile_in(a, tpb, tn):
    if a.shape[0] == 1:
        return (a, (None, 1, tn), lambda i, j: (0, 0, j))
    return (a, (None, 1, tn), lambda i, j: (i // tpb, 0, j))


def _const_tile_in(a, tn):
    return (a, (1, tn), lambda i, j: (0, j))


def _norm_mod_rows(x, w, sc, sh, *, tm, tpb):
    rows, d = x.shape

    def body(x_ref, w_ref, sc_ref, sh_ref, o_ref):
        o_ref[...] = _norm_mod(x_ref[...], w_ref[...], sc_ref[...], sh_ref[...])

    def bidx(a):
        if a.shape[0] == 1:
            return lambda i: (0, 0, 0)
        return lambda i: (i // tpb, 0, 0)

    return pl.pallas_call(
        body,
        out_shape=jax.ShapeDtypeStruct((rows, d), F32),
        grid=(rows // tm,),
        in_specs=[pl.BlockSpec((tm, d), lambda i: (i, 0)),
                  pl.BlockSpec((1, d), lambda i: (0, 0)),
                  pl.BlockSpec((None, 1, d), bidx(sc)),
                  pl.BlockSpec((None, 1, d), bidx(sh))],
        out_specs=pl.BlockSpec((tm, d), lambda i: (i, 0)),
        compiler_params=_params(("parallel",), 6 * tm * d * 4),
        name="norm_mod",
    )(x, w, sc, sh)


def _dwconv(x, w, b, *, nb, seq, col_off, channels, act, out_dtype=F32):
    cb = channels // LANES
    rc = _pick(seq, (512, 256, 128))
    pad = SUBLANES

    def body(x_ref, w_ref, b_ref, o_ref, p_scr):
        p_scr[0:pad, :] = jnp.zeros((pad, LANES), F32)
        p_scr[pad + seq:pad + seq + pad, :] = jnp.zeros((pad, LANES), F32)
        p_scr[pad:pad + seq, :] = x_ref[...].astype(F32)
        wv = w_ref[...]
        bv = b_ref[...]

        def step(i, carry):
            r0 = pl.multiple_of(i * rc, rc)
            y = bv
            for t in range(CONV_WIDTH):
                y = y + wv[t:t + 1, :] * p_scr[pl.ds(r0 + pad - CONV_PAD_LEFT + t, rc), :]
            if act:
                y = _silu(y)
            o_ref[pl.ds(r0, rc), :] = y.astype(o_ref.dtype)
            return carry

        lax.fori_loop(0, seq // rc, step, 0)

    return pl.pallas_call(
        body,
        out_shape=jax.ShapeDtypeStruct((nb * seq, channels), out_dtype),
        grid=(nb, cb),
        in_specs=[pl.BlockSpec((seq, LANES), lambda bi, ci: (bi, col_off + ci)),
                  pl.BlockSpec((CONV_WIDTH, LANES), lambda bi, ci: (0, ci)),
                  pl.BlockSpec((1, LANES), lambda bi, ci: (0, ci))],
        out_specs=pl.BlockSpec((seq, LANES), lambda bi, ci: (bi, ci)),
        scratch_shapes=[pltpu.VMEM((seq + 2 * pad, LANES), F32)],
        compiler_params=_params(("parallel", "parallel"), 5 * seq * LANES * 4),
        name="dwconv",
    )(x, w, b)


def _ssd_direction(xbc, dt, par, dskip, h0, yprev, consts, *, nb, seq, reverse):
    tri, e_head, e_wide = consts
    q = SSD_CHUNK
    nc = seq // q
    inner = xbc.shape[1] - 2 * M2_GROUPS * M2_STATE
    gw = inner // M2_GROUPS
    heads = inner // M2_HEADDIM
    hpg = heads // M2_GROUPS
    has_prev = yprev is not None
    has_skip = dskip is not None

    def body(*refs):
        it = iter(refs)
        xbc_ref, dt_ref, par_ref = next(it), next(it), next(it)
        dsk_ref = next(it) if has_skip else None
        h0_ref = next(it)
        yp_ref = next(it) if has_prev else None
        tri_ref, eh_ref, ew_ref = next(it), next(it), next(it)
        y_ref, hout_ref, h_scr = next(it), next(it), next(it)
        c = pl.program_id(1)

        @pl.when(c == 0)
        def _():
            h_scr[...] = h0_ref[...]

        dtv = _softplus(dt_ref[...] + par_ref[0:1, :])
        da = dtv * (-jnp.exp(par_ref[1:2, :]))
        acs = _select_rows(tri_ref[...], da)
        acs_t = acs.T
        dt_t = dtv.T
        last = acs[0:1, :] if reverse else acs[q - 1:q, :]
        eh = eh_ref[...]
        w_e = _select_cols(dtv * jnp.exp(last - acs), eh, parts=2)
        cdec_e = _select_cols(jnp.broadcast_to(jnp.exp(last), (SUBLANES, LANES)), eh)[0:1, :]
        acol = _select_cols(acs, ew_ref[...])
        x_b = xbc_ref[:, 0:inner].astype(BF16)
        xs = xbc_ref[:, 0:inner].astype(F32)
        x_d = (xs * w_e).astype(BF16)
        row = lax.broadcasted_iota(jnp.int32, (q, q), 0)
        col = lax.broadcasted_iota(jnp.int32, (q, q), 1)
        mask = (col >= row) if reverse else (col <= row)
        lane = lax.broadcasted_iota(jnp.int32, (q, LANES), 1)
        for g in range(M2_GROUPS):
            bm = xbc_ref[:, inner + g * M2_STATE:inner + (g + 1) * M2_STATE]
            cm = xbc_ref[:, inner + (M2_GROUPS + g) * M2_STATE:inner + (M2_GROUPS + g + 1) * M2_STATE]
            scores = lax.dot_general(cm.astype(BF16), bm.astype(BF16), (((1,), (1,)), ((), ())),
                                     preferred_element_type=F32)
            h_t = h_scr[g]
            h_b = h_t.astype(BF16)
            s_t = jnp.dot(bm.astype(F32).T.astype(BF16), x_d[:, g * gw:(g + 1) * gw], preferred_element_type=F32)
            h_scr[g] = h_t * cdec_e[:, g * gw:(g + 1) * gw] + s_t
            for pr in range(hpg // 2):
                k0 = g * hpg + 2 * pr
                c0 = k0 * M2_HEADDIM
                rhs = jnp.concatenate([x_b[:, c0:c0 + LANES], h_b[:, 2 * pr * M2_HEADDIM:2 * pr * M2_HEADDIM + LANES]],
                                      axis=0)
                ys = []
                for k in (k0, k0 + 1):
                    a_col = acol[:, k * LANES:(k + 1) * LANES]
                    decay = jnp.exp(jnp.where(mask, a_col - acs_t[k:k + 1, :], -jnp.inf))
                    m = (scores * decay * dt_t[k:k + 1, :]).astype(BF16)
                    c_e = (cm.astype(F32) * jnp.exp(a_col)).astype(BF16)
                    ys.append(jnp.dot(jnp.concatenate([m, c_e], axis=1), rhs, preferred_element_type=F32))
                y = jnp.where(lane < M2_HEADDIM, ys[0], ys[1])
                if has_skip:
                    y = y + dsk_ref[:, c0:c0 + LANES] * xs[:, c0:c0 + LANES]
                if has_prev:
                    y = y + yp_ref[:, c0:c0 + LANES]
                y_ref[:, c0:c0 + LANES] = y

        @pl.when(c == nc - 1)
        def _():
            hout_ref[...] = h_scr[...]

    def rb(bi, ci):
        return bi * nc + ((nc - 1 - ci) if reverse else ci)

    width = xbc.shape[1]
    st_shape = (M2_GROUPS, M2_STATE, gw)
    ins = [xbc, dt, par]
    in_specs = [pl.BlockSpec((q, width), lambda bi, ci: (rb(bi, ci), 0)),
                pl.BlockSpec((q, LANES), lambda bi, ci: (rb(bi, ci), 0)),
                pl.BlockSpec((SUBLANES, LANES), lambda bi, ci: (0, 0))]
    if has_skip:
        ins.append(dskip)
        in_specs.append(pl.BlockSpec((1, inner), lambda bi, ci: (0, 0)))
    ins.append(h0)
    in_specs.append(pl.BlockSpec((None,) + st_shape, lambda bi, ci: (bi, 0, 0, 0)))
    if has_prev:
        ins.append(yprev)
        in_specs.append(pl.BlockSpec((q, inner), lambda bi, ci: (rb(bi, ci), 0)))
    ins += [tri, e_head, e_wide]
    in_specs += [pl.BlockSpec(tri.shape, lambda bi, ci: (0, 0)),
                 pl.BlockSpec(e_head.shape, lambda bi, ci: (0, 0)),
                 pl.BlockSpec(e_wide.shape, lambda bi, ci: (0, 0))]
    est = (q * width * 4 * 2 + q * inner * 4 * 4 + 3 * math.prod(st_shape) * 4 * 2
           + (e_head.size + e_wide.size) * 2 * 2 + 12 * q * inner * 4 + q * heads * LANES * 4)
    y, h_last = pl.pallas_call(
        body,
        out_shape=[jax.ShapeDtypeStruct((nb * seq, inner), F32),
                   jax.ShapeDtypeStruct((nb,) + st_shape, F32)],
        grid=(nb, nc),
        in_specs=in_specs,
        out_specs=[pl.BlockSpec((q, inner), lambda bi, ci: (rb(bi, ci), 0)),
                   pl.BlockSpec((None,) + st_shape, lambda bi, ci: (bi, 0, 0, 0))],
        scratch_shapes=[pltpu.VMEM(st_shape, F32)],
        compiler_params=_params(("parallel", "arbitrary"), est),
        name="ssd_bwd" if reverse else "ssd_fwd",
    )(*ins)
    return y, h_last


def _ssd_consts(inner, reverse):
    heads = inner // M2_HEADDIM
    r = jnp.arange(SSD_CHUNK)
    tri = (r[None, :] >= r[:, None]) if reverse else (r[None, :] <= r[:, None])
    hk = jnp.arange(LANES)[:, None]
    e_head = (hk == (jnp.arange(inner)[None, :] // M2_HEADDIM))
    e_wide = (hk == (jnp.arange(heads * LANES)[None, :] // LANES))
    return tri.astype(BF16), e_head.astype(BF16), e_wide.astype(BF16)


def _tile_scan(a, u, reverse):
    row = lax.broadcasted_iota(jnp.int32, a.shape, 0)
    for d in (1, 2, 4):
        shift = (SUBLANES - d) if reverse else d
        valid = (row < SUBLANES - d) if reverse else (row >= d)
        a_s = jnp.where(valid, pltpu.roll(a, shift, 0), 1.0)
        u_s = jnp.where(valid, pltpu.roll(u, shift, 0), 0.0)
        u = a * u_s + u
        a = a * a_s
    return a, u


def _rglru_direction(xc, gate_w, gate_b, a_param, h0, rprev, *, nb, seq, reverse):
    width = xc.shape[1]
    nblk = width // LRU_BLOCK
    tb = _pick(seq, (256, 128))
    nt = seq // tb
    has_prev = rprev is not None

    def body(*refs):
        it = iter(refs)
        x_ref, gw_ref, gb_ref, ap_ref, h0_ref = next(it), next(it), next(it), next(it), next(it)
        rp_ref = next(it) if has_prev else None
        r_ref, hout_ref, a_scr, u_scr, h_scr = next(it), next(it), next(it), next(it), next(it)
        tstep = pl.program_id(1)

        @pl.when(tstep == 0)
        def _():
            h_scr[...] = jnp.broadcast_to(h0_ref[...], (SUBLANES, width))

        log_base = -LRU_C * _softplus(-ap_ref[...])
        for nbk in range(nblk):
            cs = slice(nbk * LRU_BLOCK, (nbk + 1) * LRU_BLOCK)
            xb = x_ref[:, cs]
            g = jnp.dot(xb.astype(BF16), gw_ref[nbk], preferred_element_type=F32) + gb_ref[nbk]
            g = _sigmoid(g)
            log_a = g[:, :LRU_BLOCK] * log_base[:, cs]
            a = jnp.exp(log_a)
            mult = jnp.sqrt(jnp.maximum(1.0 - a * a, 0.0))
            a_scr[:, cs] = a
            u_scr[:, cs] = xb * g[:, LRU_BLOCK:] * mult

        def step(i, h):
            grp = (tb // SUBLANES - 1 - i) if reverse else i
            r0 = pl.multiple_of(grp * SUBLANES, SUBLANES)
            a_c, h_loc = _tile_scan(a_scr[pl.ds(r0, SUBLANES), :], u_scr[pl.ds(r0, SUBLANES), :], reverse)
            h_new = a_c * h + h_loc
            out = h_new
            if has_prev:
                out = out + rp_ref[pl.ds(r0, SUBLANES), :]
            r_ref[pl.ds(r0, SUBLANES), :] = out
            edge = h_new[0:1, :] if reverse else h_new[SUBLANES - 1:SUBLANES, :]
            return jnp.broadcast_to(edge, (SUBLANES, width))

        h_fin = lax.fori_loop(0, tb // SUBLANES, step, h_scr[...])
        h_scr[...] = h_fin

        @pl.when(tstep == nt - 1)
        def _():
            hout_ref[...] = h_fin[0:1, :]

    def rb(bi, ti):
        return bi * nt + ((nt - 1 - ti) if reverse else ti)

    ins = [xc, gate_w, gate_b, a_param, h0]
    in_specs = [pl.BlockSpec((tb, width), lambda bi, ti: (rb(bi, ti), 0)),
                pl.BlockSpec(gate_w.shape, lambda bi, ti: (0, 0, 0)),
                pl.BlockSpec(gate_b.shape, lambda bi, ti: (0, 0, 0)),
                pl.BlockSpec((1, width), lambda bi, ti: (0, 0)),
                pl.BlockSpec((None, 1, width), lambda bi, ti: (bi, 0, 0))]
    if has_prev:
        ins.append(rprev)
        in_specs.append(pl.BlockSpec((tb, width), lambda bi, ti: (rb(bi, ti), 0)))
    est = 10 * tb * width * 4 + gate_w.size * 2 * 2
    r, h_last = pl.pallas_call(
        body,
        out_shape=[jax.ShapeDtypeStruct((nb * seq, width), F32),
                   jax.ShapeDtypeStruct((nb, 1, width), F32)],
        grid=(nb, nt),
        in_specs=in_specs,
        out_specs=[pl.BlockSpec((tb, width), lambda bi, ti: (rb(bi, ti), 0)),
                   pl.BlockSpec((None, 1, width), lambda bi, ti: (bi, 0, 0))],
        scratch_shapes=[pltpu.VMEM((tb, width), F32), pltpu.VMEM((tb, width), F32),
                        pltpu.VMEM((SUBLANES, width), F32)],
        compiler_params=_params(("parallel", "arbitrary"), est),
        name="rglru_bwd" if reverse else "rglru_fwd",
    )(*ins)
    return r, h_last


def _s5_bidir(u, w, pw_f, pw_b, h_in, *, nb, seq):
    d_model = u.shape[1]
    lbs = d_model // LANES
    sw = pw_f.shape[3]
    m_t = seq // S5_TILE
    wide = S5_TILE * LANES

    def scan(s_scr, pw_ref, c_re, c_im, reverse):
        row = lax.broadcasted_iota(jnp.int32, (SUBLANES, sw), 0)

        def step(i, carry):
            hr, hi = carry
            grp = (m_t // SUBLANES - 1 - i) if reverse else i
            r0 = pl.multiple_of(grp * SUBLANES, SUBLANES)
            xr = s_scr[pl.ds(r0, SUBLANES), 0:sw]
            xi = s_scr[pl.ds(r0, SUBLANES), sw:2 * sw]
            for n_step, dist in enumerate((1, 2, 4)):
                shift = (SUBLANES - dist) if reverse else dist
                pr = pw_ref[2 * n_step]
                pi = pw_ref[2 * n_step + 1]
                sr = pltpu.roll(xr, shift, 0)
                si = pltpu.roll(xi, shift, 0)
                xr, xi = xr + (pr * sr - pi * si), xi + (pr * si + pi * sr)
            nr = xr + (pw_ref[6] * hr - pw_ref[7] * hi)
            ni = xi + (pw_ref[6] * hi + pw_ref[7] * hr)
            edge = (SUBLANES - 1) if reverse else 0
            back = 1 if not reverse else SUBLANES - 1
            s_scr[pl.ds(r0, SUBLANES), 0:sw] = jnp.where(row == edge, hr, pltpu.roll(nr, back, 0))
            s_scr[pl.ds(r0, SUBLANES), sw:2 * sw] = jnp.where(row == edge, hi, pltpu.roll(ni, back, 0))
            e = 0 if reverse else SUBLANES - 1
            return (jnp.broadcast_to(nr[e:e + 1, :], (SUBLANES, sw)),
                    jnp.broadcast_to(ni[e:e + 1, :], (SUBLANES, sw)))

        init = (jnp.broadcast_to(c_re, (SUBLANES, sw)), jnp.broadcast_to(c_im, (SUBLANES, sw)))
        return lax.fori_loop(0, m_t // SUBLANES, step, init)

    def body(u_ref, t_ref, pf_ref, pb_ref, of_ref, ob_ref, pwf_ref, pwb_ref, hin_ref,
             y_ref, hout_ref, x_scr, f_scr, g_scr):
        for l in range(S5_TILE):
            x_scr[:, l * LANES:(l + 1) * LANES] = u_ref[pl.ds(l, m_t, stride=S5_TILE), :].astype(BF16)
        x = x_scr[...]
        f_scr[...] = jnp.dot(x, pf_ref[...], preferred_element_type=F32)
        g_scr[...] = jnp.dot(x, pb_ref[...], preferred_element_type=F32)
        fr, fi = scan(f_scr, pwf_ref, hin_ref[0:1, :], hin_ref[1:2, :], False)
        gr, gi = scan(g_scr, pwb_ref, hin_ref[2:3, :], hin_ref[3:4, :], True)
        hout_ref[0:1, :] = fr[0:1, :]
        hout_ref[1:2, :] = fi[0:1, :]
        hout_ref[2:3, :] = gr[0:1, :]
        hout_ref[3:4, :] = gi[0:1, :]
        y = (jnp.dot(x, t_ref[...], preferred_element_type=F32)
             + jnp.dot(f_scr[...].astype(BF16), of_ref[...], preferred_element_type=F32)
             + jnp.dot(g_scr[...].astype(BF16), ob_ref[...], preferred_element_type=F32))
        for l in range(S5_TILE):
            y_ref[pl.ds(l, m_t, stride=S5_TILE), :] = y[:, l * LANES:(l + 1) * LANES]

    def wspec():
        return pl.BlockSpec((None, wide, wide), lambda li, bi: (li, 0, 0), pipeline_mode=pl.Buffered(1))

    est = (4 * seq * LANES * 4 + 5 * wide * wide * 2 + m_t * wide * 2 + 2 * m_t * 2 * sw * 4
           + 3 * m_t * wide * 4)
    y, h_out = pl.pallas_call(
        body,
        out_shape=[jax.ShapeDtypeStruct((nb * seq, d_model), F32),
                   jax.ShapeDtypeStruct((nb, lbs, 4, sw), F32)],
        grid=(lbs, nb),
        in_specs=[pl.BlockSpec((seq, LANES), lambda li, bi: (bi, li)),
                  wspec(), wspec(), wspec(), wspec(), wspec(),
                  pl.BlockSpec((None, 8, SUBLANES, sw), lambda li, bi: (li, 0, 0, 0)),
                  pl.BlockSpec((None, 8, SUBLANES, sw), lambda li, bi: (li, 0, 0, 0)),
                  pl.BlockSpec((None, None, 4, sw), lambda li, bi: (bi, li, 0, 0))],
        out_specs=[pl.BlockSpec((seq, LANES), lambda li, bi: (bi, li)),
                   pl.BlockSpec((None, None, 4, sw), lambda li, bi: (bi, li, 0, 0))],
        scratch_shapes=[pltpu.VMEM((m_t, wide), BF16), pltpu.VMEM((m_t, 2 * sw), F32),
                        pltpu.VMEM((m_t, 2 * sw), F32)],
        compiler_params=_params(("parallel", "parallel"), est),
        name="s5_bidir",
    )(u, w["t"], w["pf"], w["pb"], w["of"], w["ob"], pw_f, pw_b, h_in)
    return y, h_out


def _attention(q, kv_parts, *, nb, lq, scale):
    heads = q.shape[1] // HEAD_PAD
    tq = _pick(lq, (1024, 512, 256, 128))
    tsub = _pick(tq, (256, 128))
    nq = lq // tq
    c = scale * math.log2(math.e)
    chunks = [_pick(length, (2048, 1024, 512, 256, 128)) for (_, _, length) in kv_parts]
    n_parts = len(kv_parts)

    def body(*refs):
        q_ref = refs[0]
        kv_refs = refs[1:1 + 2 * n_parts]
        o_ref = refs[1 + 2 * n_parts]
        s_a, s_b = refs[2 + 2 * n_parts], refs[3 + 2 * n_parts]

        def q_tile(t, carry):
            r0 = pl.multiple_of(t * tsub, tsub)
            qt = q_ref[pl.ds(r0, tsub), :]

            def halves(n):
                return ((0, n // 2), (n // 2, n)) if n >= 2 * MAX_FOLD_ROWS else ((0, n),)

            def scores(k):
                return jnp.concatenate(
                    [lax.dot_general(k[a:b], qt, (((1,), (1,)), ((), ())), preferred_element_type=F32)
                     for a, b in halves(k.shape[0])], axis=0)

            def absorb(s_t, v_t, m, acc):
                part = jnp.max(s_t.reshape(-1, MAX_FOLD_ROWS, tsub), axis=0)
                m_new = jnp.maximum(m, jnp.max(part, axis=0, keepdims=True))
                p_t = jnp.exp2((s_t - m_new) * c).astype(BF16)
                alpha = jnp.exp2((m - m_new) * c)
                acc = alpha * acc
                for a, b in halves(s_t.shape[0]):
                    acc = acc + jnp.dot(v_t[:, a:b], p_t[a:b], preferred_element_type=F32)
                return m_new, acc

            m = jnp.full((1, tsub), -jnp.inf, F32)
            acc = jnp.zeros((HEAD_PAD, tsub), F32)
            sched = [(kv_refs[2 * pi], kv_refs[2 * pi + 1], ci, chunks[pi])
                     for pi, (_, _, length) in enumerate(kv_parts) for ci in range(length // chunks[pi])]
            bufs = (s_a, s_b)

            def issue(i):
                k_ref, _, ci, tk = sched[i]
                bufs[i % 2][0:tk, :] = scores(k_ref[ci * tk:(ci + 1) * tk, :])

            issue(0)
            for i, (_, v_ref, ci, tk) in enumerate(sched):
                if i + 1 < len(sched):
                    issue(i + 1)
                m, acc = absorb(bufs[i % 2][0:tk, :], v_ref[ci], m, acc)
            o_t = acc / acc[V_ONES_LANE:V_ONES_LANE + 1, :]
            o_ref[pl.ds(r0, tsub), :] = o_t.T.astype(o_ref.dtype)
            return carry

        lax.fori_loop(0, tq // tsub, q_tile, 0)

    ins = [q]
    in_specs = [pl.BlockSpec((tq, HEAD_PAD), lambda b, h, qi: (b * nq + qi, h))]
    est = tq * HEAD_PAD * 2 * 4 + 6 * tsub * max(chunks) * 4
    for (k, v, length), tk in zip(kv_parts, chunks):
        n_ch = length // tk
        v_t = v.reshape(nb, n_ch, tk, heads, HEAD_PAD).transpose(0, 3, 1, 4, 2)
        ins += [k, v_t]
        in_specs += [pl.BlockSpec((length, HEAD_PAD), lambda b, h, qi: (b, h)),
                     pl.BlockSpec((None, None, n_ch, HEAD_PAD, tk), lambda b, h, qi: (b, h, 0, 0, 0))]
        est += 2 * length * HEAD_PAD * 2 * 2
    return pl.pallas_call(
        body,
        out_shape=jax.ShapeDtypeStruct(q.shape, BF16),
        grid=(nb, heads, nq),
        in_specs=in_specs,
        out_specs=pl.BlockSpec((tq, HEAD_PAD), lambda b, h, qi: (b * nq + qi, h)),
        scratch_shapes=[pltpu.VMEM((max(chunks), tsub), F32), pltpu.VMEM((max(chunks), tsub), F32)],
        compiler_params=_params(("parallel", "parallel", "arbitrary"), est),
        name="attention",
    )(*ins)


class _Stream:
    def __init__(self, x, nb, seq, mods):
        self.x = x
        self.nb = nb
        self.seq = seq
        self.tm = _pick(seq, (512, 256, 128))
        self.tpb = seq // self.tm
        self.mods = mods
        self.rows = nb * seq


def _in_proj(st, norm_w, sc, sh, weights, n, tn, out_dtypes=None):
    d = st.x.shape[1]
    pro = [_rows_in(st.x, st.tm, d), _const_in(norm_w, d), _batch_in(sc, st.tpb, d), _batch_in(sh, st.tpb, d)]
    return _fused_matmul(pro, weights, [], _norm_mod, lambda accs: accs, rows=st.rows, tm=st.tm, k=d, n=n,
                         tn=tn, out_dtypes=out_dtypes or [F32] * len(weights))


def _out_proj(st, pro_ins, prologue, weight, gate, k):
    d = st.x.shape[1]
    epi = [_tile_in(st.x, st.tm, d), _batch_tile_in(gate, st.tpb, d)]
    (out,) = _fused_matmul(pro_ins, [weight], epi, prologue, lambda accs, x, g: [x + g * accs[0]],
                           rows=st.rows, tm=st.tm, k=k, n=d, tn=d, out_dtypes=[F32])
    return out


def _ffn(st, norm_w, sc, sh, gate, w1, w3, w2):
    d = st.x.shape[1]
    hidden = w1.shape[1]
    th = _pick(hidden, (1408, 1024, 512, 256, 128))
    tm, tpb = st.tm, st.tpb
    nh = hidden // th

    def body(x_ref, nw_ref, sc_ref, sh_ref, w1_ref, w3_ref, w2_ref, g_ref, o_ref, a_scr, acc_scr):
        j = pl.program_id(1)

        @pl.when(j == 0)
        def _():
            a_scr[...] = _norm_mod(x_ref[...], nw_ref[...], sc_ref[...], sh_ref[...]).astype(BF16)
            acc_scr[...] = jnp.zeros((tm, d), F32)

        a = a_scr[...]
        up = (_silu(jnp.dot(a, w1_ref[...], preferred_element_type=F32))
              * jnp.dot(a, w3_ref[...], preferred_element_type=F32))
        acc_scr[...] += jnp.dot(up.astype(BF16), w2_ref[...], preferred_element_type=F32)

        @pl.when(j == nh - 1)
        def _():
            o_ref[...] = x_ref[...] + g_ref[...] * acc_scr[...]

    def bidx(a):
        if a.shape[0] == 1:
            return lambda i, j: (0, 0, 0)
        return lambda i, j: (i // tpb, 0, 0)

    est = 4 * tm * d * 4 + 3 * d * th * 2 * 2 + tm * d * 6 + 4 * tm * th * 4
    return pl.pallas_call(
        body,
        out_shape=jax.ShapeDtypeStruct((st.rows, d), F32),
        grid=(st.rows // tm, nh),
        in_specs=[pl.BlockSpec((tm, d), lambda i, j: (i, 0)),
                  pl.BlockSpec((1, d), lambda i, j: (0, 0)),
                  pl.BlockSpec((None, 1, d), bidx(sc)),
                  pl.BlockSpec((None, 1, d), bidx(sh)),
                  pl.BlockSpec((d, th), lambda i, j: (0, j)),
                  pl.BlockSpec((d, th), lambda i, j: (0, j)),
                  pl.BlockSpec((th, d), lambda i, j: (j, 0)),
                  pl.BlockSpec((None, 1, d), bidx(gate))],
        out_specs=pl.BlockSpec((tm, d), lambda i, j: (i, 0)),
        scratch_shapes=[pltpu.VMEM((tm, d), BF16), pltpu.VMEM((tm, d), F32)],
        compiler_params=_params(("parallel", "arbitrary"), est),
        name="ffn",
    )(st.x, norm_w, sc, sh, w1, w3, w2, gate)


def _mamba_layer(lat, ctx, p, ctx_out):
    inner = p["inner"]
    conv_ch = p["conv_ch"]
    heads = inner // M2_HEADDIM
    gw = inner // M2_GROUPS
    outs = {}
    acts = {}
    for name, st in (("ctx", ctx), ("lat", lat)):
        m = st.mods
        (zx,) = _in_proj(st, p["norm1"], m["sc1"], m["sh1"], [p["w_zx"]], inner + conv_ch,
                         _pick(inner + conv_ch, (1024, 512, 256, 128)), out_dtypes=[BF16])
        dt0, dt1 = _in_proj(st, p["norm1"], m["sc1"], m["sh1"], [p["w_dt0"], p["w_dt1"]], LANES, LANES)
        xbc = _dwconv(zx, p["conv_w"], p["conv_b"], nb=st.nb, seq=st.seq, col_off=inner // LANES,
                      channels=conv_ch, act=True, out_dtype=BF16)
        acts[name] = (zx, (dt0, dt1), xbc)
    ys = {}
    for dirn, reverse in ((0, False), (1, True)):
        consts = _ssd_consts(inner, reverse)
        h = jnp.zeros((ctx.nb, M2_GROUPS, M2_STATE, gw), F32)
        for name, st in (("ctx", ctx), ("lat", lat)):
            zx, dts, xbc = acts[name]
            y, h = _ssd_direction(xbc, dts[dirn], p["par"][dirn], p["dskip"] if dirn == 0 else None, h,
                                  ys.get(name), consts, nb=st.nb, seq=st.seq, reverse=reverse)
            ys[name] = y

    def gated_norm(y, z, w):
        return _rms(y * _silu(z.astype(F32)), w)

    for name, st in (("ctx", ctx), ("lat", lat)):
        if name == "ctx" and not ctx_out:
            continue
        zx = acts[name][0]
        pro = [_rows_in(ys[name], st.tm, inner), _rows_in(zx, st.tm, inner), _const_in(p["norm_w"], inner)]
        outs[name] = _out_proj(st, pro, gated_norm, p["w_out"], st.mods["g1"], inner)
    return outs


def _rglru_layer(lat, ctx, p, ctx_out):
    width = p["width"]
    outs = {}
    acts = {}
    for name, st in (("ctx", ctx), ("lat", lat)):
        m = st.mods
        (yx,) = _in_proj(st, p["norm1"], m["sc1"], m["sh1"], [p["w_in"]], 2 * width,
                         _pick(2 * width, (1280, 1024, 512, 256, 128)), out_dtypes=[BF16])
        xc = _dwconv(yx, p["conv_w"], p["conv_b"], nb=st.nb, seq=st.seq, col_off=width // LANES,
                     channels=width, act=False)
        acts[name] = (yx, xc)
    rs = {}
    for dirn, reverse in ((0, False), (1, True)):
        h = jnp.zeros((ctx.nb, 1, width), F32)
        for name, st in (("ctx", ctx), ("lat", lat)):
            r, h = _rglru_direction(acts[name][1], p["gate_w"][dirn], p["gate_b"][dirn], p["a_param"][dirn], h,
                                    rs.get(name), nb=st.nb, seq=st.seq, reverse=reverse)
            rs[name] = r
    for name, st in (("ctx", ctx), ("lat", lat)):
        if name == "ctx" and not ctx_out:
            continue
        pro = [_rows_in(acts[name][0], st.tm, width), _rows_in(rs[name], st.tm, width)]
        outs[name] = _out_proj(st, pro, lambda y, r: _gelu_tanh(y.astype(F32)) * r, p["w_out"], st.mods["g1"], width)
    return outs


def _s5_layer(lat, ctx, p, ctx_out):
    d = lat.x.shape[1]
    lbs = d // LANES
    sw = p["pw"][0].shape[3]
    us = {}
    for name, st in (("ctx", ctx), ("lat", lat)):
        m = st.mods
        us[name] = _norm_mod_rows(st.x, p["norm1"], m["sc1"], m["sh1"], tm=st.tm, tpb=st.tpb)
    ys = {}
    h = jnp.zeros((ctx.nb, lbs, 4, sw), F32)
    for name, st in (("ctx", ctx), ("lat", lat)):
        ys[name], h = _s5_bidir(us[name], p["w"], p["pw"][0], p["pw"][1], h, nb=st.nb, seq=st.seq)
    outs = {}
    for name, st in (("ctx", ctx), ("lat", lat)):
        if name == "ctx" and not ctx_out:
            continue
        pro = [_rows_in(ys[name], st.tm, d), _rows_in(us[name], st.tm, d), _const_in(p["dskip"], d)]
        epi = [_const_tile_in(p["glu_ba"], d), _const_tile_in(p["glu_bg"], d),
               _tile_in(st.x, st.tm, d), _batch_tile_in(st.mods["g1"], st.tpb, d)]
        (out,) = _fused_matmul(
            pro, [p["glu_wa"], p["glu_wg"]], epi,
            lambda y, u, dsk: _gelu_tanh(y + dsk * u),
            lambda accs, ba, bg, x, g: [x + g * ((accs[0] + ba) * _sigmoid(accs[1] + bg))],
            rows=st.rows, tm=st.tm, k=d, n=d, tn=d, out_dtypes=[F32])
        outs[name] = out
    return outs


def _mla_layer(lat, ctx, p, ctx_out):
    d = lat.x.shape[1]
    qr, kvr = p["q_rank"], p["kv_rank"]
    hp = MLA_HEADS * HEAD_PAD
    tn_h = _pick(hp, (512, 256, 128))
    reps = tn_h // HEAD_PAD
    qkv = {}
    for name, st in (("ctx", ctx), ("lat", lat)):
        m = st.mods
        n_in = p["w_in"].shape[1]
        (lat_all,) = _in_proj(st, p["norm1"], m["sc1"], m["sh1"], [p["w_in"]], n_in, n_in)
        cos_t, sin_t = p["tables"][name]
        tab_spec = lambda a, st=st: (a, (st.tm, HEAD_PAD), lambda i, j: (i % st.tpb, 0))

        def rope_epi(accs, cos, sin, reps=reps):
            cos_r = jnp.concatenate([cos] * reps, axis=1)
            sin_r = jnp.concatenate([sin] * reps, axis=1)
            return [accs[0] * cos_r + accs[1] * sin_r]

        want_q = name == "lat" or ctx_out
        q = None
        if want_q:
            (q,) = _fused_matmul([_rows_in(lat_all, st.tm, qr, 0), _const_in(p["q_norm"], qr)],
                                 [p["w_qa"], p["w_qb"]], [tab_spec(cos_t), tab_spec(sin_t)],
                                 _rms, rope_epi, rows=st.rows, tm=st.tm, k=qr, n=hp, tn=tn_h, out_dtypes=[BF16])

        def kv_epi(accs, kra, krb, cos, sin, ones, reps=reps):
            kr = kra * cos + krb * sin
            return [accs[0] + jnp.concatenate([kr] * reps, axis=1), accs[1] + ones]

        kra_blk = (qr // LANES)
        kv_blk = (qr + LANES) // kvr
        krb_blk = (qr + LANES + kvr) // LANES
        k, v = _fused_matmul(
            [_rows_in(lat_all, st.tm, kvr, kv_blk), _const_in(p["kv_norm"], kvr)],
            [p["w_k"], p["w_v"]],
            [(lat_all, (st.tm, LANES), lambda i, j: (i, kra_blk)),
             (lat_all, (st.tm, LANES), lambda i, j: (i, krb_blk)),
             tab_spec(cos_t), tab_spec(sin_t), _const_tile_in(p["v_ones"], tn_h)],
            _rms, kv_epi, rows=st.rows, tm=st.tm, k=kvr, n=hp, tn=tn_h, out_dtypes=[BF16, BF16])
        qkv[name] = (q, k, v)
    scale = (MLA_NOPE + MLA_ROPE) ** -0.5
    nb = lat.nb
    kv_c = (qkv["ctx"][1], qkv["ctx"][2], ctx.seq)
    kv_l = (qkv["lat"][1], qkv["lat"][2], lat.seq)
    outs = {}
    o_l = _attention(qkv["lat"][0], [kv_c, kv_l], nb=nb, lq=lat.seq, scale=scale)
    outs["lat"] = _out_proj(lat, [_rows_in(o_l, lat.tm, hp)], lambda a: a, p["w_out"], lat.mods["g1"], hp)
    if ctx_out:
        o_c = _attention(qkv["ctx"][0], [kv_c], nb=nb, lq=ctx.seq, scale=scale)
        outs["ctx"] = _out_proj(ctx, [_rows_in(o_c, ctx.tm, hp)], lambda a: a, p["w_out"], ctx.mods["g1"], hp)
    return outs


def _pad_cols(w, n):
    return jnp.pad(w, ((0, 0), (0, n - w.shape[1])))


def _mamba_params(in_w, conv_w, conv_b, dt_bias, a_log, d_skip, norm_w, out_w):
    d, proj = in_w.shape
    heads = dt_bias.shape[1]
    inner = heads * M2_HEADDIM
    conv_ch = inner + 2 * M2_GROUPS * M2_STATE
    par = []
    for dirn in range(2):
        rows = jnp.zeros((SUBLANES, LANES), F32)
        rows = rows.at[0, :heads].set(dt_bias[dirn].astype(F32)).at[1, :heads].set(a_log[dirn].astype(F32))
        par.append(rows)
    off = inner + conv_ch
    return dict(
        inner=inner, conv_ch=conv_ch,
        w_zx=in_w[:, :off].astype(BF16),
        w_dt0=_pad_cols(in_w[:, off:off + heads], LANES).astype(BF16),
        w_dt1=_pad_cols(in_w[:, off + heads:off + 2 * heads], LANES).astype(BF16),
        conv_w=conv_w.astype(F32), conv_b=conv_b.astype(F32)[None, :],
        par=par, dskip=jnp.repeat(d_skip.astype(F32), M2_HEADDIM)[None, :],
        norm_w=norm_w.astype(F32)[None, :], w_out=out_w.astype(BF16))


def _rglru_params(in_w, conv_w, conv_b, gate_w, gate_b, a_param, out_w):
    width = conv_w.shape[1]
    return dict(
        width=width, w_in=in_w.astype(BF16), conv_w=conv_w.astype(F32), conv_b=conv_b.astype(F32)[None, :],
        gate_w=[gate_w[dirn].astype(BF16) for dirn in range(2)],
        gate_b=[gate_b[dirn].astype(F32)[:, None, :] for dirn in range(2)],
        a_param=[a_param[dirn].astype(F32)[None, :] for dirn in range(2)],
        w_out=out_w.astype(BF16))


def _block_diag(m, per):
    g, r, c = m.shape
    m = m.reshape(g // per, per, r, c)
    eye = jnp.eye(per, dtype=m.dtype)
    return jnp.einsum("qarc,ab->qarbc", m, eye).reshape(g // per, per * r, per * c)


def _complex_powers(ar, ai, n):
    pr, pi = [jnp.ones_like(ar)], [jnp.zeros_like(ai)]
    for _ in range(n):
        pr, pi = pr + [pr[-1] * ar - pi[-1] * ai], pi + [pr[-1] * ai + pi[-1] * ar]
    return pr, pi


def _s5_params(lam_re, lam_im, log_step, b_re, b_im, c_re, c_im, d_skip, glu_w, glu_b):
    per = LANES // S5_GROUP
    hp = lax.Precision.HIGHEST
    br, bi = b_re.astype(F32), b_im.astype(F32)
    out = dict(pw=[])
    w = {}
    for dirn in range(2):
        reverse = dirn == 1
        lr = jnp.minimum(lam_re[dirn].astype(F32), -1e-4)
        li = lam_im[dirn].astype(F32)
        step = jnp.exp(log_step[dirn].astype(F32))[:, None]
        mag = jnp.exp(lr * step)
        abr, abi = mag * jnp.cos(li * step), mag * jnp.sin(li * step)
        den = lr * lr + li * li
        zr = ((abr - 1.0) * lr + abi * li) / den
        zi = (abi * lr - (abr - 1.0) * li) / den
        bbr = zr[..., None] * br - zi[..., None] * bi
        bbi = zr[..., None] * bi + zi[..., None] * br
        cr, ci = c_re[dirn].astype(F32), c_im[dirn].astype(F32)
        pr, pi = _complex_powers(abr, abi, S5_TILE)

        pra, pia = jnp.stack(pr), jnp.stack(pi)
        g = abr.shape[0]
        lbs, wide = g // per, S5_TILE * LANES
        wr = pra[..., None] * bbr - pia[..., None] * bbi
        wi = pra[..., None] * bbi + pia[..., None] * bbr

        def bdiag(m):
            e = m.shape[0]
            return _block_diag(m.reshape((e * g,) + m.shape[2:]), per).reshape(
                (e, lbs, per * m.shape[2], per * m.shape[3]))

        taps = (jnp.einsum("gjp,egpi->egij", cr, wr[:S5_TILE], precision=hp)
                - jnp.einsum("gjp,egpi->egij", ci, wi[:S5_TILE], precision=hp))
        taps = bdiag(jnp.concatenate([taps, jnp.zeros_like(taps[:1])], axis=0))
        pos = jnp.arange(S5_TILE)
        dist = (pos[:, None] - pos[None, :]) if reverse else (pos[None, :] - pos[:, None])
        t_dir = taps[jnp.where(dist >= 0, dist, S5_TILE)]
        t_dir = t_dir.transpose(2, 0, 3, 1, 4).reshape(lbs, wide, wide)
        w["t"] = t_dir if dirn == 0 else w["t"] + t_dir
        e_in = pos if reverse else S5_TILE - 1 - pos
        p_mat = jnp.concatenate([bdiag(jnp.swapaxes(wr[e_in], 2, 3)), bdiag(jnp.swapaxes(wi[e_in], 2, 3))],
                                axis=3)
        w["pb" if reverse else "pf"] = p_mat.transpose(1, 0, 2, 3).reshape(lbs, wide, -1).astype(BF16)
        e_out = (S5_TILE - pos) if reverse else (pos + 1)
        o_re = jnp.swapaxes(cr * pra[e_out][:, :, None, :] - ci * pia[e_out][:, :, None, :], 2, 3)
        o_im = jnp.swapaxes(-(cr * pia[e_out][:, :, None, :] + ci * pra[e_out][:, :, None, :]), 2, 3)
        o_mat = jnp.concatenate([bdiag(o_re), bdiag(o_im)], axis=2)
        w["ob" if reverse else "of"] = o_mat.transpose(1, 2, 0, 3).reshape(lbs, -1, wide).astype(BF16)
        qr, qi = _complex_powers(pr[S5_TILE], pi[S5_TILE], SUBLANES)
        g = abr.shape[0]
        flat = lambda a: a.reshape(g // per, per * a.shape[1])
        t = jnp.arange(SUBLANES)[None, :, None]
        tiles = []
        for dist in (1, 2, 4):
            valid = (t < SUBLANES - dist) if reverse else (t >= dist)
            tiles.append(jnp.where(valid, flat(qr[dist])[:, None, :], 0.0))
            tiles.append(jnp.where(valid, flat(qi[dist])[:, None, :], 0.0))
        order = list(range(SUBLANES, 0, -1)) if reverse else list(range(1, SUBLANES + 1))
        tiles.append(jnp.stack([flat(qr[o]) for o in order], axis=1))
        tiles.append(jnp.stack([flat(qi[o]) for o in order], axis=1))
        out["pw"].append(jnp.stack(tiles, axis=1).astype(F32))
    w["t"] = w["t"].astype(BF16)
    out["w"] = w
    d = glu_w.shape[0]
    out.update(dskip=d_skip.astype(F32)[None, :],
               glu_wa=glu_w[:, :d].astype(BF16), glu_wg=glu_w[:, d:].astype(BF16),
               glu_ba=glu_b[:d].astype(F32)[None, :], glu_bg=glu_b[d:].astype(F32)[None, :])
    return out


def _head_pad_cols(w, per_head, take):
    kdim = w.shape[0]
    w = w.reshape(kdim, MLA_HEADS, per_head)[:, :, take]
    w = jnp.pad(w, ((0, 0), (0, 0), (0, HEAD_PAD - w.shape[2])))
    return w.reshape(kdim, MLA_HEADS * HEAD_PAD)


def _rope_swap(w):
    idx = jnp.arange(MLA_ROPE)
    blk, pos = idx // (2 * ROPE_FREQ), idx % (2 * ROPE_FREQ)
    return w[..., blk * 2 * ROPE_FREQ + (pos + ROPE_FREQ) % (2 * ROPE_FREQ)]


def _mla_params(in_w, q_norm_w, kv_norm_w, qb_w, kvb_w, out_w, seq, ctx_len):
    d = in_w.shape[0]
    qr, kvr = q_norm_w.shape[0], kv_norm_w.shape[0]
    dq = MLA_NOPE + MLA_ROPE
    w_q, w_kv, w_kr = in_w[:, :qr], in_w[:, qr:qr + kvr], in_w[:, qr + kvr:]

    def place(w):
        return jnp.pad(w, ((0, 0), (MLA_NOPE, HEAD_PAD - dq)))

    w_in = jnp.concatenate([w_q, place(w_kr), w_kv, place(_rope_swap(w_kr))], axis=1).astype(BF16)
    qb = qb_w.reshape(qr, MLA_HEADS, dq)
    qa = jnp.pad(qb, ((0, 0), (0, 0), (0, HEAD_PAD - dq))).reshape(qr, -1)
    qb_sw = jnp.pad(_rope_swap(qb[:, :, MLA_NOPE:]), ((0, 0), (0, 0), (MLA_NOPE, HEAD_PAD - dq))).reshape(qr, -1)
    per = MLA_NOPE + MLA_V
    w_k = _head_pad_cols(kvb_w, per, slice(0, MLA_NOPE))
    w_v = _head_pad_cols(kvb_w, per, slice(MLA_NOPE, per))
    w_out = jnp.pad(out_w.reshape(MLA_HEADS, MLA_V, d), ((0, 0), (0, HEAD_PAD - MLA_V), (0, 0))).reshape(-1, d)
    rows = seq // GRID_W
    row = jnp.repeat(jnp.arange(rows, dtype=F32), GRID_W)
    col = jnp.tile(jnp.arange(GRID_W, dtype=F32), rows)
    inv_freq = ROPE_BASE ** (-jnp.arange(ROPE_FREQ, dtype=F32) / ROPE_FREQ)
    ang = jnp.stack([row[:, None] * inv_freq, col[:, None] * inv_freq], axis=1)
    cos, sin = jnp.cos(ang), jnp.sin(ang)
    cos32 = jnp.concatenate([cos, cos], axis=2).reshape(seq, MLA_ROPE)
    sin32 = jnp.concatenate([-sin, sin], axis=2).reshape(seq, MLA_ROPE)
    ones = jnp.ones((seq, MLA_NOPE), F32)
    cos_l = jnp.pad(jnp.concatenate([ones, cos32], axis=1), ((0, 0), (0, HEAD_PAD - dq)))
    sin_l = jnp.pad(sin32, ((0, 0), (MLA_NOPE, HEAD_PAD - dq)))
    cos_c = jnp.pad(jnp.ones((ctx_len, dq), F32), ((0, 0), (0, HEAD_PAD - dq)))
    sin_c = jnp.zeros((ctx_len, HEAD_PAD), F32)
    return dict(q_rank=qr, kv_rank=kvr, w_in=w_in, q_norm=q_norm_w.astype(F32)[None, :],
                kv_norm=kv_norm_w.astype(F32)[None, :], w_qa=qa.astype(BF16), w_qb=qb_sw.astype(BF16),
                w_k=w_k.astype(BF16), w_v=w_v.astype(BF16), w_out=w_out.astype(BF16),
                v_ones=jnp.tile((jnp.arange(HEAD_PAD) == V_ONES_LANE).astype(F32), MLA_HEADS)[None, :],
                tables=dict(lat=(cos_l, sin_l), ctx=(cos_c, sin_c)))


def _modulation(c, c_ctx, ada_w, ada_b):
    nb, d = c.shape
    rows = jnp.zeros((SUBLANES, d), F32).at[:nb].set(c).at[nb].set(c_ctx)
    n = ada_w.shape[1]
    (mod,) = _fused_matmul([(rows, (SUBLANES, d), lambda i, j: (0, 0))], [ada_w.astype(BF16)],
                           [_const_tile_in(ada_b.astype(F32)[None, :], _pick(n, (1024, 512, 256, 128)))],
                           _silu, lambda accs, b: [accs[0] + b], rows=SUBLANES, tm=SUBLANES, k=d, n=n,
                           tn=_pick(n, (1024, 512, 256, 128)), out_dtypes=[F32])
    names = ("sh1", "sc1", "g1", "sh2", "sc2", "g2")
    parts = jnp.split(mod, 6, axis=1)
    lat = {nm: pt[:nb][:, None, :] for nm, pt in zip(names, parts)}
    ctx = {nm: pt[nb:nb + 1][:, None, :] for nm, pt in zip(names, parts)}
    return lat, ctx


def kernel(x, c, ctx, c_ctx, ada_w, ada_b, norm1_w, norm2_w, ffn_w13, ffn_w2, m2_in_w, m2_conv_w, m2_conv_b, m2_dt_bias, m2_a_log, m2_d, m2_norm_w, m2_out_w, lru_in_w, lru_conv_w, lru_conv_b, lru_gate_w, lru_gate_b, lru_a_param, lru_out_w, s5_lambda_re, s5_lambda_im, s5_log_step, s5_b_re, s5_b_im, s5_c_re, s5_c_im, s5_d, s5_glu_w, s5_glu_b, mla_in_w, mla_q_norm_w, mla_kv_norm_w, mla_qb_w, mla_kvb_w, mla_out_w, final_norm_w):
    nb, seq, d = x.shape
    ctx_len = ctx.shape[1]
    depth = ada_w.shape[0]
    hidden = ffn_w2.shape[1]
    xl = x.reshape(nb * seq, d).astype(F32)
    xc = ctx.reshape(nb * ctx_len, d).astype(F32)
    for i in range(depth):
        kind, j = i % N_MIXERS, i // N_MIXERS
        ctx_out = i < depth - 1
        mods_l, mods_c = _modulation(c.astype(F32), c_ctx.astype(F32), ada_w[i], ada_b[i])
        lat_s = _Stream(xl, nb, seq, mods_l)
        ctx_s = _Stream(xc, nb, ctx_len, mods_c)
        n1 = norm1_w[i].astype(F32)[None, :]
        if kind == 0:
            p = _mamba_params(m2_in_w[j], m2_conv_w[j], m2_conv_b[j], m2_dt_bias[j], m2_a_log[j], m2_d[j],
                              m2_norm_w[j], m2_out_w[j])
            p["norm1"] = n1
            outs = _mamba_layer(lat_s, ctx_s, p, ctx_out)
        elif kind == 1:
            p = _rglru_params(lru_in_w[j], lru_conv_w[j], lru_conv_b[j], lru_gate_w[j], lru_gate_b[j],
                              lru_a_param[j], lru_out_w[j])
            p["norm1"] = n1
            outs = _rglru_layer(lat_s, ctx_s, p, ctx_out)
        elif kind == 2:
            p = _s5_params(s5_lambda_re[j], s5_lambda_im[j], s5_log_step[j], s5_b_re[j], s5_b_im[j],
                           s5_c_re[j], s5_c_im[j], s5_d[j], s5_glu_w[j], s5_glu_b[j])
            p["norm1"] = n1
            outs = _s5_layer(lat_s, ctx_s, p, ctx_out)
        else:
            p = _mla_params(mla_in_w[j], mla_q_norm_w[j], mla_kv_norm_w[j], mla_qb_w[j], mla_kvb_w[j],
                            mla_out_w[j], seq, ctx_len)
            p["norm1"] = n1
            outs = _mla_layer(lat_s, ctx_s, p, ctx_out)
        n2 = norm2_w[i].astype(F32)[None, :]
        w1 = ffn_w13[i][:, :hidden].astype(BF16)
        w3 = ffn_w13[i][:, hidden:].astype(BF16)
        w2 = ffn_w2[i].astype(BF16)
        lat_s = _Stream(outs["lat"], nb, seq, mods_l)
        xl = _ffn(lat_s, n2, mods_l["sc2"], mods_l["sh2"], mods_l["g2"], w1, w3, w2)
        if ctx_out:
            ctx_s = _Stream(outs["ctx"], nb, ctx_len, mods_c)
            xc = _ffn(ctx_s, n2, mods_c["sc2"], mods_c["sh2"], mods_c["g2"], w1, w3, w2)
    zero = jnp.zeros((1, 1, d), F32)
    lat_s = _Stream(xl, nb, seq, None)
    out = _norm_mod_rows(xl, final_norm_w.astype(F32)[None, :], zero, zero, tm=lat_s.tm, tpb=lat_s.tpb)
    return out.reshape(nb, seq, d).astype(x.dtype)
```

```python
import functools
import math

import jax
import jax.numpy as jnp
from jax import lax
from jax.experimental import pallas as pl
from jax.experimental.pallas import tpu as pltpu

F32 = jnp.float32
BF16 = jnp.bfloat16

LANES = 128
SUBLANES = 8
VMEM_BUDGET_BYTES = 56 * 2**20

GRID_W = 64
N_MIXERS = 4
NORM_EPS = 1e-6
CONV_WIDTH = 4
CONV_PAD_LEFT = 2
M2_HEADDIM = 64
M2_GROUPS = 4
M2_STATE = 128
SSD_CHUNK = 128
LRU_BLOCK = 128
LRU_C = 8.0
S5_GROUP = 16
S5_STATE = 64
S5_TILE = 8
MLA_HEADS = 16
MLA_NOPE = 64
MLA_ROPE = 32
MLA_V = 64
ROPE_FREQ = MLA_ROPE // 4
ROPE_BASE = 10000.0
HEAD_PAD = 128
MAX_FOLD_ROWS = 64
V_ONES_LANE = MLA_V


def _pick(n, cands):
    for c in cands:
        if n % c == 0:
            return c
    raise ValueError(f"no tile in {cands} divides {n}")


def _params(sem, est_bytes):
    limit = int(min(max(2 * est_bytes, 32 * 2**20), VMEM_BUDGET_BYTES))
    return pltpu.CompilerParams(dimension_semantics=sem, vmem_limit_bytes=limit)


def _nbytes(shape, dtype):
    return math.prod(s for s in shape if s is not None) * jnp.dtype(dtype).itemsize


def _sigmoid(x):
    return 0.5 * jnp.tanh(0.5 * x) + 0.5


def _silu(x):
    return x * _sigmoid(x)


def _softplus(x):
    return jnp.maximum(x, 0.0) + jnp.log1p(jnp.exp(-jnp.abs(x)))


def _gelu_tanh(x):
    return 0.5 * x * (1.0 + jnp.tanh(math.sqrt(2.0 / math.pi) * (x + 0.044715 * (x * x * x))))


def _rms(x, w):
    return x * lax.rsqrt(jnp.mean(x * x, axis=-1, keepdims=True) + NORM_EPS) * w


def _norm_mod(x, w, sc, sh):
    return _rms(x, w) * (1.0 + sc) + sh


def _split3(q):
    q1 = q.astype(BF16)
    r1 = q - q1.astype(F32)
    q2 = r1.astype(BF16)
    q3 = (r1 - q2.astype(F32)).astype(BF16)
    return q1, q2, q3


def _select_cols(q, e, parts=3):
    return sum(jnp.dot(p, e, preferred_element_type=F32) for p in _split3(q)[:parts])


def _select_rows(t, q):
    return sum(jnp.dot(t, p, preferred_element_type=F32) for p in _split3(q))


def _fused_matmul(pro_ins, weights, epi_ins, prologue, epilogue, *, rows, tm, k, n, tn, out_dtypes):
    n_p, n_w, n_e, n_o = len(pro_ins), len(weights), len(epi_ins), len(out_dtypes)

    def body(*refs):
        p = refs[:n_p]
        w = refs[n_p:n_p + n_w]
        e = refs[n_p + n_w:n_p + n_w + n_e]
        o = refs[n_p + n_w + n_e:n_p + n_w + n_e + n_o]
        a_scr = refs[-1]

        @pl.when(pl.program_id(1) == 0)
        def _():
            a_scr[...] = prologue(*[r[...] for r in p]).astype(BF16)

        a = a_scr[...]
        accs = [jnp.dot(a, wr[...], preferred_element_type=F32) for wr in w]
        outs = epilogue(accs, *[r[...] for r in e])
        for o_ref, val in zip(o, outs):
            o_ref[...] = val.astype(o_ref.dtype)

    in_specs = ([pl.BlockSpec(bs, im) for (_, bs, im) in pro_ins]
                + [pl.BlockSpec((k, tn), lambda i, j: (0, j)) for _ in weights]
                + [pl.BlockSpec(bs, im) for (_, bs, im) in epi_ins])
    out_specs = [pl.BlockSpec((tm, tn), lambda i, j: (i, j)) for _ in out_dtypes]
    est = (sum(_nbytes(bs, a.dtype) for (a, bs, _) in pro_ins + epi_ins) * 2
           + n_w * k * tn * 2 * 2 + sum(tm * tn * jnp.dtype(d).itemsize for d in out_dtypes) * 2
           + tm * k * 2 + (n_w + 2) * tm * tn * 4 + tm * k * 8)
    outs = pl.pallas_call(
        body,
        out_shape=[jax.ShapeDtypeStruct((rows, n), d) for d in out_dtypes],
        grid=(rows // tm, n // tn),
        in_specs=in_specs,
        out_specs=out_specs,
        scratch_shapes=[pltpu.VMEM((tm, k), BF16)],
        compiler_params=_params(("parallel", "arbitrary"), est),
        name=f"mm_r{rows}_k{k}_n{n}x{n_w}",
    )(*[a for (a, _, _) in pro_ins], *weights, *[a for (a, _, _) in epi_ins])
    return outs


def _rows_in(a, tm, width, col_block=0):
    return (a, (tm, width), lambda i, j: (i, col_block))


def _batch_in(a, tpb, width):
    if a.shape[0] == 1:
        return (a, (None, 1, width), lambda i, j: (0, 0, 0))
    return (a, (None, 1, width), lambda i, j: (i // tpb, 0, 0))


def _const_in(a, width):
    return (a, (1, width), lambda i, j: (0, 0))


def _tile_in(a, tm, tn):
    return (a, (tm, tn), lambda i, j: (i, j))


def _batch_tile_in(a, tpb, tn):
    if a.shape[0] == 1:
        return (a, (None, 1, tn), lambda i, j: (0, 0, j))
    return (a, (None, 1, tn), lambda i, j: (i // tpb, 0, j))


def _const_tile_in(a, tn):
    return (a, (1, tn), lambda i, j: (0, j))


def _norm_mod_rows(x, w, sc, sh, *, tm, tpb):
    rows, d = x.shape

    def body(x_ref, w_ref, sc_ref, sh_ref, o_ref):
        o_ref[...] = _norm_mod(x_ref[...], w_ref[...], sc_ref[...], sh_ref[...])

    def bidx(a):
        if a.shape[0] == 1:
            return lambda i: (0, 0, 0)
        return lambda i: (i // tpb, 0, 0)

    return pl.pallas_call(
        body,
        out_shape=jax.ShapeDtypeStruct((rows, d), F32),
        grid=(rows // tm,),
        in_specs=[pl.BlockSpec((tm, d), lambda i: (i, 0)),
                  pl.BlockSpec((1, d), lambda i: (0, 0)),
                  pl.BlockSpec((None, 1, d), bidx(sc)),
                  pl.BlockSpec((None, 1, d), bidx(sh))],
        out_specs=pl.BlockSpec((tm, d), lambda i: (i, 0)),
        compiler_params=_params(("parallel",), 6 * tm * d * 4),
        name="norm_mod",
    )(x, w, sc, sh)


def _dwconv(x, w, b, *, nb, seq, col_off, channels, act, out_dtype=F32):
    cb = channels // LANES
    rc = _pick(seq, (512, 256, 128))
    pad = SUBLANES

    def body(x_ref, w_ref, b_ref, o_ref, p_scr):
        p_scr[0:pad, :] = jnp.zeros((pad, LANES), F32)
        p_scr[pad + seq:pad + seq + pad, :] = jnp.zeros((pad, LANES), F32)
        p_scr[pad:pad + seq, :] = x_ref[...].astype(F32)
        wv = w_ref[...]
        bv = b_ref[...]

        def step(i, carry):
            r0 = pl.multiple_of(i * rc, rc)
            y = bv
            for t in range(CONV_WIDTH):
                y = y + wv[t:t + 1, :] * p_scr[pl.ds(r0 + pad - CONV_PAD_LEFT + t, rc), :]
            if act:
                y = _silu(y)
            o_ref[pl.ds(r0, rc), :] = y.astype(o_ref.dtype)
            return carry

        lax.fori_loop(0, seq // rc, step, 0)

    return pl.pallas_call(
        body,
        out_shape=jax.ShapeDtypeStruct((nb * seq, channels), out_dtype),
        grid=(nb, cb),
        in_specs=[pl.BlockSpec((seq, LANES), lambda bi, ci: (bi, col_off + ci)),
                  pl.BlockSpec((CONV_WIDTH, LANES), lambda bi, ci: (0, ci)),
                  pl.BlockSpec((1, LANES), lambda bi, ci: (0, ci))],
        out_specs=pl.BlockSpec((seq, LANES), lambda bi, ci: (bi, ci)),
        scratch_shapes=[pltpu.VMEM((seq + 2 * pad, LANES), F32)],
        compiler_params=_params(("parallel", "parallel"), 5 * seq * LANES * 4),
        name="dwconv",
    )(x, w, b)


def _ssd_direction(xbc, dt, par, dskip, h0, yprev, consts, *, nb, seq, reverse):
    tri, e_head = consts
    q = SSD_CHUNK
    nc = seq // q
    inner = xbc.shape[1] - 2 * M2_GROUPS * M2_STATE
    gw = inner // M2_GROUPS
    heads = inner // M2_HEADDIM
    hpg = heads // M2_GROUPS
    has_prev = yprev is not None
    has_skip = dskip is not None

    def body(*refs):
        it = iter(refs)
        xbc_ref, dt_ref, par_ref = next(it), next(it), next(it)
        dsk_ref = next(it) if has_skip else None
        h0_ref = next(it)
        yp_ref = next(it) if has_prev else None
        tri_ref, eh_ref = next(it), next(it)
        y_ref, hout_ref, h_scr = next(it), next(it), next(it)
        c = pl.program_id(1)

        @pl.when(c == 0)
        def _():
            h_scr[...] = h0_ref[...]

        dtv = _softplus(dt_ref[...] + par_ref[0:1, :])
        da = dtv * (-jnp.exp(par_ref[1:2, :]))
        acs = _select_rows(tri_ref[...], da)
        acs_t = acs.T
        dt_t = dtv.T
        last = acs[0:1, :] if reverse else acs[q - 1:q, :]
        eh = eh_ref[...]
        w_e = _select_cols(dtv * jnp.exp(last - acs), eh, parts=2)
        cdec_e = _select_cols(jnp.broadcast_to(jnp.exp(last), (SUBLANES, LANES)), eh)[0:1, :]
        x_b = xbc_ref[:, 0:inner].astype(BF16)
        xs = xbc_ref[:, 0:inner].astype(F32)
        x_d = (xs * w_e).astype(BF16)
        row = lax.broadcasted_iota(jnp.int32, (q, q), 0)
        col = lax.broadcasted_iota(jnp.int32, (q, q), 1)
        mask = (col >= row) if reverse else (col <= row)
        lane = lax.broadcasted_iota(jnp.int32, (q, LANES), 1)
        for g in range(M2_GROUPS):
            bm = xbc_ref[:, inner + g * M2_STATE:inner + (g + 1) * M2_STATE]
            cm = xbc_ref[:, inner + (M2_GROUPS + g) * M2_STATE:inner + (M2_GROUPS + g + 1) * M2_STATE]
            scores = lax.dot_general(cm.astype(BF16), bm.astype(BF16), (((1,), (1,)), ((), ())),
                                     preferred_element_type=F32)
            h_t = h_scr[g]
            h_b = h_t.astype(BF16)
            s_t = jnp.dot(bm.astype(F32).T.astype(BF16), x_d[:, g * gw:(g + 1) * gw], preferred_element_type=F32)
            h_scr[g] = h_t * cdec_e[:, g * gw:(g + 1) * gw] + s_t
            for pr in range(hpg // 2):
                k0 = g * hpg + 2 * pr
                c0 = k0 * M2_HEADDIM
                rhs = jnp.concatenate([x_b[:, c0:c0 + LANES], h_b[:, 2 * pr * M2_HEADDIM:2 * pr * M2_HEADDIM + LANES]],
                                      axis=0)
                ys = []
                for k in (k0, k0 + 1):
                    a_col = jnp.broadcast_to(acs_t[k:k + 1, :], (q, q)).T
                    decay = jnp.exp(jnp.where(mask, a_col - acs_t[k:k + 1, :], -jnp.inf))
                    m = (scores * decay * dt_t[k:k + 1, :]).astype(BF16)
                    c_e = (cm.astype(F32) * jnp.exp(a_col)).astype(BF16)
                    ys.append(jnp.dot(jnp.concatenate([m, c_e], axis=1), rhs, preferred_element_type=F32))
                y = jnp.where(lane < M2_HEADDIM, ys[0], ys[1])
                if has_skip:
                    y = y + dsk_ref[:, c0:c0 + LANES] * xs[:, c0:c0 + LANES]
                if has_prev:
                    y = y + yp_ref[:, c0:c0 + LANES]
                y_ref[:, c0:c0 + LANES] = y

        @pl.when(c == nc - 1)
        def _():
            hout_ref[...] = h_scr[...]

    def rb(bi, ci):
        return bi * nc + ((nc - 1 - ci) if reverse else ci)

    width = xbc.shape[1]
    st_shape = (M2_GROUPS, M2_STATE, gw)
    ins = [xbc, dt, par]
    in_specs = [pl.BlockSpec((q, width), lambda bi, ci: (rb(bi, ci), 0)),
                pl.BlockSpec((q, LANES), lambda bi, ci: (rb(bi, ci), 0)),
                pl.BlockSpec((SUBLANES, LANES), lambda bi, ci: (0, 0))]
    if has_skip:
        ins.append(dskip)
        in_specs.append(pl.BlockSpec((1, inner), lambda bi, ci: (0, 0)))
    ins.append(h0)
    in_specs.append(pl.BlockSpec((None,) + st_shape, lambda bi, ci: (bi, 0, 0, 0)))
    if has_prev:
        ins.append(yprev)
        in_specs.append(pl.BlockSpec((q, inner), lambda bi, ci: (rb(bi, ci), 0)))
    ins += [tri, e_head]
    in_specs += [pl.BlockSpec(tri.shape, lambda bi, ci: (0, 0)),
                 pl.BlockSpec(e_head.shape, lambda bi, ci: (0, 0))]
    est = (q * width * 4 * 2 + q * inner * 4 * 4 + 3 * math.prod(st_shape) * 4 * 2
           + e_head.size * 2 * 2 + 12 * q * inner * 4 + q * heads * LANES * 4)
    y, h_last = pl.pallas_call(
        body,
        out_shape=[jax.ShapeDtypeStruct((nb * seq, inner), F32),
                   jax.ShapeDtypeStruct((nb,) + st_shape, F32)],
        grid=(nb, nc),
        in_specs=in_specs,
        out_specs=[pl.BlockSpec((q, inner), lambda bi, ci: (rb(bi, ci), 0)),
                   pl.BlockSpec((None,) + st_shape, lambda bi, ci: (bi, 0, 0, 0))],
        scratch_shapes=[pltpu.VMEM(st_shape, F32)],
        compiler_params=_params(("parallel", "arbitrary"), est),
        name="ssd_bwd" if reverse else "ssd_fwd",
    )(*ins)
    return y, h_last


def _ssd_consts(inner, reverse):
    heads = inner // M2_HEADDIM
    r = jnp.arange(SSD_CHUNK)
    tri = (r[None, :] >= r[:, None]) if reverse else (r[None, :] <= r[:, None])
    hk = jnp.arange(LANES)[:, None]
    e_head = (hk == (jnp.arange(inner)[None, :] // M2_HEADDIM))
    return tri.astype(BF16), e_head.astype(BF16)


def _tile_scan(a, u, reverse):
    row = lax.broadcasted_iota(jnp.int32, a.shape, 0)
    for d in (1, 2, 4):
        shift = (SUBLANES - d) if reverse else d
        valid = (row < SUBLANES - d) if reverse else (row >= d)
        a_s = jnp.where(valid, pltpu.roll(a, shift, 0), 1.0)
        u_s = jnp.where(valid, pltpu.roll(u, shift, 0), 0.0)
        u = a * u_s + u
        a = a * a_s
    return a, u


def _rglru_direction(xc, gate_w, gate_b, a_param, h0, rprev, *, nb, seq, reverse):
    width = xc.shape[1]
    nblk = width // LRU_BLOCK
    tb = _pick(seq, (512, 256, 128))
    nt = seq // tb
    has_prev = rprev is not None

    def body(*refs):
        it = iter(refs)
        x_ref, gw_ref, gb_ref, ap_ref, h0_ref = next(it), next(it), next(it), next(it), next(it)
        rp_ref = next(it) if has_prev else None
        r_ref, hout_ref, a_scr, u_scr, h_scr = next(it), next(it), next(it), next(it), next(it)
        tstep = pl.program_id(1)

        @pl.when(tstep == 0)
        def _():
            h_scr[...] = jnp.broadcast_to(h0_ref[...], (SUBLANES, width))

        log_base = -LRU_C * _softplus(-ap_ref[...])
        for nbk in range(nblk):
            cs = slice(nbk * LRU_BLOCK, (nbk + 1) * LRU_BLOCK)
            xb = x_ref[:, cs]
            g = jnp.dot(xb.astype(BF16), gw_ref[nbk], preferred_element_type=F32) + gb_ref[nbk]
            g = _sigmoid(g)
            log_a = g[:, :LRU_BLOCK] * log_base[:, cs]
            a = jnp.exp(log_a)
            gap = jnp.maximum(1.0 - a * a, 0.0)
            mult = jnp.where(gap > 0.0, gap * lax.rsqrt(gap), 0.0)
            a_scr[:, cs] = a
            u_scr[:, cs] = xb * g[:, LRU_BLOCK:] * mult

        def step(i, h):
            grp = (tb // SUBLANES - 1 - i) if reverse else i
            r0 = pl.multiple_of(grp * SUBLANES, SUBLANES)
            a_c, h_loc = _tile_scan(a_scr[pl.ds(r0, SUBLANES), :], u_scr[pl.ds(r0, SUBLANES), :], reverse)
            h_new = a_c * h + h_loc
            out = h_new
            if has_prev:
                out = out + rp_ref[pl.ds(r0, SUBLANES), :]
            r_ref[pl.ds(r0, SUBLANES), :] = out
            edge = h_new[0:1, :] if reverse else h_new[SUBLANES - 1:SUBLANES, :]
            return jnp.broadcast_to(edge, (SUBLANES, width))

        h_fin = lax.fori_loop(0, tb // SUBLANES, step, h_scr[...])
        h_scr[...] = h_fin

        @pl.when(tstep == nt - 1)
        def _():
            hout_ref[...] = h_fin[0:1, :]

    def rb(bi, ti):
        return bi * nt + ((nt - 1 - ti) if reverse else ti)

    ins = [xc, gate_w, gate_b, a_param, h0]
    in_specs = [pl.BlockSpec((tb, width), lambda bi, ti: (rb(bi, ti), 0)),
                pl.BlockSpec(gate_w.shape, lambda bi, ti: (0, 0, 0)),
                pl.BlockSpec(gate_b.shape, lambda bi, ti: (0, 0, 0)),
                pl.BlockSpec((1, width), lambda bi, ti: (0, 0)),
                pl.BlockSpec((None, 1, width), lambda bi, ti: (bi, 0, 0))]
    if has_prev:
        ins.append(rprev)
        in_specs.append(pl.BlockSpec((tb, width), lambda bi, ti: (rb(bi, ti), 0)))
    est = 10 * tb * width * 4 + gate_w.size * 2 * 2
    r, h_last = pl.pallas_call(
        body,
        out_shape=[jax.ShapeDtypeStruct((nb * seq, width), F32),
                   jax.ShapeDtypeStruct((nb, 1, width), F32)],
        grid=(nb, nt),
        in_specs=in_specs,
        out_specs=[pl.BlockSpec((tb, width), lambda bi, ti: (rb(bi, ti), 0)),
                   pl.BlockSpec((None, 1, width), lambda bi, ti: (bi, 0, 0))],
        scratch_shapes=[pltpu.VMEM((tb, width), F32), pltpu.VMEM((tb, width), F32),
                        pltpu.VMEM((SUBLANES, width), F32)],
        compiler_params=_params(("parallel", "arbitrary"), est),
        name="rglru_bwd" if reverse else "rglru_fwd",
    )(*ins)
    return r, h_last


def _s5_bidir(u, w, pw_f, pw_b, h_in, *, nb, seq):
    d_model = u.shape[1]
    lbs = d_model // LANES
    sw = pw_f.shape[3]
    m_t = seq // S5_TILE
    wide = S5_TILE * LANES

    def scan(s_scr, pw_ref, c_re, c_im, reverse):
        row = lax.broadcasted_iota(jnp.int32, (SUBLANES, sw), 0)

        def step(i, carry):
            hr, hi = carry
            grp = (m_t // SUBLANES - 1 - i) if reverse else i
            r0 = pl.multiple_of(grp * SUBLANES, SUBLANES)
            xr = s_scr[pl.ds(r0, SUBLANES), 0:sw]
            xi = s_scr[pl.ds(r0, SUBLANES), sw:2 * sw]
            for n_step, dist in enumerate((1, 2, 4)):
                shift = (SUBLANES - dist) if reverse else dist
                pr = pw_ref[2 * n_step]
                pi = pw_ref[2 * n_step + 1]
                sr = pltpu.roll(xr, shift, 0)
                si = pltpu.roll(xi, shift, 0)
                xr, xi = xr + (pr * sr - pi * si), xi + (pr * si + pi * sr)
            nr = xr + (pw_ref[6] * hr - pw_ref[7] * hi)
            ni = xi + (pw_ref[6] * hi + pw_ref[7] * hr)
            edge = (SUBLANES - 1) if reverse else 0
            back = 1 if not reverse else SUBLANES - 1
            s_scr[pl.ds(r0, SUBLANES), 0:sw] = jnp.where(row == edge, hr, pltpu.roll(nr, back, 0))
            s_scr[pl.ds(r0, SUBLANES), sw:2 * sw] = jnp.where(row == edge, hi, pltpu.roll(ni, back, 0))
            e = 0 if reverse else SUBLANES - 1
            return (jnp.broadcast_to(nr[e:e + 1, :], (SUBLANES, sw)),
                    jnp.broadcast_to(ni[e:e + 1, :], (SUBLANES, sw)))

        init = (jnp.broadcast_to(c_re, (SUBLANES, sw)), jnp.broadcast_to(c_im, (SUBLANES, sw)))
        return lax.fori_loop(0, m_t // SUBLANES, step, init)

    def body(u_ref, t_ref, pf_ref, pb_ref, of_ref, ob_ref, pwf_ref, pwb_ref, hin_ref,
             y_ref, hout_ref, x_scr, f_scr, g_scr):
        for l in range(S5_TILE):
            x_scr[:, l * LANES:(l + 1) * LANES] = u_ref[pl.ds(l, m_t, stride=S5_TILE), :].astype(BF16)
        x = x_scr[...]
        f_scr[...] = jnp.dot(x, pf_ref[...], preferred_element_type=F32)
        g_scr[...] = jnp.dot(x, pb_ref[...], preferred_element_type=F32)
        fr, fi = scan(f_scr, pwf_ref, hin_ref[0:1, :], hin_ref[1:2, :], False)
        gr, gi = scan(g_scr, pwb_ref, hin_ref[2:3, :], hin_ref[3:4, :], True)
        hout_ref[0:1, :] = fr[0:1, :]
        hout_ref[1:2, :] = fi[0:1, :]
        hout_ref[2:3, :] = gr[0:1, :]
        hout_ref[3:4, :] = gi[0:1, :]
        y = (jnp.dot(x, t_ref[...], preferred_element_type=F32)
             + jnp.dot(f_scr[...].astype(BF16), of_ref[...], preferred_element_type=F32)
             + jnp.dot(g_scr[...].astype(BF16), ob_ref[...], preferred_element_type=F32))
        for l in range(S5_TILE):
            y_ref[pl.ds(l, m_t, stride=S5_TILE), :] = y[:, l * LANES:(l + 1) * LANES]

    def wspec():
        return pl.BlockSpec((None, wide, wide), lambda li, bi: (li, 0, 0), pipeline_mode=pl.Buffered(1))

    est = (4 * seq * LANES * 4 + 5 * wide * wide * 2 + m_t * wide * 2 + 2 * m_t * 2 * sw * 4
           + 3 * m_t * wide * 4)
    y, h_out = pl.pallas_call(
        body,
        out_shape=[jax.ShapeDtypeStruct((nb * seq, d_model), F32),
                   jax.ShapeDtypeStruct((nb, lbs, 4, sw), F32)],
        grid=(lbs, nb),
        in_specs=[pl.BlockSpec((seq, LANES), lambda li, bi: (bi, li)),
                  wspec(), wspec(), wspec(), wspec(), wspec(),
                  pl.BlockSpec((None, 8, SUBLANES, sw), lambda li, bi: (li, 0, 0, 0)),
                  pl.BlockSpec((None, 8, SUBLANES, sw), lambda li, bi: (li, 0, 0, 0)),
                  pl.BlockSpec((None, None, 4, sw), lambda li, bi: (bi, li, 0, 0))],
        out_specs=[pl.BlockSpec((seq, LANES), lambda li, bi: (bi, li)),
                   pl.BlockSpec((None, None, 4, sw), lambda li, bi: (bi, li, 0, 0))],
        scratch_shapes=[pltpu.VMEM((m_t, wide), BF16), pltpu.VMEM((m_t, 2 * sw), F32),
                        pltpu.VMEM((m_t, 2 * sw), F32)],
        compiler_params=_params(("parallel", "parallel"), est),
        name="s5_bidir",
    )(u, w["t"], w["pf"], w["pb"], w["of"], w["ob"], pw_f, pw_b, h_in)
    return y, h_out


def _attention(q, kv_parts, *, nb, lq, scale):
    heads = q.shape[1] // HEAD_PAD
    tq = _pick(lq, (1024, 512, 256, 128))
    tsub = _pick(tq, (256, 128))
    nq = lq // tq
    c = scale * math.log2(math.e)
    chunks = [_pick(length, (2048, 1024, 512, 256, 128)) for (_, _, length) in kv_parts]
    n_parts = len(kv_parts)

    def body(*refs):
        q_ref = refs[0]
        kv_refs = refs[1:1 + 2 * n_parts]
        o_ref = refs[1 + 2 * n_parts]
        s_a, s_b = refs[2 + 2 * n_parts], refs[3 + 2 * n_parts]

        def q_tile(t, carry):
            r0 = pl.multiple_of(t * tsub, tsub)
            qt = q_ref[pl.ds(r0, tsub), :]

            def key_parts(n):
                parts = 4 if n >= 2048 else 2 if n >= 2 * MAX_FOLD_ROWS else 1
                return tuple((i * n // parts, (i + 1) * n // parts) for i in range(parts))

            def scores(k):
                return jnp.concatenate(
                    [lax.dot_general(k[a:b], qt, (((1,), (1,)), ((), ())), preferred_element_type=F32)
                     for a, b in key_parts(k.shape[0])], axis=0)

            def absorb(s_t, v_t, m, acc):
                part = jnp.max(s_t.reshape(-1, MAX_FOLD_ROWS, tsub), axis=0)
                m_new = jnp.maximum(m, jnp.max(part, axis=0, keepdims=True))
                p_t = jnp.exp2((s_t - m_new) * c).astype(BF16)
                alpha = jnp.exp2((m - m_new) * c)
                acc = alpha * acc
                for a, b in key_parts(s_t.shape[0]):
                    acc = acc + jnp.dot(v_t[:, a:b], p_t[a:b], preferred_element_type=F32)
                return m_new, acc

            m = jnp.full((1, tsub), -jnp.inf, F32)
            acc = jnp.zeros((HEAD_PAD, tsub), F32)
            sched = [(kv_refs[2 * pi], kv_refs[2 * pi + 1], ci, chunks[pi])
                     for pi, (_, _, length) in enumerate(kv_parts) for ci in range(length // chunks[pi])]
            bufs = (s_a, s_b)

            def issue(i):
                k_ref, _, ci, tk = sched[i]
                bufs[i % 2][0:tk, :] = scores(k_ref[ci * tk:(ci + 1) * tk, :])

            issue(0)
            for i, (_, v_ref, ci, tk) in enumerate(sched):
                if i + 1 < len(sched):
                    issue(i + 1)
                m, acc = absorb(bufs[i % 2][0:tk, :], v_ref[ci], m, acc)
            o_t = acc / acc[V_ONES_LANE:V_ONES_LANE + 1, :]
            o_ref[pl.ds(r0, tsub), :] = o_t.T.astype(o_ref.dtype)
            return carry

        lax.fori_loop(0, tq // tsub, q_tile, 0)

    ins = [q]
    in_specs = [pl.BlockSpec((tq, HEAD_PAD), lambda b, h, qi: (b * nq + qi, h))]
    est = tq * HEAD_PAD * 2 * 4 + 6 * tsub * max(chunks) * 4
    for (k, v, length), tk in zip(kv_parts, chunks):
        n_ch = length // tk
        v_t = v.reshape(nb, n_ch, tk, heads, HEAD_PAD).transpose(0, 3, 1, 4, 2)
        ins += [k, v_t]
        in_specs += [pl.BlockSpec((length, HEAD_PAD), lambda b, h, qi: (b, h)),
                     pl.BlockSpec((None, None, n_ch, HEAD_PAD, tk), lambda b, h, qi: (b, h, 0, 0, 0))]
        est += 2 * length * HEAD_PAD * 2 * 2
    return pl.pallas_call(
        body,
        out_shape=jax.ShapeDtypeStruct(q.shape, BF16),
        grid=(nb, heads, nq),
        in_specs=in_specs,
        out_specs=pl.BlockSpec((tq, HEAD_PAD), lambda b, h, qi: (b * nq + qi, h)),
        scratch_shapes=[pltpu.VMEM((max(chunks), tsub), F32), pltpu.VMEM((max(chunks), tsub), F32)],
        compiler_params=_params(("parallel", "parallel", "arbitrary"), est),
        name="attention",
    )(*ins)


class _Stream:
    def __init__(self, x, nb, seq, mods):
        self.x = x
        self.nb = nb
        self.seq = seq
        self.tm = _pick(seq, (512, 256, 128))
        self.tpb = seq // self.tm
        self.mods = mods
        self.rows = nb * seq


def _in_proj(st, norm_w, sc, sh, weights, n, tn, out_dtypes=None):
    d = st.x.shape[1]
    pro = [_rows_in(st.x, st.tm, d), _const_in(norm_w, d), _batch_in(sc, st.tpb, d), _batch_in(sh, st.tpb, d)]
    return _fused_matmul(pro, weights, [], _norm_mod, lambda accs: accs, rows=st.rows, tm=st.tm, k=d, n=n,
                         tn=tn, out_dtypes=out_dtypes or [F32] * len(weights))


def _out_proj(st, pro_ins, prologue, weight, gate, k):
    d = st.x.shape[1]
    epi = [_tile_in(st.x, st.tm, d), _batch_tile_in(gate, st.tpb, d)]
    (out,) = _fused_matmul(pro_ins, [weight], epi, prologue, lambda accs, x, g: [x + g * accs[0]],
                           rows=st.rows, tm=st.tm, k=k, n=d, tn=d, out_dtypes=[F32])
    return out


def _ffn(st, norm_w, sc, sh, gate, w1, w3, w2):
    d = st.x.shape[1]
    hidden = w1.shape[1]
    th = _pick(hidden, (1408, 1024, 512, 256, 128))
    tm, tpb = st.tm, st.tpb
    nh = hidden // th

    def body(x_ref, nw_ref, sc_ref, sh_ref, w1_ref, w3_ref, w2_ref, g_ref, o_ref, a_scr, acc_scr):
        j = pl.program_id(1)

        @pl.when(j == 0)
        def _():
            a_scr[...] = _norm_mod(x_ref[...], nw_ref[...], sc_ref[...], sh_ref[...]).astype(BF16)
            acc_scr[...] = jnp.zeros((tm, d), F32)

        a = a_scr[...]
        up = (_silu(jnp.dot(a, w1_ref[...], preferred_element_type=F32))
              * jnp.dot(a, w3_ref[...], preferred_element_type=F32))
        acc_scr[...] += jnp.dot(up.astype(BF16), w2_ref[...], preferred_element_type=F32)

        @pl.when(j == nh - 1)
        def _():
            o_ref[...] = x_ref[...] + g_ref[...] * acc_scr[...]

    def bidx(a):
        if a.shape[0] == 1:
            return lambda i, j: (0, 0, 0)
        return lambda i, j: (i // tpb, 0, 0)

    est = 4 * tm * d * 4 + 3 * d * th * 2 * 2 + tm * d * 6 + 4 * tm * th * 4
    return pl.pallas_call(
        body,
        out_shape=jax.ShapeDtypeStruct((st.rows, d), F32),
        grid=(st.rows // tm, nh),
        in_specs=[pl.BlockSpec((tm, d), lambda i, j: (i, 0)),
                  pl.BlockSpec((1, d), lambda i, j: (0, 0)),
                  pl.BlockSpec((None, 1, d), bidx(sc)),
                  pl.BlockSpec((None, 1, d), bidx(sh)),
                  pl.BlockSpec((d, th), lambda i, j: (0, j)),
                  pl.BlockSpec((d, th), lambda i, j: (0, j)),
                  pl.BlockSpec((th, d), lambda i, j: (j, 0)),
                  pl.BlockSpec((None, 1, d), bidx(gate))],
        out_specs=pl.BlockSpec((tm, d), lambda i, j: (i, 0)),
        scratch_shapes=[pltpu.VMEM((tm, d), BF16), pltpu.VMEM((tm, d), F32)],
        compiler_params=_params(("parallel", "arbitrary"), est),
        name="ffn",
    )(st.x, norm_w, sc, sh, w1, w3, w2, gate)


def _mamba_layer(lat, ctx, p, ctx_out):
    inner = p["inner"]
    conv_ch = p["conv_ch"]
    heads = inner // M2_HEADDIM
    gw = inner // M2_GROUPS
    outs = {}
    acts = {}
    for name, st in (("ctx", ctx), ("lat", lat)):
        m = st.mods
        (zx,) = _in_proj(st, p["norm1"], m["sc1"], m["sh1"], [p["w_zx"]], inner + conv_ch,
                         _pick(inner + conv_ch, (1024, 512, 256, 128)), out_dtypes=[BF16])
        dt0, dt1 = _in_proj(st, p["norm1"], m["sc1"], m["sh1"], [p["w_dt0"], p["w_dt1"]], LANES, LANES)
        xbc = _dwconv(zx, p["conv_w"], p["conv_b"], nb=st.nb, seq=st.seq, col_off=inner // LANES,
                      channels=conv_ch, act=True, out_dtype=BF16)
        acts[name] = (zx, (dt0, dt1), xbc)
    ys = {}
    for dirn, reverse in ((0, False), (1, True)):
        consts = _ssd_consts(inner, reverse)
        h = jnp.zeros((ctx.nb, M2_GROUPS, M2_STATE, gw), F32)
        for name, st in (("ctx", ctx), ("lat", lat)):
            zx, dts, xbc = acts[name]
            y, h = _ssd_direction(xbc, dts[dirn], p["par"][dirn], p["dskip"] if dirn == 0 else None, h,
                                  ys.get(name), consts, nb=st.nb, seq=st.seq, reverse=reverse)
            ys[name] = y

    def gated_norm(y, z, w):
        return _rms(y * _silu(z.astype(F32)), w)

    for name, st in (("ctx", ctx), ("lat", lat)):
        if name == "ctx" and not ctx_out:
            continue
        zx = acts[name][0]
        pro = [_rows_in(ys[name], st.tm, inner), _rows_in(zx, st.tm, inner), _const_in(p["norm_w"], inner)]
        outs[name] = _out_proj(st, pro, gated_norm, p["w_out"], st.mods["g1"], inner)
    return outs


def _rglru_layer(lat, ctx, p, ctx_out):
    width = p["width"]
    outs = {}
    acts = {}
    for name, st in (("ctx", ctx), ("lat", lat)):
        m = st.mods
        (yx,) = _in_proj(st, p["norm1"], m["sc1"], m["sh1"], [p["w_in"]], 2 * width,
                         _pick(2 * width, (1280, 1024, 512, 256, 128)), out_dtypes=[BF16])
        xc = _dwconv(yx, p["conv_w"], p["conv_b"], nb=st.nb, seq=st.seq, col_off=width // LANES,
                     channels=width, act=False)
        acts[name] = (yx, xc)
    rs = {}
    for dirn, reverse in ((0, False), (1, True)):
        h = jnp.zeros((ctx.nb, 1, width), F32)
        for name, st in (("ctx", ctx), ("lat", lat)):
            r, h = _rglru_direction(acts[name][1], p["gate_w"][dirn], p["gate_b"][dirn], p["a_param"][dirn], h,
                                    rs.get(name), nb=st.nb, seq=st.seq, reverse=reverse)
            rs[name] = r
    for name, st in (("ctx", ctx), ("lat", lat)):
        if name == "ctx" and not ctx_out:
            continue
        pro = [_rows_in(acts[name][0], st.tm, width), _rows_in(rs[name], st.tm, width)]
        outs[name] = _out_proj(st, pro, lambda y, r: _gelu_tanh(y.astype(F32)) * r, p["w_out"], st.mods["g1"], width)
    return outs


def _s5_layer(lat, ctx, p, ctx_out):
    d = lat.x.shape[1]
    lbs = d // LANES
    sw = p["pw"][0].shape[3]
    us = {}
    for name, st in (("ctx", ctx), ("lat", lat)):
        m = st.mods
        us[name] = _norm_mod_rows(st.x, p["norm1"], m["sc1"], m["sh1"], tm=st.tm, tpb=st.tpb)
    ys = {}
    h = jnp.zeros((ctx.nb, lbs, 4, sw), F32)
    for name, st in (("ctx", ctx), ("lat", lat)):
        ys[name], h = _s5_bidir(us[name], p["w"], p["pw"][0], p["pw"][1], h, nb=st.nb, seq=st.seq)
    outs = {}
    for name, st in (("ctx", ctx), ("lat", lat)):
        if name == "ctx" and not ctx_out:
            continue
        pro = [_rows_in(ys[name], st.tm, d), _rows_in(us[name], st.tm, d), _const_in(p["dskip"], d)]
        epi = [_const_tile_in(p["glu_ba"], d), _const_tile_in(p["glu_bg"], d),
               _tile_in(st.x, st.tm, d), _batch_tile_in(st.mods["g1"], st.tpb, d)]
        (out,) = _fused_matmul(
            pro, [p["glu_wa"], p["glu_wg"]], epi,
            lambda y, u, dsk: _gelu_tanh(y + dsk * u),
            lambda accs, ba, bg, x, g: [x + g * ((accs[0] + ba) * _sigmoid(accs[1] + bg))],
            rows=st.rows, tm=st.tm, k=d, n=d, tn=d, out_dtypes=[F32])
        outs[name] = out
    return outs


def _mla_layer(lat, ctx, p, ctx_out):
    d = lat.x.shape[1]
    qr, kvr = p["q_rank"], p["kv_rank"]
    hp = MLA_HEADS * HEAD_PAD
    tn_h = _pick(hp, (1024, 512, 256, 128))
    reps = tn_h // HEAD_PAD
    qkv = {}
    for name, st in (("ctx", ctx), ("lat", lat)):
        m = st.mods
        n_in = p["w_in"].shape[1]
        (lat_all,) = _in_proj(st, p["norm1"], m["sc1"], m["sh1"], [p["w_in"]], n_in, n_in)
        cos_t, sin_t = p["tables"][name]
        tab_spec = lambda a, st=st: (a, (st.tm, HEAD_PAD), lambda i, j: (i % st.tpb, 0))

        def rope_epi(accs, cos, sin, reps=reps):
            cos_r = jnp.concatenate([cos] * reps, axis=1)
            sin_r = jnp.concatenate([sin] * reps, axis=1)
            return [accs[0] * cos_r + accs[1] * sin_r]

        want_q = name == "lat" or ctx_out
        q = None
        if want_q:
            (q,) = _fused_matmul([_rows_in(lat_all, st.tm, qr, 0), _const_in(p["q_norm"], qr)],
                                 [p["w_qa"], p["w_qb"]], [tab_spec(cos_t), tab_spec(sin_t)],
                                 _rms, rope_epi, rows=st.rows, tm=st.tm, k=qr, n=hp, tn=tn_h, out_dtypes=[BF16])

        def kv_epi(accs, kra, krb, cos, sin, ones, reps=reps):
            kr = kra * cos + krb * sin
            return [accs[0] + jnp.concatenate([kr] * reps, axis=1), accs[1] + ones]

        kra_blk = (qr // LANES)
        kv_blk = (qr + LANES) // kvr
        krb_blk = (qr + LANES + kvr) // LANES
        k, v = _fused_matmul(
            [_rows_in(lat_all, st.tm, kvr, kv_blk), _const_in(p["kv_norm"], kvr)],
            [p["w_k"], p["w_v"]],
            [(lat_all, (st.tm, LANES), lambda i, j: (i, kra_blk)),
             (lat_all, (st.tm, LANES), lambda i, j: (i, krb_blk)),
             tab_spec(cos_t), tab_spec(sin_t), _const_tile_in(p["v_ones"], tn_h)],
            _rms, kv_epi, rows=st.rows, tm=st.tm, k=kvr, n=hp, tn=tn_h, out_dtypes=[BF16, BF16])
        qkv[name] = (q, k, v)
    scale = (MLA_NOPE + MLA_ROPE) ** -0.5
    nb = lat.nb
    kv_c = (qkv["ctx"][1], qkv["ctx"][2], ctx.seq)
    kv_l = (qkv["lat"][1], qkv["lat"][2], lat.seq)
    outs = {}
    o_l = _attention(qkv["lat"][0], [kv_c, kv_l], nb=nb, lq=lat.seq, scale=scale)
    outs["lat"] = _out_proj(lat, [_rows_in(o_l, lat.tm, hp)], lambda a: a, p["w_out"], lat.mods["g1"], hp)
    if ctx_out:
        o_c = _attention(qkv["ctx"][0], [kv_c], nb=nb, lq=ctx.seq, scale=scale)
        outs["ctx"] = _out_proj(ctx, [_rows_in(o_c, ctx.tm, hp)], lambda a: a, p["w_out"], ctx.mods["g1"], hp)
    return outs


def _pad_cols(w, n):
    return jnp.pad(w, ((0, 0), (0, n - w.shape[1])))


def _mamba_params(in_w, conv_w, conv_b, dt_bias, a_log, d_skip, norm_w, out_w):
    d, proj = in_w.shape
    heads = dt_bias.shape[1]
    inner = heads * M2_HEADDIM
    conv_ch = inner + 2 * M2_GROUPS * M2_STATE
    par = []
    for dirn in range(2):
        rows = jnp.zeros((SUBLANES, LANES), F32)
        rows = rows.at[0, :heads].set(dt_bias[dirn].astype(F32)).at[1, :heads].set(a_log[dirn].astype(F32))
        par.append(rows)
    off = inner + conv_ch
    return dict(
        inner=inner, conv_ch=conv_ch,
        w_zx=in_w[:, :off].astype(BF16),
        w_dt0=_pad_cols(in_w[:, off:off + heads], LANES).astype(BF16),
        w_dt1=_pad_cols(in_w[:, off + heads:off + 2 * heads], LANES).astype(BF16),
        conv_w=conv_w.astype(F32), conv_b=conv_b.astype(F32)[None, :],
        par=par, dskip=jnp.repeat(d_skip.astype(F32), M2_HEADDIM)[None, :],
        norm_w=norm_w.astype(F32)[None, :], w_out=out_w.astype(BF16))


def _rglru_params(in_w, conv_w, conv_b, gate_w, gate_b, a_param, out_w):
    width = conv_w.shape[1]
    return dict(
        width=width, w_in=in_w.astype(BF16), conv_w=conv_w.astype(F32), conv_b=conv_b.astype(F32)[None, :],
        gate_w=[gate_w[dirn].astype(BF16) for dirn in range(2)],
        gate_b=[gate_b[dirn].astype(F32)[:, None, :] for dirn in range(2)],
        a_param=[a_param[dirn].astype(F32)[None, :] for dirn in range(2)],
        w_out=out_w.astype(BF16))


def _block_diag(m, per):
    g, r, c = m.shape
    m = m.reshape(g // per, per, r, c)
    eye = jnp.eye(per, dtype=m.dtype)
    return jnp.einsum("qarc,ab->qarbc", m, eye).reshape(g // per, per * r, per * c)


def _complex_powers(ar, ai, n):
    pr, pi = [jnp.ones_like(ar)], [jnp.zeros_like(ai)]
    for _ in range(n):
        pr, pi = pr + [pr[-1] * ar - pi[-1] * ai], pi + [pr[-1] * ai + pi[-1] * ar]
    return pr, pi


def _s5_params(lam_re, lam_im, log_step, b_re, b_im, c_re, c_im, d_skip, glu_w, glu_b):
    per = LANES // S5_GROUP
    hp = lax.Precision.HIGHEST
    br, bi = b_re.astype(F32), b_im.astype(F32)
    out = dict(pw=[])
    w = {}
    for dirn in range(2):
        reverse = dirn == 1
        lr = jnp.minimum(lam_re[dirn].astype(F32), -1e-4)
        li = lam_im[dirn].astype(F32)
        step = jnp.exp(log_step[dirn].astype(F32))[:, None]
        mag = jnp.exp(lr * step)
        abr, abi = mag * jnp.cos(li * step), mag * jnp.sin(li * step)
        den = lr * lr + li * li
        zr = ((abr - 1.0) * lr + abi * li) / den
        zi = (abi * lr - (abr - 1.0) * li) / den
        bbr = zr[..., None] * br - zi[..., None] * bi
        bbi = zr[..., None] * bi + zi[..., None] * br
        cr, ci = c_re[dirn].astype(F32), c_im[dirn].astype(F32)
        pr, pi = _complex_powers(abr, abi, S5_TILE)

        pra, pia = jnp.stack(pr), jnp.stack(pi)
        g = abr.shape[0]
        lbs, wide = g // per, S5_TILE * LANES
        wr = pra[..., None] * bbr - pia[..., None] * bbi
        wi = pra[..., None] * bbi + pia[..., None] * bbr

        def bdiag(m):
            e = m.shape[0]
            return _block_diag(m.reshape((e * g,) + m.shape[2:]), per).reshape(
                (e, lbs, per * m.shape[2], per * m.shape[3]))

        taps = (jnp.einsum("gjp,egpi->egij", cr, wr[:S5_TILE], precision=hp)
                - jnp.einsum("gjp,egpi->egij", ci, wi[:S5_TILE], precision=hp))
        taps = bdiag(jnp.concatenate([taps, jnp.zeros_like(taps[:1])], axis=0))
        pos = jnp.arange(S5_TILE)
        dist = (pos[:, None] - pos[None, :]) if reverse else (pos[None, :] - pos[:, None])
        t_dir = taps[jnp.where(dist >= 0, dist, S5_TILE)]
        t_dir = t_dir.transpose(2, 0, 3, 1, 4).reshape(lbs, wide, wide)
        w["t"] = t_dir if dirn == 0 else w["t"] + t_dir
        e_in = pos if reverse else S5_TILE - 1 - pos
        p_mat = jnp.concatenate([bdiag(jnp.swapaxes(wr[e_in], 2, 3).astype(BF16)),
                                 bdiag(jnp.swapaxes(wi[e_in], 2, 3).astype(BF16))], axis=3)
        w["pb" if reverse else "pf"] = p_mat.transpose(1, 0, 2, 3).reshape(lbs, wide, -1)
        e_out = (S5_TILE - pos) if reverse else (pos + 1)
        o_re = jnp.swapaxes(cr * pra[e_out][:, :, None, :] - ci * pia[e_out][:, :, None, :], 2, 3)
        o_im = jnp.swapaxes(-(cr * pia[e_out][:, :, None, :] + ci * pra[e_out][:, :, None, :]), 2, 3)
        o_mat = jnp.concatenate([bdiag(o_re.astype(BF16)), bdiag(o_im.astype(BF16))], axis=2)
        w["ob" if reverse else "of"] = o_mat.transpose(1, 2, 0, 3).reshape(lbs, -1, wide)
        qr, qi = _complex_powers(pr[S5_TILE], pi[S5_TILE], SUBLANES)
        g = abr.shape[0]
        flat = lambda a: a.reshape(g // per, per * a.shape[1])
        t = jnp.arange(SUBLANES)[None, :, None]
        tiles = []
        for dist in (1, 2, 4):
            valid = (t < SUBLANES - dist) if reverse else (t >= dist)
            tiles.append(jnp.where(valid, flat(qr[dist])[:, None, :], 0.0))
            tiles.append(jnp.where(valid, flat(qi[dist])[:, None, :], 0.0))
        order = list(range(SUBLANES, 0, -1)) if reverse else list(range(1, SUBLANES + 1))
        tiles.append(jnp.stack([flat(qr[o]) for o in order], axis=1))
        tiles.append(jnp.stack([flat(qi[o]) for o in order], axis=1))
        out["pw"].append(jnp.stack(tiles, axis=1).astype(F32))
    w["t"] = w["t"].astype(BF16)
    out["w"] = w
    d = glu_w.shape[0]
    out.update(dskip=d_skip.astype(F32)[None, :],
               glu_wa=glu_w[:, :d].astype(BF16), glu_wg=glu_w[:, d:].astype(BF16),
               glu_ba=glu_b[:d].astype(F32)[None, :], glu_bg=glu_b[d:].astype(F32)[None, :])
    return out


def _head_pad_cols(w, per_head, take):
    kdim = w.shape[0]
    w = w.reshape(kdim, MLA_HEADS, per_head)[:, :, take]
    w = jnp.pad(w, ((0, 0), (0, 0), (0, HEAD_PAD - w.shape[2])))
    return w.reshape(kdim, MLA_HEADS * HEAD_PAD)


def _rope_swap(w):
    idx = jnp.arange(MLA_ROPE)
    blk, pos = idx // (2 * ROPE_FREQ), idx % (2 * ROPE_FREQ)
    return w[..., blk * 2 * ROPE_FREQ + (pos + ROPE_FREQ) % (2 * ROPE_FREQ)]


def _mla_params(in_w, q_norm_w, kv_norm_w, qb_w, kvb_w, out_w, seq, ctx_len):
    d = in_w.shape[0]
    qr, kvr = q_norm_w.shape[0], kv_norm_w.shape[0]
    dq = MLA_NOPE + MLA_ROPE
    w_q, w_kv, w_kr = in_w[:, :qr], in_w[:, qr:qr + kvr], in_w[:, qr + kvr:]

    def place(w):
        return jnp.pad(w, ((0, 0), (MLA_NOPE, HEAD_PAD - dq)))

    w_in = jnp.concatenate([w_q, place(w_kr), w_kv, place(_rope_swap(w_kr))], axis=1).astype(BF16)
    qb = qb_w.reshape(qr, MLA_HEADS, dq)
    qa = jnp.pad(qb, ((0, 0), (0, 0), (0, HEAD_PAD - dq))).reshape(qr, -1)
    qb_sw = jnp.pad(_rope_swap(qb[:, :, MLA_NOPE:]), ((0, 0), (0, 0), (MLA_NOPE, HEAD_PAD - dq))).reshape(qr, -1)
    per = MLA_NOPE + MLA_V
    w_k = _head_pad_cols(kvb_w, per, slice(0, MLA_NOPE))
    w_v = _head_pad_cols(kvb_w, per, slice(MLA_NOPE, per))
    w_out = jnp.pad(out_w.reshape(MLA_HEADS, MLA_V, d), ((0, 0), (0, HEAD_PAD - MLA_V), (0, 0))).reshape(-1, d)
    rows = seq // GRID_W
    row = jnp.repeat(jnp.arange(rows, dtype=F32), GRID_W)
    col = jnp.tile(jnp.arange(GRID_W, dtype=F32), rows)
    inv_freq = ROPE_BASE ** (-jnp.arange(ROPE_FREQ, dtype=F32) / ROPE_FREQ)
    ang = jnp.stack([row[:, None] * inv_freq, col[:, None] * inv_freq], axis=1)
    cos, sin = jnp.cos(ang), jnp.sin(ang)
    cos32 = jnp.concatenate([cos, cos], axis=2).reshape(seq, MLA_ROPE)
    sin32 = jnp.concatenate([-sin, sin], axis=2).reshape(seq, MLA_ROPE)
    ones = jnp.ones((seq, MLA_NOPE), F32)
    cos_l = jnp.pad(jnp.concatenate([ones, cos32], axis=1), ((0, 0), (0, HEAD_PAD - dq)))
    sin_l = jnp.pad(sin32, ((0, 0), (MLA_NOPE, HEAD_PAD - dq)))
    cos_c = jnp.pad(jnp.ones((ctx_len, dq), F32), ((0, 0), (0, HEAD_PAD - dq)))
    sin_c = jnp.zeros((ctx_len, HEAD_PAD), F32)
    return dict(q_rank=qr, kv_rank=kvr, w_in=w_in, q_norm=q_norm_w.astype(F32)[None, :],
                kv_norm=kv_norm_w.astype(F32)[None, :], w_qa=qa.astype(BF16), w_qb=qb_sw.astype(BF16),
                w_k=w_k.astype(BF16), w_v=w_v.astype(BF16), w_out=w_out.astype(BF16),
                v_ones=jnp.tile((jnp.arange(HEAD_PAD) == V_ONES_LANE).astype(F32), MLA_HEADS)[None, :],
                tables=dict(lat=(cos_l, sin_l), ctx=(cos_c, sin_c)))


def _modulation(c, c_ctx, ada_w, ada_b):
    nb, d = c.shape
    rows = jnp.zeros((SUBLANES, d), F32).at[:nb].set(c).at[nb].set(c_ctx)
    n = ada_w.shape[1]
    (mod,) = _fused_matmul([(rows, (SUBLANES, d), lambda i, j: (0, 0))], [ada_w.astype(BF16)],
                           [_const_tile_in(ada_b.astype(F32)[None, :], _pick(n, (1024, 512, 256, 128)))],
                           _silu, lambda accs, b: [accs[0] + b], rows=SUBLANES, tm=SUBLANES, k=d, n=n,
                           tn=_pick(n, (1024, 512, 256, 128)), out_dtypes=[F32])
    names = ("sh1", "sc1", "g1", "sh2", "sc2", "g2")
    parts = jnp.split(mod, 6, axis=1)
    lat = {nm: pt[:nb][:, None, :] for nm, pt in zip(names, parts)}
    ctx = {nm: pt[nb:nb + 1][:, None, :] for nm, pt in zip(names, parts)}
    return lat, ctx


def kernel(x, c, ctx, c_ctx, ada_w, ada_b, norm1_w, norm2_w, ffn_w13, ffn_w2, m2_in_w, m2_conv_w, m2_conv_b, m2_dt_bias, m2_a_log, m2_d, m2_norm_w, m2_out_w, lru_in_w, lru_conv_w, lru_conv_b, lru_gate_w, lru_gate_b, lru_a_param, lru_out_w, s5_lambda_re, s5_lambda_im, s5_log_step, s5_b_re, s5_b_im, s5_c_re, s5_c_im, s5_d, s5_glu_w, s5_glu_b, mla_in_w, mla_q_norm_w, mla_kv_norm_w, mla_qb_w, mla_kvb_w, mla_out_w, final_norm_w):
    nb, seq, d = x.shape
    ctx_len = ctx.shape[1]
    depth = ada_w.shape[0]
    hidden = ffn_w2.shape[1]
    xl = x.reshape(nb * seq, d).astype(F32)
    xc = ctx.reshape(nb * ctx_len, d).astype(F32)
    for i in range(depth):
        kind, j = i % N_MIXERS, i // N_MIXERS
        ctx_out = i < depth - 1
        mods_l, mods_c = _modulation(c.astype(F32), c_ctx.astype(F32), ada_w[i], ada_b[i])
        lat_s = _Stream(xl, nb, seq, mods_l)
        ctx_s = _Stream(xc, nb, ctx_len, mods_c)
        n1 = norm1_w[i].astype(F32)[None, :]
        if kind == 0:
            p = _mamba_params(m2_in_w[j], m2_conv_w[j], m2_conv_b[j], m2_dt_bias[j], m2_a_log[j], m2_d[j],
                              m2_norm_w[j], m2_out_w[j])
            p["norm1"] = n1
            outs = _mamba_layer(lat_s, ctx_s, p, ctx_out)
        elif kind == 1:
            p = _rglru_params(lru_in_w[j], lru_conv_w[j], lru_conv_b[j], lru_gate_w[j], lru_gate_b[j],
                              lru_a_param[j], lru_out_w[j])
            p["norm1"] = n1
            outs = _rglru_layer(lat_s, ctx_s, p, ctx_out)
        elif kind == 2:
            p = _s5_params(s5_lambda_re[j], s5_lambda_im[j], s5_log_step[j], s5_b_re[j], s5_b_im[j],
                           s5_c_re[j], s5_c_im[j], s5_d[j], s5_glu_w[j], s5_glu_b[j])
            p["norm1"] = n1
            outs = _s5_layer(lat_s, ctx_s, p, ctx_out)
        else:
            p = _mla_params(mla_in_w[j], mla_q_norm_w[j], mla_kv_norm_w[j], mla_qb_w[j], mla_kvb_w[j],
                            mla_out_w[j], seq, ctx_len)
            p["norm1"] = n1
            outs = _mla_layer(lat_s, ctx_s, p, ctx_out)
        n2 = norm2_w[i].astype(F32)[None, :]
        w1 = ffn_w13[i][:, :hidden].astype(BF16)
        w3 = ffn_w13[i][:, hidden:].astype(BF16)
        w2 = ffn_w2[i].astype(BF16)
        lat_s = _Stream(outs["lat"], nb, seq, mods_l)
        xl = _ffn(lat_s, n2, mods_l["sc2"], mods_l["sh2"], mods_l["g2"], w1, w3, w2)
        if ctx_out:
            ctx_s = _Stream(outs["ctx"], nb, ctx_len, mods_c)
            xc = _ffn(ctx_s, n2, mods_c["sc2"], mods_c["sh2"], mods_c["g2"], w1, w3, w2)
    zero = jnp.zeros((1, 1, d), F32)
    lat_s = _Stream(xl, nb, seq, None)
    out = _norm_mod_rows(xl, final_norm_w.astype(F32)[None, :], zero, zero, tm=lat_s.tm, tpb=lat_s.tpb)
    return out.reshape(nb, seq, d).astype(x.dtype)
```

```python
import functools
import math

import jax
import jax.numpy as jnp
from jax import lax
from jax.experimental import pallas as pl
from jax.experimental.pallas import tpu as pltpu

F32 = jnp.float32
BF16 = jnp.bfloat16

LANES = 128
SUBLANES = 8
VMEM_BUDGET_BYTES = 56 * 2**20

GRID_W = 64
N_MIXERS = 4
NORM_EPS = 1e-6
CONV_WIDTH = 4
CONV_PAD_LEFT = 2
M2_HEADDIM = 64
M2_GROUPS = 4
M2_STATE = 128
SSD_CHUNK = 128
LRU_BLOCK = 128
LRU_C = 8.0
S5_GROUP = 16
S5_STATE = 64
S5_TILE = 8
MLA_HEADS = 16
MLA_NOPE = 64
MLA_ROPE = 32
MLA_V = 64
ROPE_FREQ = MLA_ROPE // 4
ROPE_BASE = 10000.0
HEAD_PAD = 128
MAX_FOLD_ROWS = 64
V_ONES_LANE = MLA_V


def _pick(n, cands):
    for c in cands:
        if n % c == 0:
            return c
    raise ValueError(f"no tile in {cands} divides {n}")


def _params(sem, est_bytes):
    limit = int(min(max(2 * est_bytes, 32 * 2**20), VMEM_BUDGET_BYTES))
    return pltpu.CompilerParams(dimension_semantics=sem, vmem_limit_bytes=limit)


def _nbytes(shape, dtype):
    return math.prod(s for s in shape if s is not None) * jnp.dtype(dtype).itemsize


def _sigmoid(x):
    return 0.5 * jnp.tanh(0.5 * x) + 0.5


def _silu(x):
    return x * _sigmoid(x)


def _softplus(x):
    return jnp.maximum(x, 0.0) + jnp.log1p(jnp.exp(-jnp.abs(x)))


def _gelu_tanh(x):
    return 0.5 * x * (1.0 + jnp.tanh(math.sqrt(2.0 / math.pi) * (x + 0.044715 * (x * x * x))))


def _rms(x, w):
    return x * lax.rsqrt(jnp.mean(x * x, axis=-1, keepdims=True) + NORM_EPS) * w


def _norm_mod(x, w, sc, sh):
    return _rms(x, w) * (1.0 + sc) + sh


def _split3(q):
    q1 = q.astype(BF16)
    r1 = q - q1.astype(F32)
    q2 = r1.astype(BF16)
    q3 = (r1 - q2.astype(F32)).astype(BF16)
    return q1, q2, q3


def _select_cols(q, e, parts=3):
    return sum(jnp.dot(p, e, preferred_element_type=F32) for p in _split3(q)[:parts])


def _select_rows(t, q):
    return sum(jnp.dot(t, p, preferred_element_type=F32) for p in _split3(q))


def _fused_matmul(pro_ins, weights, epi_ins, prologue, epilogue, *, rows, tm, k, n, tn, out_dtypes):
    n_p, n_w, n_e, n_o = len(pro_ins), len(weights), len(epi_ins), len(out_dtypes)

    def body(*refs):
        p = refs[:n_p]
        w = refs[n_p:n_p + n_w]
        e = refs[n_p + n_w:n_p + n_w + n_e]
        o = refs[n_p + n_w + n_e:n_p + n_w + n_e + n_o]
        a_scr = refs[-1]

        @pl.when(pl.program_id(1) == 0)
        def _():
            a_scr[...] = prologue(*[r[...] for r in p]).astype(BF16)

        a = a_scr[...]
        accs = [jnp.dot(a, wr[...], preferred_element_type=F32) for wr in w]
        outs = epilogue(accs, *[r[...] for r in e])
        for o_ref, val in zip(o, outs):
            o_ref[...] = val.astype(o_ref.dtype)

    in_specs = ([pl.BlockSpec(bs, im) for (_, bs, im) in pro_ins]
                + [pl.BlockSpec((k, tn), lambda i, j: (0, j)) for _ in weights]
                + [pl.BlockSpec(bs, im) for (_, bs, im) in epi_ins])
    out_specs = [pl.BlockSpec((tm, tn), lambda i, j: (i, j)) for _ in out_dtypes]
    est = (sum(_nbytes(bs, a.dtype) for (a, bs, _) in pro_ins + epi_ins) * 2
           + n_w * k * tn * 2 * 2 + sum(tm * tn * jnp.dtype(d).itemsize for d in out_dtypes) * 2
           + tm * k * 2 + (n_w + 2) * tm * tn * 4 + tm * k * 8)
    outs = pl.pallas_call(
        body,
        out_shape=[jax.ShapeDtypeStruct((rows, n), d) for d in out_dtypes],
        grid=(rows // tm, n // tn),
        in_specs=in_specs,
        out_specs=out_specs,
        scratch_shapes=[pltpu.VMEM((tm, k), BF16)],
        compiler_params=_params(("parallel", "arbitrary"), est),
        name=f"mm_r{rows}_k{k}_n{n}x{n_w}",
    )(*[a for (a, _, _) in pro_ins], *weights, *[a for (a, _, _) in epi_ins])
    return outs


def _rows_in(a, tm, width, col_block=0):
    return (a, (tm, width), lambda i, j: (i, col_block))


def _batch_in(a, tpb, width):
    if a.shape[0] == 1:
        return (a, (None, 1, width), lambda i, j: (0, 0, 0))
    return (a, (None, 1, width), lambda i, j: (i // tpb, 0, 0))


def _const_in(a, width):
    return (a, (1, width), lambda i, j: (0, 0))


def _tile_in(a, tm, tn):
    return (a, (tm, tn), lambda i, j: (i, j))


def _batch_tile_in(a, tpb, tn):
    if a.shape[0] == 1:
        return (a, (None, 1, tn), lambda i, j: (0, 0, j))
    return (a, (None, 1, tn), lambda i, j: (i // tpb, 0, j))


def _const_tile_in(a, tn):
    return (a, (1, tn), lambda i, j: (0, j))


def _norm_mod_rows(x, w, sc, sh, *, tm, tpb):
    rows, d = x.shape

    def body(x_ref, w_ref, sc_ref, sh_ref, o_ref):
        o_ref[...] = _norm_mod(x_ref[...], w_ref[...], sc_ref[...], sh_ref[...])

    def bidx(a):
        if a.shape[0] == 1:
            return lambda i: (0, 0, 0)
        return lambda i: (i // tpb, 0, 0)

    return pl.pallas_call(
        body,
        out_shape=jax.ShapeDtypeStruct((rows, d), F32),
        grid=(rows // tm,),
        in_specs=[pl.BlockSpec((tm, d), lambda i: (i, 0)),
                  pl.BlockSpec((1, d), lambda i: (0, 0)),
                  pl.BlockSpec((None, 1, d), bidx(sc)),
                  pl.BlockSpec((None, 1, d), bidx(sh))],
        out_specs=pl.BlockSpec((tm, d), lambda i: (i, 0)),
        compiler_params=_params(("parallel",), 6 * tm * d * 4),
        name="norm_mod",
    )(x, w, sc, sh)


def _dwconv(x, w, b, *, nb, seq, col_off, channels, act, out_dtype=F32):
    cb = channels // LANES
    rc = _pick(seq, (512, 256, 128))
    pad = SUBLANES

    def body(x_ref, w_ref, b_ref, o_ref, p_scr):
        p_scr[0:pad, :] = jnp.zeros((pad, LANES), F32)
        p_scr[pad + seq:pad + seq + pad, :] = jnp.zeros((pad, LANES), F32)
        p_scr[pad:pad + seq, :] = x_ref[...].astype(F32)
        wv = w_ref[...]
        bv = b_ref[...]

        def step(i, carry):
            r0 = pl.multiple_of(i * rc, rc)
            y = bv
            for t in range(CONV_WIDTH):
                y = y + wv[t:t + 1, :] * p_scr[pl.ds(r0 + pad - CONV_PAD_LEFT + t, rc), :]
            if act:
                y = _silu(y)
            o_ref[pl.ds(r0, rc), :] = y.astype(o_ref.dtype)
            return carry

        lax.fori_loop(0, seq // rc, step, 0)

    return pl.pallas_call(
        body,
        out_shape=jax.ShapeDtypeStruct((nb * seq, channels), out_dtype),
        grid=(nb, cb),
        in_specs=[pl.BlockSpec((seq, LANES), lambda bi, ci: (bi, col_off + ci)),
                  pl.BlockSpec((CONV_WIDTH, LANES), lambda bi, ci: (0, ci)),
                  pl.BlockSpec((1, LANES), lambda bi, ci: (0, ci))],
        out_specs=pl.BlockSpec((seq, LANES), lambda bi, ci: (bi, ci)),
        scratch_shapes=[pltpu.VMEM((seq + 2 * pad, LANES), F32)],
        compiler_params=_params(("parallel", "parallel"), 5 * seq * LANES * 4),
        name="dwconv",
    )(x, w, b)


def _ssd_direction(xbc, dt, par, dskip, h0, yprev, consts, *, nb, seq, reverse):
    tri, e_head = consts
    q = SSD_CHUNK
    nc = seq // q
    inner = xbc.shape[1] - 2 * M2_GROUPS * M2_STATE
    gw = inner // M2_GROUPS
    heads = inner // M2_HEADDIM
    hpg = heads // M2_GROUPS
    has_prev = yprev is not None
    has_skip = dskip is not None

    def body(*refs):
        it = iter(refs)
        xbc_ref, dt_ref, par_ref = next(it), next(it), next(it)
        dsk_ref = next(it) if has_skip else None
        h0_ref = next(it)
        yp_ref = next(it) if has_prev else None
        tri_ref, eh_ref = next(it), next(it)
        y_ref, hout_ref, h_scr = next(it), next(it), next(it)
        c = pl.program_id(1)

        @pl.when(c == 0)
        def _():
            h_scr[...] = h0_ref[...]

        dtv = _softplus(dt_ref[...] + par_ref[0:1, :])
        da = dtv * (-jnp.exp(par_ref[1:2, :]))
        acs = _select_rows(tri_ref[...], da)
        acs_t = acs.T
        dt_t = dtv.T
        last = acs[0:1, :] if reverse else acs[q - 1:q, :]
        eh = eh_ref[...]
        w_pos = dtv * jnp.exp(last - acs)
        cdec_e = _select_cols(jnp.broadcast_to(jnp.exp(last), (SUBLANES, LANES)), eh)[0:1, :]
        row = lax.broadcasted_iota(jnp.int32, (q, q), 0)
        col = lax.broadcasted_iota(jnp.int32, (q, q), 1)
        mask = (col >= row) if reverse else (col <= row)
        lane = lax.broadcasted_iota(jnp.int32, (q, LANES), 1)
        for g in range(M2_GROUPS):
            bm = xbc_ref[:, inner + g * M2_STATE:inner + (g + 1) * M2_STATE]
            cm = xbc_ref[:, inner + (M2_GROUPS + g) * M2_STATE:inner + (M2_GROUPS + g + 1) * M2_STATE]
            scores = lax.dot_general(cm.astype(BF16), bm.astype(BF16), (((1,), (1,)), ((), ())),
                                     preferred_element_type=F32)
            h_t = h_scr[g]
            h_b = h_t.astype(BF16)
            gs = slice(g * gw, (g + 1) * gw)
            x_d = (xbc_ref[:, gs].astype(F32) * _select_cols(w_pos, eh[:, gs], parts=2)).astype(BF16)
            s_t = jnp.dot(bm.astype(F32).T.astype(BF16), x_d, preferred_element_type=F32)
            h_scr[g] = h_t * cdec_e[:, g * gw:(g + 1) * gw] + s_t
            for pr in range(hpg // 2):
                k0 = g * hpg + 2 * pr
                c0 = k0 * M2_HEADDIM
                x_pair = xbc_ref[:, c0:c0 + LANES]
                rhs = jnp.concatenate([x_pair.astype(BF16), h_b[:, 2 * pr * M2_HEADDIM:2 * pr * M2_HEADDIM + LANES]],
                                      axis=0)
                ys = []
                for k in (k0, k0 + 1):
                    a_col = jnp.broadcast_to(acs_t[k:k + 1, :], (q, q)).T
                    decay = jnp.exp(jnp.where(mask, a_col - acs_t[k:k + 1, :], -jnp.inf))
                    m = (scores * decay * dt_t[k:k + 1, :]).astype(BF16)
                    c_e = (cm.astype(F32) * jnp.exp(a_col)).astype(BF16)
                    ys.append(jnp.dot(jnp.concatenate([m, c_e], axis=1), rhs, preferred_element_type=F32))
                y = jnp.where(lane < M2_HEADDIM, ys[0], ys[1])
                if has_skip:
                    y = y + dsk_ref[:, c0:c0 + LANES] * x_pair.astype(F32)
                if has_prev:
                    y = y + yp_ref[:, c0:c0 + LANES]
                y_ref[:, c0:c0 + LANES] = y

        @pl.when(c == nc - 1)
        def _():
            hout_ref[...] = h_scr[...]

    def rb(bi, ci):
        return bi * nc + ((nc - 1 - ci) if reverse else ci)

    width = xbc.shape[1]
    st_shape = (M2_GROUPS, M2_STATE, gw)
    ins = [xbc, dt, par]
    in_specs = [pl.BlockSpec((q, width), lambda bi, ci: (rb(bi, ci), 0)),
                pl.BlockSpec((q, LANES), lambda bi, ci: (rb(bi, ci), 0)),
                pl.BlockSpec((SUBLANES, LANES), lambda bi, ci: (0, 0))]
    if has_skip:
        ins.append(dskip)
        in_specs.append(pl.BlockSpec((1, inner), lambda bi, ci: (0, 0)))
    ins.append(h0)
    in_specs.append(pl.BlockSpec((None,) + st_shape, lambda bi, ci: (bi, 0, 0, 0)))
    if has_prev:
        ins.append(yprev)
        in_specs.append(pl.BlockSpec((q, inner), lambda bi, ci: (rb(bi, ci), 0)))
    ins += [tri, e_head]
    in_specs += [pl.BlockSpec(tri.shape, lambda bi, ci: (0, 0)),
                 pl.BlockSpec(e_head.shape, lambda bi, ci: (0, 0))]
    est = (q * width * 4 * 2 + q * inner * 4 * 4 + 3 * math.prod(st_shape) * 4 * 2
           + e_head.size * 2 * 2 + 12 * q * inner * 4 + q * heads * LANES * 4)
    y, h_last = pl.pallas_call(
        body,
        out_shape=[jax.ShapeDtypeStruct((nb * seq, inner), F32),
                   jax.ShapeDtypeStruct((nb,) + st_shape, F32)],
        grid=(nb, nc),
        in_specs=in_specs,
        out_specs=[pl.BlockSpec((q, inner), lambda bi, ci: (rb(bi, ci), 0)),
                   pl.BlockSpec((None,) + st_shape, lambda bi, ci: (bi, 0, 0, 0))],
        scratch_shapes=[pltpu.VMEM(st_shape, F32)],
        compiler_params=_params(("parallel", "arbitrary"), est),
        name="ssd_bwd" if reverse else "ssd_fwd",
    )(*ins)
    return y, h_last


def _ssd_consts(inner, reverse):
    heads = inner // M2_HEADDIM
    r = jnp.arange(SSD_CHUNK)
    tri = (r[None, :] >= r[:, None]) if reverse else (r[None, :] <= r[:, None])
    hk = jnp.arange(LANES)[:, None]
    e_head = (hk == (jnp.arange(inner)[None, :] // M2_HEADDIM))
    return tri.astype(BF16), e_head.astype(BF16)


def _tile_scan(a, u, reverse):
    row = lax.broadcasted_iota(jnp.int32, a.shape, 0)
    for d in (1, 2, 4):
        shift = (SUBLANES - d) if reverse else d
        valid = (row < SUBLANES - d) if reverse else (row >= d)
        a_s = jnp.where(valid, pltpu.roll(a, shift, 0), 1.0)
        u_s = jnp.where(valid, pltpu.roll(u, shift, 0), 0.0)
        u = a * u_s + u
        a = a * a_s
    return a, u


def _rglru_direction(xc, gate_w, gate_b, a_param, h0, rprev, *, nb, seq, reverse):
    width = xc.shape[1]
    nblk = width // LRU_BLOCK
    tb = _pick(seq, (512, 256, 128))
    nt = seq // tb
    has_prev = rprev is not None

    def body(*refs):
        it = iter(refs)
        x_ref, gw_ref, gb_ref, ap_ref, h0_ref = next(it), next(it), next(it), next(it), next(it)
        rp_ref = next(it) if has_prev else None
        r_ref, hout_ref, a_scr, u_scr, h_scr = next(it), next(it), next(it), next(it), next(it)
        tstep = pl.program_id(1)

        @pl.when(tstep == 0)
        def _():
            h_scr[...] = jnp.broadcast_to(h0_ref[...], (SUBLANES, width))

        log_base = -LRU_C * _softplus(-ap_ref[...])
        for nbk in range(nblk):
            cs = slice(nbk * LRU_BLOCK, (nbk + 1) * LRU_BLOCK)
            xb = x_ref[:, cs]
            g = jnp.dot(xb.astype(BF16), gw_ref[nbk], preferred_element_type=F32) + gb_ref[nbk]
            g = _sigmoid(g)
            log_a = g[:, :LRU_BLOCK] * log_base[:, cs]
            a = jnp.exp(log_a)
            gap = jnp.maximum(1.0 - a * a, 0.0)
            mult = jnp.where(gap > 0.0, gap * lax.rsqrt(gap), 0.0)
            a_scr[:, cs] = a
            u_scr[:, cs] = xb * g[:, LRU_BLOCK:] * mult

        def step(i, h):
            grp = (tb // SUBLANES - 1 - i) if reverse else i
            r0 = pl.multiple_of(grp * SUBLANES, SUBLANES)
            a_c, h_loc = _tile_scan(a_scr[pl.ds(r0, SUBLANES), :], u_scr[pl.ds(r0, SUBLANES), :], reverse)
            h_new = a_c * h + h_loc
            out = h_new
            if has_prev:
                out = out + rp_ref[pl.ds(r0, SUBLANES), :]
            r_ref[pl.ds(r0, SUBLANES), :] = out
            edge = h_new[0:1, :] if reverse else h_new[SUBLANES - 1:SUBLANES, :]
            return jnp.broadcast_to(edge, (SUBLANES, width))

        h_fin = lax.fori_loop(0, tb // SUBLANES, step, h_scr[...])
        h_scr[...] = h_fin

        @pl.when(tstep == nt - 1)
        def _():
            hout_ref[...] = h_fin[0:1, :]

    def rb(bi, ti):
        return bi * nt + ((nt - 1 - ti) if reverse else ti)

    ins = [xc, gate_w, gate_b, a_param, h0]
    in_specs = [pl.BlockSpec((tb, width), lambda bi, ti: (rb(bi, ti), 0)),
                pl.BlockSpec(gate_w.shape, lambda bi, ti: (0, 0, 0)),
                pl.BlockSpec(gate_b.shape, lambda bi, ti: (0, 0, 0)),
                pl.BlockSpec((1, width), lambda bi, ti: (0, 0)),
                pl.BlockSpec((None, 1, width), lambda bi, ti: (bi, 0, 0))]
    if has_prev:
        ins.append(rprev)
        in_specs.append(pl.BlockSpec((tb, width), lambda bi, ti: (rb(bi, ti), 0)))
    est = 10 * tb * width * 4 + gate_w.size * 2 * 2
    r, h_last = pl.pallas_call(
        body,
        out_shape=[jax.ShapeDtypeStruct((nb * seq, width), F32),
                   jax.ShapeDtypeStruct((nb, 1, width), F32)],
        grid=(nb, nt),
        in_specs=in_specs,
        out_specs=[pl.BlockSpec((tb, width), lambda bi, ti: (rb(bi, ti), 0)),
                   pl.BlockSpec((None, 1, width), lambda bi, ti: (bi, 0, 0))],
        scratch_shapes=[pltpu.VMEM((tb, width), F32), pltpu.VMEM((tb, width), F32),
                        pltpu.VMEM((SUBLANES, width), F32)],
        compiler_params=_params(("parallel", "arbitrary"), est),
        name="rglru_bwd" if reverse else "rglru_fwd",
    )(*ins)
    return r, h_last


def _s5_bidir(u, w, pw_f, pw_b, h_in, *, nb, seq):
    d_model = u.shape[1]
    lbs = d_model // LANES
    sw = pw_f.shape[3]
    m_t = seq // S5_TILE
    wide = S5_TILE * LANES

    def scan(s_scr, pw_ref, c_re, c_im, reverse):
        row = lax.broadcasted_iota(jnp.int32, (SUBLANES, sw), 0)

        def step(i, carry):
            hr, hi = carry
            grp = (m_t // SUBLANES - 1 - i) if reverse else i
            r0 = pl.multiple_of(grp * SUBLANES, SUBLANES)
            xr = s_scr[pl.ds(r0, SUBLANES), 0:sw]
            xi = s_scr[pl.ds(r0, SUBLANES), sw:2 * sw]
            for n_step, dist in enumerate((1, 2, 4)):
                shift = (SUBLANES - dist) if reverse else dist
                pr = pw_ref[2 * n_step]
                pi = pw_ref[2 * n_step + 1]
                sr = pltpu.roll(xr, shift, 0)
                si = pltpu.roll(xi, shift, 0)
                xr, xi = xr + (pr * sr - pi * si), xi + (pr * si + pi * sr)
            nr = xr + (pw_ref[6] * hr - pw_ref[7] * hi)
            ni = xi + (pw_ref[6] * hi + pw_ref[7] * hr)
            edge = (SUBLANES - 1) if reverse else 0
            back = 1 if not reverse else SUBLANES - 1
            s_scr[pl.ds(r0, SUBLANES), 0:sw] = jnp.where(row == edge, hr, pltpu.roll(nr, back, 0))
            s_scr[pl.ds(r0, SUBLANES), sw:2 * sw] = jnp.where(row == edge, hi, pltpu.roll(ni, back, 0))
            e = 0 if reverse else SUBLANES - 1
            return (jnp.broadcast_to(nr[e:e + 1, :], (SUBLANES, sw)),
                    jnp.broadcast_to(ni[e:e + 1, :], (SUBLANES, sw)))

        init = (jnp.broadcast_to(c_re, (SUBLANES, sw)), jnp.broadcast_to(c_im, (SUBLANES, sw)))
        return lax.fori_loop(0, m_t // SUBLANES, step, init)

    def body(u_ref, t_ref, pf_ref, pb_ref, of_ref, ob_ref, pwf_ref, pwb_ref, hin_ref,
             y_ref, hout_ref, x_scr, f_scr, g_scr):
        for l in range(S5_TILE):
            x_scr[:, l * LANES:(l + 1) * LANES] = u_ref[pl.ds(l, m_t, stride=S5_TILE), :].astype(BF16)
        x = x_scr[...]
        f_scr[...] = jnp.dot(x, pf_ref[...], preferred_element_type=F32)
        g_scr[...] = jnp.dot(x, pb_ref[...], preferred_element_type=F32)
        fr, fi = scan(f_scr, pwf_ref, hin_ref[0:1, :], hin_ref[1:2, :], False)
        gr, gi = scan(g_scr, pwb_ref, hin_ref[2:3, :], hin_ref[3:4, :], True)
        hout_ref[0:1, :] = fr[0:1, :]
        hout_ref[1:2, :] = fi[0:1, :]
        hout_ref[2:3, :] = gr[0:1, :]
        hout_ref[3:4, :] = gi[0:1, :]
        y = (jnp.dot(x, t_ref[...], preferred_element_type=F32)
             + jnp.dot(f_scr[...].astype(BF16), of_ref[...], preferred_element_type=F32)
             + jnp.dot(g_scr[...].astype(BF16), ob_ref[...], preferred_element_type=F32))
        for l in range(S5_TILE):
            y_ref[pl.ds(l, m_t, stride=S5_TILE), :] = y[:, l * LANES:(l + 1) * LANES]

    def wspec():
        return pl.BlockSpec((None, wide, wide), lambda li, bi: (li, 0, 0), pipeline_mode=pl.Buffered(1))

    est = (4 * seq * LANES * 4 + 5 * wide * wide * 2 + m_t * wide * 2 + 2 * m_t * 2 * sw * 4
           + 3 * m_t * wide * 4)
    y, h_out = pl.pallas_call(
        body,
        out_shape=[jax.ShapeDtypeStruct((nb * seq, d_model), F32),
                   jax.ShapeDtypeStruct((nb, lbs, 4, sw), F32)],
        grid=(lbs, nb),
        in_specs=[pl.BlockSpec((seq, LANES), lambda li, bi: (bi, li)),
                  wspec(), wspec(), wspec(), wspec(), wspec(),
                  pl.BlockSpec((None, 8, SUBLANES, sw), lambda li, bi: (li, 0, 0, 0)),
                  pl.BlockSpec((None, 8, SUBLANES, sw), lambda li, bi: (li, 0, 0, 0)),
                  pl.BlockSpec((None, None, 4, sw), lambda li, bi: (bi, li, 0, 0))],
        out_specs=[pl.BlockSpec((seq, LANES), lambda li, bi: (bi, li)),
                   pl.BlockSpec((None, None, 4, sw), lambda li, bi: (bi, li, 0, 0))],
        scratch_shapes=[pltpu.VMEM((m_t, wide), BF16), pltpu.VMEM((m_t, 2 * sw), F32),
                        pltpu.VMEM((m_t, 2 * sw), F32)],
        compiler_params=_params(("parallel", "parallel"), est),
        name="s5_bidir",
    )(u, w["t"], w["pf"], w["pb"], w["of"], w["ob"], pw_f, pw_b, h_in)
    return y, h_out


def _attention(q, kv_parts, *, nb, lq, scale):
    heads = q.shape[1] // HEAD_PAD
    tq = _pick(lq, (2048, 1024, 512, 256, 128))
    tsub = _pick(tq, (256, 128))
    nq = lq // tq
    c = scale * math.log2(math.e)
    chunks = [_pick(length, (2048, 1024, 512, 256, 128)) for (_, _, length) in kv_parts]
    n_parts = len(kv_parts)

    def body(*refs):
        q_ref = refs[0]
        kv_refs = refs[1:1 + 2 * n_parts]
        o_ref = refs[1 + 2 * n_parts]
        s_a, s_b = refs[2 + 2 * n_parts], refs[3 + 2 * n_parts]

        def q_tile(t, carry):
            r0 = pl.multiple_of(t * tsub, tsub)
            qt = q_ref[pl.ds(r0, tsub), :]

            def key_parts(n):
                parts = 4 if n >= 2048 else 2 if n >= 2 * MAX_FOLD_ROWS else 1
                return tuple((i * n // parts, (i + 1) * n // parts) for i in range(parts))

            def scores(k):
                return jnp.concatenate(
                    [lax.dot_general(k[a:b], qt, (((1,), (1,)), ((), ())), preferred_element_type=F32)
                     for a, b in key_parts(k.shape[0])], axis=0)

            def absorb(s_t, v_t, m, acc):
                part = jnp.max(s_t.reshape(-1, MAX_FOLD_ROWS, tsub), axis=0)
                m_new = jnp.maximum(m, jnp.max(part, axis=0, keepdims=True))
                p_t = jnp.exp2((s_t - m_new) * c).astype(BF16)
                alpha = jnp.exp2((m - m_new) * c)
                acc = alpha * acc
                for a, b in key_parts(s_t.shape[0]):
                    acc = acc + jnp.dot(v_t[:, a:b], p_t[a:b], preferred_element_type=F32)
                return m_new, acc

            m = jnp.full((1, tsub), -jnp.inf, F32)
            acc = jnp.zeros((HEAD_PAD, tsub), F32)
            sched = [(kv_refs[2 * pi], kv_refs[2 * pi + 1], ci, chunks[pi])
                     for pi, (_, _, length) in enumerate(kv_parts) for ci in range(length // chunks[pi])]
            bufs = (s_a, s_b)

            def issue(i):
                k_ref, _, ci, tk = sched[i]
                bufs[i % 2][0:tk, :] = scores(k_ref[ci * tk:(ci + 1) * tk, :])

            issue(0)
            for i, (_, v_ref, ci, tk) in enumerate(sched):
                if i + 1 < len(sched):
                    issue(i + 1)
                m, acc = absorb(bufs[i % 2][0:tk, :], v_ref[ci], m, acc)
            o_t = acc / acc[V_ONES_LANE:V_ONES_LANE + 1, :]
            o_ref[pl.ds(r0, tsub), :] = o_t.T.astype(o_ref.dtype)
            return carry

        lax.fori_loop(0, tq // tsub, q_tile, 0)

    ins = [q]
    in_specs = [pl.BlockSpec((tq, HEAD_PAD), lambda b, h, qi: (b * nq + qi, h))]
    est = tq * HEAD_PAD * 2 * 4 + 6 * tsub * max(chunks) * 4
    for (k, v, length), tk in zip(kv_parts, chunks):
        n_ch = length // tk
        v_t = v.reshape(nb, n_ch, tk, heads, HEAD_PAD).transpose(0, 3, 1, 4, 2)
        ins += [k, v_t]
        in_specs += [pl.BlockSpec((length, HEAD_PAD), lambda b, h, qi: (b, h)),
                     pl.BlockSpec((None, None, n_ch, HEAD_PAD, tk), lambda b, h, qi: (b, h, 0, 0, 0))]
        est += 2 * length * HEAD_PAD * 2 * 2
    return pl.pallas_call(
        body,
        out_shape=jax.ShapeDtypeStruct(q.shape, BF16),
        grid=(nb, heads, nq),
        in_specs=in_specs,
        out_specs=pl.BlockSpec((tq, HEAD_PAD), lambda b, h, qi: (b * nq + qi, h)),
        scratch_shapes=[pltpu.VMEM((max(chunks), tsub), F32), pltpu.VMEM((max(chunks), tsub), F32)],
        compiler_params=_params(("parallel", "parallel", "arbitrary"), est),
        name="attention",
    )(*ins)


class _Stream:
    def __init__(self, x, nb, seq, mods):
        self.x = x
        self.nb = nb
        self.seq = seq
        self.tm = _pick(seq, (512, 256, 128))
        self.tpb = seq // self.tm
        self.mods = mods
        self.rows = nb * seq


def _in_proj(st, norm_w, sc, sh, weights, n, tn, out_dtypes=None):
    d = st.x.shape[1]
    pro = [_rows_in(st.x, st.tm, d), _const_in(norm_w, d), _batch_in(sc, st.tpb, d), _batch_in(sh, st.tpb, d)]
    return _fused_matmul(pro, weights, [], _norm_mod, lambda accs: accs, rows=st.rows, tm=st.tm, k=d, n=n,
                         tn=tn, out_dtypes=out_dtypes or [F32] * len(weights))


def _out_proj(st, pro_ins, prologue, weight, gate, k):
    d = st.x.shape[1]
    epi = [_tile_in(st.x, st.tm, d), _batch_tile_in(gate, st.tpb, d)]
    (out,) = _fused_matmul(pro_ins, [weight], epi, prologue, lambda accs, x, g: [x + g * accs[0]],
                           rows=st.rows, tm=st.tm, k=k, n=d, tn=d, out_dtypes=[F32])
    return out


def _ffn(st, norm_w, sc, sh, gate, w1, w3, w2):
    d = st.x.shape[1]
    hidden = w1.shape[1]
    th = _pick(hidden, (1408, 1024, 512, 256, 128))
    tm, tpb = st.tm, st.tpb
    nh = hidden // th

    def body(x_ref, nw_ref, sc_ref, sh_ref, w1_ref, w3_ref, w2_ref, g_ref, o_ref, a_scr, acc_scr):
        j = pl.program_id(1)

        @pl.when(j == 0)
        def _():
            a_scr[...] = _norm_mod(x_ref[...], nw_ref[...], sc_ref[...], sh_ref[...]).astype(BF16)
            acc_scr[...] = jnp.zeros((tm, d), F32)

        a = a_scr[...]
        up = (_silu(jnp.dot(a, w1_ref[...], preferred_element_type=F32))
              * jnp.dot(a, w3_ref[...], preferred_element_type=F32))
        acc_scr[...] += jnp.dot(up.astype(BF16), w2_ref[...], preferred_element_type=F32)

        @pl.when(j == nh - 1)
        def _():
            o_ref[...] = x_ref[...] + g_ref[...] * acc_scr[...]

    def bidx(a):
        if a.shape[0] == 1:
            return lambda i, j: (0, 0, 0)
        return lambda i, j: (i // tpb, 0, 0)

    est = 4 * tm * d * 4 + 3 * d * th * 2 * 2 + tm * d * 6 + 4 * tm * th * 4
    return pl.pallas_call(
        body,
        out_shape=jax.ShapeDtypeStruct((st.rows, d), F32),
        grid=(st.rows // tm, nh),
        in_specs=[pl.BlockSpec((tm, d), lambda i, j: (i, 0)),
                  pl.BlockSpec((1, d), lambda i, j: (0, 0)),
                  pl.BlockSpec((None, 1, d), bidx(sc)),
                  pl.BlockSpec((None, 1, d), bidx(sh)),
                  pl.BlockSpec((d, th), lambda i, j: (0, j)),
                  pl.BlockSpec((d, th), lambda i, j: (0, j)),
                  pl.BlockSpec((th, d), lambda i, j: (j, 0)),
                  pl.BlockSpec((None, 1, d), bidx(gate))],
        out_specs=pl.BlockSpec((tm, d), lambda i, j: (i, 0)),
        scratch_shapes=[pltpu.VMEM((tm, d), BF16), pltpu.VMEM((tm, d), F32)],
        compiler_params=_params(("parallel", "arbitrary"), est),
        name="ffn",
    )(st.x, norm_w, sc, sh, w1, w3, w2, gate)


def _mamba_layer(lat, ctx, p, ctx_out):
    inner = p["inner"]
    conv_ch = p["conv_ch"]
    heads = inner // M2_HEADDIM
    gw = inner // M2_GROUPS
    outs = {}
    acts = {}
    for name, st in (("ctx", ctx), ("lat", lat)):
        m = st.mods
        (zx,) = _in_proj(st, p["norm1"], m["sc1"], m["sh1"], [p["w_zx"]], inner + conv_ch,
                         _pick(inner + conv_ch, (2560, 1024, 512, 256, 128)), out_dtypes=[BF16])
        dt0, dt1 = _in_proj(st, p["norm1"], m["sc1"], m["sh1"], [p["w_dt0"], p["w_dt1"]], LANES, LANES)
        xbc = _dwconv(zx, p["conv_w"], p["conv_b"], nb=st.nb, seq=st.seq, col_off=inner // LANES,
                      channels=conv_ch, act=True, out_dtype=BF16)
        acts[name] = (zx, (dt0, dt1), xbc)
    ys = {}
    for dirn, reverse in ((0, False), (1, True)):
        consts = _ssd_consts(inner, reverse)
        h = jnp.zeros((ctx.nb, M2_GROUPS, M2_STATE, gw), F32)
        for name, st in (("ctx", ctx), ("lat", lat)):
            zx, dts, xbc = acts[name]
            y, h = _ssd_direction(xbc, dts[dirn], p["par"][dirn], p["dskip"] if dirn == 0 else None, h,
                                  ys.get(name), consts, nb=st.nb, seq=st.seq, reverse=reverse)
            ys[name] = y

    def gated_norm(y, z, w):
        return _rms(y * _silu(z.astype(F32)), w)

    for name, st in (("ctx", ctx), ("lat", lat)):
        if name == "ctx" and not ctx_out:
            continue
        zx = acts[name][0]
        pro = [_rows_in(ys[name], st.tm, inner), _rows_in(zx, st.tm, inner), _const_in(p["norm_w"], inner)]
        outs[name] = _out_proj(st, pro, gated_norm, p["w_out"], st.mods["g1"], inner)
    return outs


def _rglru_layer(lat, ctx, p, ctx_out):
    width = p["width"]
    outs = {}
    acts = {}
    for name, st in (("ctx", ctx), ("lat", lat)):
        m = st.mods
        (yx,) = _in_proj(st, p["norm1"], m["sc1"], m["sh1"], [p["w_in"]], 2 * width,
                         _pick(2 * width, (2560, 1280, 1024, 512, 256, 128)), out_dtypes=[BF16])
        xc = _dwconv(yx, p["conv_w"], p["conv_b"], nb=st.nb, seq=st.seq, col_off=width // LANES,
                     channels=width, act=False)
        acts[name] = (yx, xc)
    rs = {}
    for dirn, reverse in ((0, False), (1, True)):
        h = jnp.zeros((ctx.nb, 1, width), F32)
        for name, st in (("ctx", ctx), ("lat", lat)):
            r, h = _rglru_direction(acts[name][1], p["gate_w"][dirn], p["gate_b"][dirn], p["a_param"][dirn], h,
                                    rs.get(name), nb=st.nb, seq=st.seq, reverse=reverse)
            rs[name] = r
    for name, st in (("ctx", ctx), ("lat", lat)):
        if name == "ctx" and not ctx_out:
            continue
        pro = [_rows_in(acts[name][0], st.tm, width), _rows_in(rs[name], st.tm, width)]
        outs[name] = _out_proj(st, pro, lambda y, r: _gelu_tanh(y.astype(F32)) * r, p["w_out"], st.mods["g1"], width)
    return outs


def _s5_layer(lat, ctx, p, ctx_out):
    d = lat.x.shape[1]
    lbs = d // LANES
    sw = p["pw"][0].shape[3]
    us = {}
    for name, st in (("ctx", ctx), ("lat", lat)):
        m = st.mods
        us[name] = _norm_mod_rows(st.x, p["norm1"], m["sc1"], m["sh1"], tm=st.tm, tpb=st.tpb)
    ys = {}
    h = jnp.zeros((ctx.nb, lbs, 4, sw), F32)
    for name, st in (("ctx", ctx), ("lat", lat)):
        ys[name], h = _s5_bidir(us[name], p["w"], p["pw"][0], p["pw"][1], h, nb=st.nb, seq=st.seq)
    outs = {}
    for name, st in (("ctx", ctx), ("lat", lat)):
        if name == "ctx" and not ctx_out:
            continue
        pro = [_rows_in(ys[name], st.tm, d), _rows_in(us[name], st.tm, d), _const_in(p["dskip"], d)]
        epi = [_const_tile_in(p["glu_ba"], d), _const_tile_in(p["glu_bg"], d),
               _tile_in(st.x, st.tm, d), _batch_tile_in(st.mods["g1"], st.tpb, d)]
        (out,) = _fused_matmul(
            pro, [p["glu_wa"], p["glu_wg"]], epi,
            lambda y, u, dsk: _gelu_tanh(y + dsk * u),
            lambda accs, ba, bg, x, g: [x + g * ((accs[0] + ba) * _sigmoid(accs[1] + bg))],
            rows=st.rows, tm=st.tm, k=d, n=d, tn=d, out_dtypes=[F32])
        outs[name] = out
    return outs


def _mla_layer(lat, ctx, p, ctx_out):
    d = lat.x.shape[1]
    qr, kvr = p["q_rank"], p["kv_rank"]
    hp = MLA_HEADS * HEAD_PAD
    tn_h = _pick(hp, (1024, 512, 256, 128))
    reps = tn_h // HEAD_PAD
    qkv = {}
    for name, st in (("ctx", ctx), ("lat", lat)):
        m = st.mods
        n_in = p["w_in"].shape[1]
        (lat_all,) = _in_proj(st, p["norm1"], m["sc1"], m["sh1"], [p["w_in"]], n_in, n_in)
        cos_t, sin_t = p["tables"][name]
        tab_spec = lambda a, st=st: (a, (st.tm, HEAD_PAD), lambda i, j: (i % st.tpb, 0))

        def rope_epi(accs, cos, sin, reps=reps):
            cos_r = jnp.concatenate([cos] * reps, axis=1)
            sin_r = jnp.concatenate([sin] * reps, axis=1)
            return [accs[0] * cos_r + accs[1] * sin_r]

        want_q = name == "lat" or ctx_out
        q = None
        if want_q:
            (q,) = _fused_matmul([_rows_in(lat_all, st.tm, qr, 0), _const_in(p["q_norm"], qr)],
                                 [p["w_qa"], p["w_qb"]], [tab_spec(cos_t), tab_spec(sin_t)],
                                 _rms, rope_epi, rows=st.rows, tm=st.tm, k=qr, n=hp, tn=tn_h, out_dtypes=[BF16])

        def kv_epi(accs, kra, krb, cos, sin, ones, reps=reps):
            kr = kra * cos + krb * sin
            return [accs[0] + jnp.concatenate([kr] * reps, axis=1), accs[1] + ones]

        kra_blk = (qr // LANES)
        kv_blk = (qr + LANES) // kvr
        krb_blk = (qr + LANES + kvr) // LANES
        k, v = _fused_matmul(
            [_rows_in(lat_all, st.tm, kvr, kv_blk), _const_in(p["kv_norm"], kvr)],
            [p["w_k"], p["w_v"]],
            [(lat_all, (st.tm, LANES), lambda i, j: (i, kra_blk)),
             (lat_all, (st.tm, LANES), lambda i, j: (i, krb_blk)),
             tab_spec(cos_t), tab_spec(sin_t), _const_tile_in(p["v_ones"], tn_h)],
            _rms, kv_epi, rows=st.rows, tm=st.tm, k=kvr, n=hp, tn=tn_h, out_dtypes=[BF16, BF16])
        qkv[name] = (q, k, v)
    scale = (MLA_NOPE + MLA_ROPE) ** -0.5
    nb = lat.nb
    kv_c = (qkv["ctx"][1], qkv["ctx"][2], ctx.seq)
    kv_l = (qkv["lat"][1], qkv["lat"][2], lat.seq)
    outs = {}
    o_l = _attention(qkv["lat"][0], [kv_c, kv_l], nb=nb, lq=lat.seq, scale=scale)
    outs["lat"] = _out_proj(lat, [_rows_in(o_l, lat.tm, hp)], lambda a: a, p["w_out"], lat.mods["g1"], hp)
    if ctx_out:
        o_c = _attention(qkv["ctx"][0], [kv_c], nb=nb, lq=ctx.seq, scale=scale)
        outs["ctx"] = _out_proj(ctx, [_rows_in(o_c, ctx.tm, hp)], lambda a: a, p["w_out"], ctx.mods["g1"], hp)
    return outs


def _pad_cols(w, n):
    return jnp.pad(w, ((0, 0), (0, n - w.shape[1])))


def _mamba_params(in_w, conv_w, conv_b, dt_bias, a_log, d_skip, norm_w, out_w):
    d, proj = in_w.shape
    heads = dt_bias.shape[1]
    inner = heads * M2_HEADDIM
    conv_ch = inner + 2 * M2_GROUPS * M2_STATE
    par = []
    for dirn in range(2):
        rows = jnp.zeros((SUBLANES, LANES), F32)
        rows = rows.at[0, :heads].set(dt_bias[dirn].astype(F32)).at[1, :heads].set(a_log[dirn].astype(F32))
        par.append(rows)
    off = inner + conv_ch
    return dict(
        inner=inner, conv_ch=conv_ch,
        w_zx=in_w[:, :off].astype(BF16),
        w_dt0=_pad_cols(in_w[:, off:off + heads], LANES).astype(BF16),
        w_dt1=_pad_cols(in_w[:, off + heads:off + 2 * heads], LANES).astype(BF16),
        conv_w=conv_w.astype(F32), conv_b=conv_b.astype(F32)[None, :],
        par=par, dskip=jnp.repeat(d_skip.astype(F32), M2_HEADDIM)[None, :],
        norm_w=norm_w.astype(F32)[None, :], w_out=out_w.astype(BF16))


def _rglru_params(in_w, conv_w, conv_b, gate_w, gate_b, a_param, out_w):
    width = conv_w.shape[1]
    return dict(
        width=width, w_in=in_w.astype(BF16), conv_w=conv_w.astype(F32), conv_b=conv_b.astype(F32)[None, :],
        gate_w=[gate_w[dirn].astype(BF16) for dirn in range(2)],
        gate_b=[gate_b[dirn].astype(F32)[:, None, :] for dirn in range(2)],
        a_param=[a_param[dirn].astype(F32)[None, :] for dirn in range(2)],
        w_out=out_w.astype(BF16))


def _block_diag(m, per):
    g, r, c = m.shape
    m = m.reshape(g // per, per, r, 1, c)
    on_diag = jnp.arange(per)[:, None, None, None] == jnp.arange(per)[None, None, :, None]
    return jnp.where(on_diag, m, jnp.zeros((), m.dtype)).reshape(g // per, per * r, per * c)


def _complex_powers(ar, ai, n):
    pr, pi = [jnp.ones_like(ar)], [jnp.zeros_like(ai)]
    for _ in range(n):
        pr, pi = pr + [pr[-1] * ar - pi[-1] * ai], pi + [pr[-1] * ai + pi[-1] * ar]
    return pr, pi


def _s5_params(lam_re, lam_im, log_step, b_re, b_im, c_re, c_im, d_skip, glu_w, glu_b):
    per = LANES // S5_GROUP
    hp = lax.Precision.HIGHEST
    br, bi = b_re.astype(F32), b_im.astype(F32)
    out = dict(pw=[])
    w = {}
    for dirn in range(2):
        reverse = dirn == 1
        lr = jnp.minimum(lam_re[dirn].astype(F32), -1e-4)
        li = lam_im[dirn].astype(F32)
        step = jnp.exp(log_step[dirn].astype(F32))[:, None]
        mag = jnp.exp(lr * step)
        abr, abi = mag * jnp.cos(li * step), mag * jnp.sin(li * step)
        den = lr * lr + li * li
        zr = ((abr - 1.0) * lr + abi * li) / den
        zi = (abi * lr - (abr - 1.0) * li) / den
        bbr = zr[..., None] * br - zi[..., None] * bi
        bbi = zr[..., None] * bi + zi[..., None] * br
        cr, ci = c_re[dirn].astype(F32), c_im[dirn].astype(F32)
        pr, pi = _complex_powers(abr, abi, S5_TILE)

        pra, pia = jnp.stack(pr), jnp.stack(pi)
        g = abr.shape[0]
        lbs, wide = g // per, S5_TILE * LANES
        wr = pra[..., None] * bbr - pia[..., None] * bbi
        wi = pra[..., None] * bbi + pia[..., None] * bbr

        def bdiag(m):
            e = m.shape[0]
            return _block_diag(m.reshape((e * g,) + m.shape[2:]), per).reshape(
                (e, lbs, per * m.shape[2], per * m.shape[3]))

        taps = (jnp.einsum("gjp,egpi->egij", cr, wr[:S5_TILE], precision=hp)
                - jnp.einsum("gjp,egpi->egij", ci, wi[:S5_TILE], precision=hp))
        taps = bdiag(jnp.concatenate([taps, jnp.zeros_like(taps[:1])], axis=0))
        pos = jnp.arange(S5_TILE)
        dist = (pos[:, None] - pos[None, :]) if reverse else (pos[None, :] - pos[:, None])
        t_dir = taps[jnp.where(dist >= 0, dist, S5_TILE)]
        t_dir = t_dir.transpose(2, 0, 3, 1, 4).reshape(lbs, wide, wide)
        w["t"] = t_dir if dirn == 0 else w["t"] + t_dir
        e_in = pos if reverse else S5_TILE - 1 - pos
        p_mat = jnp.concatenate([bdiag(jnp.swapaxes(wr[e_in], 2, 3).astype(BF16)),
                                 bdiag(jnp.swapaxes(wi[e_in], 2, 3).astype(BF16))], axis=3)
        w["pb" if reverse else "pf"] = p_mat.transpose(1, 0, 2, 3).reshape(lbs, wide, -1)
        e_out = (S5_TILE - pos) if reverse else (pos + 1)
        o_re = jnp.swapaxes(cr * pra[e_out][:, :, None, :] - ci * pia[e_out][:, :, None, :], 2, 3)
        o_im = jnp.swapaxes(-(cr * pia[e_out][:, :, None, :] + ci * pra[e_out][:, :, None, :]), 2, 3)
        o_mat = jnp.concatenate([bdiag(o_re.astype(BF16)), bdiag(o_im.astype(BF16))], axis=2)
        w["ob" if reverse else "of"] = o_mat.transpose(1, 2, 0, 3).reshape(lbs, -1, wide)
        qr, qi = _complex_powers(pr[S5_TILE], pi[S5_TILE], SUBLANES)
        g = abr.shape[0]
        flat = lambda a: a.reshape(g // per, per * a.shape[1])
        t = jnp.arange(SUBLANES)[None, :, None]
        tiles = []
        for dist in (1, 2, 4):
            valid = (t < SUBLANES - dist) if reverse else (t >= dist)
            tiles.append(jnp.where(valid, flat(qr[dist])[:, None, :], 0.0))
            tiles.append(jnp.where(valid, flat(qi[dist])[:, None, :], 0.0))
        order = list(range(SUBLANES, 0, -1)) if reverse else list(range(1, SUBLANES + 1))
        tiles.append(jnp.stack([flat(qr[o]) for o in order], axis=1))
        tiles.append(jnp.stack([flat(qi[o]) for o in order], axis=1))
        out["pw"].append(jnp.stack(tiles, axis=1).astype(F32))
    w["t"] = w["t"].astype(BF16)
    out["w"] = w
    d = glu_w.shape[0]
    out.update(dskip=d_skip.astype(F32)[None, :],
               glu_wa=glu_w[:, :d].astype(BF16), glu_wg=glu_w[:, d:].astype(BF16),
               glu_ba=glu_b[:d].astype(F32)[None, :], glu_bg=glu_b[d:].astype(F32)[None, :])
    return out


def _head_pad_cols(w, per_head, take):
    kdim = w.shape[0]
    w = w.reshape(kdim, MLA_HEADS, per_head)[:, :, take]
    w = jnp.pad(w, ((0, 0), (0, 0), (0, HEAD_PAD - w.shape[2])))
    return w.reshape(kdim, MLA_HEADS * HEAD_PAD)


def _rope_swap(w):
    idx = jnp.arange(MLA_ROPE)
    blk, pos = idx // (2 * ROPE_FREQ), idx % (2 * ROPE_FREQ)
    return w[..., blk * 2 * ROPE_FREQ + (pos + ROPE_FREQ) % (2 * ROPE_FREQ)]


def _mla_params(in_w, q_norm_w, kv_norm_w, qb_w, kvb_w, out_w, seq, ctx_len):
    d = in_w.shape[0]
    qr, kvr = q_norm_w.shape[0], kv_norm_w.shape[0]
    dq = MLA_NOPE + MLA_ROPE
    w_q, w_kv, w_kr = in_w[:, :qr], in_w[:, qr:qr + kvr], in_w[:, qr + kvr:]

    def place(w):
        return jnp.pad(w, ((0, 0), (MLA_NOPE, HEAD_PAD - dq)))

    w_in = jnp.concatenate([w_q, place(w_kr), w_kv, place(_rope_swap(w_kr))], axis=1).astype(BF16)
    qb = qb_w.reshape(qr, MLA_HEADS, dq)
    qa = jnp.pad(qb, ((0, 0), (0, 0), (0, HEAD_PAD - dq))).reshape(qr, -1)
    qb_sw = jnp.pad(_rope_swap(qb[:, :, MLA_NOPE:]), ((0, 0), (0, 0), (MLA_NOPE, HEAD_PAD - dq))).reshape(qr, -1)
    per = MLA_NOPE + MLA_V
    w_k = _head_pad_cols(kvb_w, per, slice(0, MLA_NOPE))
    w_v = _head_pad_cols(kvb_w, per, slice(MLA_NOPE, per))
    w_out = jnp.pad(out_w.reshape(MLA_HEADS, MLA_V, d), ((0, 0), (0, HEAD_PAD - MLA_V), (0, 0))).reshape(-1, d)
    rows = seq // GRID_W
    row = jnp.repeat(jnp.arange(rows, dtype=F32), GRID_W)
    col = jnp.tile(jnp.arange(GRID_W, dtype=F32), rows)
    inv_freq = ROPE_BASE ** (-jnp.arange(ROPE_FREQ, dtype=F32) / ROPE_FREQ)
    ang = jnp.stack([row[:, None] * inv_freq, col[:, None] * inv_freq], axis=1)
    cos, sin = jnp.cos(ang), jnp.sin(ang)
    cos32 = jnp.concatenate([cos, cos], axis=2).reshape(seq, MLA_ROPE)
    sin32 = jnp.concatenate([-sin, sin], axis=2).reshape(seq, MLA_ROPE)
    ones = jnp.ones((seq, MLA_NOPE), F32)
    cos_l = jnp.pad(jnp.concatenate([ones, cos32], axis=1), ((0, 0), (0, HEAD_PAD - dq)))
    sin_l = jnp.pad(sin32, ((0, 0), (MLA_NOPE, HEAD_PAD - dq)))
    cos_c = jnp.pad(jnp.ones((ctx_len, dq), F32), ((0, 0), (0, HEAD_PAD - dq)))
    sin_c = jnp.zeros((ctx_len, HEAD_PAD), F32)
    return dict(q_rank=qr, kv_rank=kvr, w_in=w_in, q_norm=q_norm_w.astype(F32)[None, :],
                kv_norm=kv_norm_w.astype(F32)[None, :], w_qa=qa.astype(BF16), w_qb=qb_sw.astype(BF16),
                w_k=w_k.astype(BF16), w_v=w_v.astype(BF16), w_out=w_out.astype(BF16),
                v_ones=jnp.tile((jnp.arange(HEAD_PAD) == V_ONES_LANE).astype(F32), MLA_HEADS)[None, :],
                tables=dict(lat=(cos_l, sin_l), ctx=(cos_c, sin_c)))


def _modulation(c, c_ctx, ada_w, ada_b):
    nb, d = c.shape
    rows = jnp.zeros((SUBLANES, d), F32).at[:nb].set(c).at[nb].set(c_ctx)
    n = ada_w.shape[1]
    (mod,) = _fused_matmul([(rows, (SUBLANES, d), lambda i, j: (0, 0))], [ada_w.astype(BF16)],
                           [_const_tile_in(ada_b.astype(F32)[None, :], _pick(n, (1024, 512, 256, 128)))],
                           _silu, lambda accs, b: [accs[0] + b], rows=SUBLANES, tm=SUBLANES, k=d, n=n,
                           tn=_pick(n, (1024, 512, 256, 128)), out_dtypes=[F32])
    names = ("sh1", "sc1", "g1", "sh2", "sc2", "g2")
    parts = jnp.split(mod, 6, axis=1)
    lat = {nm: pt[:nb][:, None, :] for nm, pt in zip(names, parts)}
    ctx = {nm: pt[nb:nb + 1][:, None, :] for nm, pt in zip(names, parts)}
    return lat, ctx


def kernel(x, c, ctx, c_ctx, ada_w, ada_b, norm1_w, norm2_w, ffn_w13, ffn_w2, m2_in_w, m2_conv_w, m2_conv_b, m2_dt_bias, m2_a_log, m2_d, m2_norm_w, m2_out_w, lru_in_w, lru_conv_w, lru_conv_b, lru_gate_w, lru_gate_b, lru_a_param, lru_out_w, s5_lambda_re, s5_lambda_im, s5_log_step, s5_b_re, s5_b_im, s5_c_re, s5_c_im, s5_d, s5_glu_w, s5_glu_b, mla_in_w, mla_q_norm_w, mla_kv_norm_w, mla_qb_w, mla_kvb_w, mla_out_w, final_norm_w):
    nb, seq, d = x.shape
    ctx_len = ctx.shape[1]
    depth = ada_w.shape[0]
    hidden = ffn_w2.shape[1]
    xl = x.reshape(nb * seq, d).astype(F32)
    xc = ctx.reshape(nb * ctx_len, d).astype(F32)
    for i in range(depth):
        kind, j = i % N_MIXERS, i // N_MIXERS
        ctx_out = i < depth - 1
        mods_l, mods_c = _modulation(c.astype(F32), c_ctx.astype(F32), ada_w[i], ada_b[i])
        lat_s = _Stream(xl, nb, seq, mods_l)
        ctx_s = _Stream(xc, nb, ctx_len, mods_c)
        n1 = norm1_w[i].astype(F32)[None, :]
        if kind == 0:
            p = _mamba_params(m2_in_w[j], m2_conv_w[j], m2_conv_b[j], m2_dt_bias[j], m2_a_log[j], m2_d[j],
                              m2_norm_w[j], m2_out_w[j])
            p["norm1"] = n1
            outs = _mamba_layer(lat_s, ctx_s, p, ctx_out)
        elif kind == 1:
            p = _rglru_params(lru_in_w[j], lru_conv_w[j], lru_conv_b[j], lru_gate_w[j], lru_gate_b[j],
                              lru_a_param[j], lru_out_w[j])
            p["norm1"] = n1
            outs = _rglru_layer(lat_s, ctx_s, p, ctx_out)
        elif kind == 2:
            p = _s5_params(s5_lambda_re[j], s5_lambda_im[j], s5_log_step[j], s5_b_re[j], s5_b_im[j],
                           s5_c_re[j], s5_c_im[j], s5_d[j], s5_glu_w[j], s5_glu_b[j])
            p["norm1"] = n1
            outs = _s5_layer(lat_s, ctx_s, p, ctx_out)
        else:
            p = _mla_params(mla_in_w[j], mla_q_norm_w[j], mla_kv_norm_w[j], mla_qb_w[j], mla_kvb_w[j],
                            mla_out_w[j], seq, ctx_len)
            p["norm1"] = n1
            outs = _mla_layer(lat_s, ctx_s, p, ctx_out)
        n2 = norm2_w[i].astype(F32)[None, :]
        w1 = ffn_w13[i][:, :hidden].astype(BF16)
        w3 = ffn_w13[i][:, hidden:].astype(BF16)
        w2 = ffn_w2[i].astype(BF16)
        lat_s = _Stream(outs["lat"], nb, seq, mods_l)
        xl = _ffn(lat_s, n2, mods_l["sc2"], mods_l["sh2"], mods_l["g2"], w1, w3, w2)
        if ctx_out:
            ctx_s = _Stream(outs["ctx"], nb, ctx_len, mods_c)
            xc = _ffn(ctx_s, n2, mods_c["sc2"], mods_c["sh2"], mods_c["g2"], w1, w3, w2)
    zero = jnp.zeros((1, 1, d), F32)
    lat_s = _Stream(xl, nb, seq, None)
    out = _norm_mod_rows(xl, final_norm_w.astype(F32)[None, :], zero, zero, tm=lat_s.tm, tpb=lat_s.tpb)
    return out.reshape(nb, seq, d).astype(x.dtype)
```

```python
import functools
import math

import jax
import jax.numpy as jnp
from jax import lax
from jax.experimental import pallas as pl
from jax.experimental.pallas import tpu as pltpu

F32 = jnp.float32
BF16 = jnp.bfloat16

LANES = 128
SUBLANES = 8
VMEM_BUDGET_BYTES = 56 * 2**20
FFN_RESIDENT_WEIGHT_BYTES = 24 * 2**20

GRID_W = 64
N_MIXERS = 4
NORM_EPS = 1e-6
CONV_WIDTH = 4
CONV_PAD_LEFT = 2
M2_HEADDIM = 64
M2_GROUPS = 4
M2_STATE = 128
SSD_CHUNK = 128
LRU_BLOCK = 128
LRU_C = 8.0
S5_GROUP = 16
S5_STATE = 64
S5_TILE = 8
MLA_HEADS = 16
MLA_NOPE = 64
MLA_ROPE = 32
MLA_V = 64
ROPE_FREQ = MLA_ROPE // 4
ROPE_BASE = 10000.0
HEAD_PAD = 128
MAX_FOLD_ROWS = 64
V_ONES_LANE = MLA_V


def _pick(n, cands):
    for c in cands:
        if n % c == 0:
            return c
    raise ValueError(f"no tile in {cands} divides {n}")


def _params(sem, est_bytes):
    limit = int(min(max(2 * est_bytes, 32 * 2**20), VMEM_BUDGET_BYTES))
    return pltpu.CompilerParams(dimension_semantics=sem, vmem_limit_bytes=limit)


def _nbytes(shape, dtype):
    return math.prod(s for s in shape if s is not None) * jnp.dtype(dtype).itemsize


def _sigmoid(x):
    return 0.5 * jnp.tanh(0.5 * x) + 0.5


def _silu(x):
    return x * _sigmoid(x)


def _softplus(x):
    return jnp.maximum(x, 0.0) + jnp.log1p(jnp.exp(-jnp.abs(x)))


def _gelu_tanh(x):
    return 0.5 * x * (1.0 + jnp.tanh(math.sqrt(2.0 / math.pi) * (x + 0.044715 * (x * x * x))))


def _rms(x, w):
    return x * lax.rsqrt(jnp.mean(x * x, axis=-1, keepdims=True) + NORM_EPS) * w


def _norm_mod(x, w, sc, sh):
    return _rms(x, w) * (1.0 + sc) + sh


def _split3(q):
    q1 = q.astype(BF16)
    r1 = q - q1.astype(F32)
    q2 = r1.astype(BF16)
    q3 = (r1 - q2.astype(F32)).astype(BF16)
    return q1, q2, q3


def _select_cols(q, e, parts=3):
    return sum(jnp.dot(p, e, preferred_element_type=F32) for p in _split3(q)[:parts])


def _select_rows(t, q):
    return sum(jnp.dot(t, p, preferred_element_type=F32) for p in _split3(q))


def _fused_matmul(pro_ins, weights, epi_ins, prologue, epilogue, *, rows, tm, k, n, tn, out_dtypes):
    n_p, n_w, n_e, n_o = len(pro_ins), len(weights), len(epi_ins), len(out_dtypes)

    def body(*refs):
        p = refs[:n_p]
        w = refs[n_p:n_p + n_w]
        e = refs[n_p + n_w:n_p + n_w + n_e]
        o = refs[n_p + n_w + n_e:n_p + n_w + n_e + n_o]
        a_scr = refs[-1]

        @pl.when(pl.program_id(1) == 0)
        def _():
            a_scr[...] = prologue(*[r[...] for r in p]).astype(BF16)

        a = a_scr[...]
        accs = [jnp.dot(a, wr[...], preferred_element_type=F32) for wr in w]
        outs = epilogue(accs, *[r[...] for r in e])
        for o_ref, val in zip(o, outs):
            o_ref[...] = val.astype(o_ref.dtype)

    in_specs = ([pl.BlockSpec(bs, im) for (_, bs, im) in pro_ins]
                + [pl.BlockSpec((k, tn), lambda i, j: (0, j)) for _ in weights]
                + [pl.BlockSpec(bs, im) for (_, bs, im) in epi_ins])
    out_specs = [pl.BlockSpec((tm, tn), lambda i, j: (i, j)) for _ in out_dtypes]
    est = (sum(_nbytes(bs, a.dtype) for (a, bs, _) in pro_ins + epi_ins) * 2
           + n_w * k * tn * 2 * 2 + sum(tm * tn * jnp.dtype(d).itemsize for d in out_dtypes) * 2
           + tm * k * 2 + (n_w + 2) * tm * tn * 4 + tm * k * 8)
    outs = pl.pallas_call(
        body,
        out_shape=[jax.ShapeDtypeStruct((rows, n), d) for d in out_dtypes],
        grid=(rows // tm, n // tn),
        in_specs=in_specs,
        out_specs=out_specs,
        scratch_shapes=[pltpu.VMEM((tm, k), BF16)],
        compiler_params=_params(("parallel", "arbitrary"), est),
        name=f"mm_r{rows}_k{k}_n{n}x{n_w}",
    )(*[a for (a, _, _) in pro_ins], *weights, *[a for (a, _, _) in epi_ins])
    return outs


def _rows_in(a, tm, width, col_block=0):
    return (a, (tm, width), lambda i, j: (i, col_block))


def _batch_in(a, tpb, width):
    if a.shape[0] == 1:
        return (a, (None, 1, width), lambda i, j: (0, 0, 0))
    return (a, (None, 1, width), lambda i, j: (i // tpb, 0, 0))


def _const_in(a, width):
    return (a, (1, width), lambda i, j: (0, 0))


def _tile_in(a, tm, tn):
    return (a, (tm, tn), lambda i, j: (i, j))


def _batch_tile_in(a, tpb, tn):
    if a.shape[0] == 1:
        return (a, (None, 1, tn), lambda i, j: (0, 0, j))
    return (a, (None, 1, tn), lambda i, j: (i // tpb, 0, j))


def _const_tile_in(a, tn):
    return (a, (1, tn), lambda i, j: (0, j))


def _norm_mod_rows(x, w, sc, sh, *, tm, tpb):
    rows, d = x.shape

    def body(x_ref, w_ref, sc_ref, sh_ref, o_ref):
        o_ref[...] = _norm_mod(x_ref[...], w_ref[...], sc_ref[...], sh_ref[...])

    def bidx(a):
        if a.shape[0] == 1:
            return lambda i: (0, 0, 0)
        return lambda i: (i // tpb, 0, 0)

    return pl.pallas_call(
        body,
        out_shape=jax.ShapeDtypeStruct((rows, d), F32),
        grid=(rows // tm,),
        in_specs=[pl.BlockSpec((tm, d), lambda i: (i, 0)),
                  pl.BlockSpec((1, d), lambda i: (0, 0)),
                  pl.BlockSpec((None, 1, d), bidx(sc)),
                  pl.BlockSpec((None, 1, d), bidx(sh))],
        out_specs=pl.BlockSpec((tm, d), lambda i: (i, 0)),
        compiler_params=_params(("parallel",), 6 * tm * d * 4),
        name="norm_mod",
    )(x, w, sc, sh)


def _dwconv(x, w, b, *, nb, seq, col_off, channels, act, out_dtype=F32):
    cb = channels // LANES
    rc = _pick(seq, (512, 256, 128))
    pad = SUBLANES

    def body(x_ref, w_ref, b_ref, o_ref, p_scr):
        p_scr[0:pad, :] = jnp.zeros((pad, LANES), F32)
        p_scr[pad + seq:pad + seq + pad, :] = jnp.zeros((pad, LANES), F32)
        p_scr[pad:pad + seq, :] = x_ref[...].astype(F32)
        wv = w_ref[...]
        bv = b_ref[...]

        def step(i, carry):
            r0 = pl.multiple_of(i * rc, rc)
            y = bv
            for t in range(CONV_WIDTH):
                y = y + wv[t:t + 1, :] * p_scr[pl.ds(r0 + pad - CONV_PAD_LEFT + t, rc), :]
            if act:
                y = _silu(y)
            o_ref[pl.ds(r0, rc), :] = y.astype(o_ref.dtype)
            return carry

        lax.fori_loop(0, seq // rc, step, 0)

    return pl.pallas_call(
        body,
        out_shape=jax.ShapeDtypeStruct((nb * seq, channels), out_dtype),
        grid=(nb, cb),
        in_specs=[pl.BlockSpec((seq, LANES), lambda bi, ci: (bi, col_off + ci)),
                  pl.BlockSpec((CONV_WIDTH, LANES), lambda bi, ci: (0, ci)),
                  pl.BlockSpec((1, LANES), lambda bi, ci: (0, ci))],
        out_specs=pl.BlockSpec((seq, LANES), lambda bi, ci: (bi, ci)),
        scratch_shapes=[pltpu.VMEM((seq + 2 * pad, LANES), F32)],
        compiler_params=_params(("parallel", "parallel"), 5 * seq * LANES * 4),
        name="dwconv",
    )(x, w, b)


def _ssd_direction(xbc, dt, par, dskip, h0, yprev, consts, *, nb, seq, reverse):
    tri, e_head = consts
    q = SSD_CHUNK
    nc = seq // q
    inner = xbc.shape[1] - 2 * M2_GROUPS * M2_STATE
    gw = inner // M2_GROUPS
    heads = inner // M2_HEADDIM
    hpg = heads // M2_GROUPS
    has_prev = yprev is not None
    has_skip = dskip is not None

    def body(*refs):
        it = iter(refs)
        xbc_ref, dt_ref, par_ref = next(it), next(it), next(it)
        dsk_ref = next(it) if has_skip else None
        h0_ref = next(it)
        yp_ref = next(it) if has_prev else None
        tri_ref, eh_ref = next(it), next(it)
        y_ref, hout_ref, h_scr = next(it), next(it), next(it)
        c = pl.program_id(1)

        @pl.when(c == 0)
        def _():
            h_scr[...] = h0_ref[...]

        dtv = _softplus(dt_ref[...] + par_ref[0:1, :])
        da = dtv * (-jnp.exp(par_ref[1:2, :]))
        acs = _select_rows(tri_ref[...], da)
        acs_t = acs.T
        dt_t = dtv.T
        last = acs[0:1, :] if reverse else acs[q - 1:q, :]
        eh = eh_ref[...]
        w_e = _select_cols(dtv * jnp.exp(last - acs), eh, parts=2)
        cdec_e = _select_cols(jnp.broadcast_to(jnp.exp(last), (SUBLANES, LANES)), eh)[0:1, :]
        x_b = xbc_ref[:, 0:inner].astype(BF16)
        xs = xbc_ref[:, 0:inner].astype(F32)
        x_d = (xs * w_e).astype(BF16)
        row = lax.broadcasted_iota(jnp.int32, (q, q), 0)
        col = lax.broadcasted_iota(jnp.int32, (q, q), 1)
        mask = (col >= row) if reverse else (col <= row)
        lane = lax.broadcasted_iota(jnp.int32, (q, LANES), 1)
        for g in range(M2_GROUPS):
            bm = xbc_ref[:, inner + g * M2_STATE:inner + (g + 1) * M2_STATE]
            cm = xbc_ref[:, inner + (M2_GROUPS + g) * M2_STATE:inner + (M2_GROUPS + g + 1) * M2_STATE]
            scores = lax.dot_general(cm.astype(BF16), bm.astype(BF16), (((1,), (1,)), ((), ())),
                                     preferred_element_type=F32)
            h_t = h_scr[g]
            h_b = h_t.astype(BF16)
            s_t = jnp.dot(bm.astype(F32).T.astype(BF16), x_d[:, g * gw:(g + 1) * gw], preferred_element_type=F32)
            h_scr[g] = h_t * cdec_e[:, g * gw:(g + 1) * gw] + s_t
            for pr in range(hpg // 2):
                k0 = g * hpg + 2 * pr
                c0 = k0 * M2_HEADDIM
                rhs = jnp.concatenate([x_b[:, c0:c0 + LANES], h_b[:, 2 * pr * M2_HEADDIM:2 * pr * M2_HEADDIM + LANES]],
                                      axis=0)
                ys = []
                for k in (k0, k0 + 1):
                    a_col = jnp.broadcast_to(acs_t[k:k + 1, :], (q, q)).T
                    decay = jnp.exp(jnp.where(mask, a_col - acs_t[k:k + 1, :], -jnp.inf))
                    m = (scores * decay * dt_t[k:k + 1, :]).astype(BF16)
                    c_e = (cm.astype(F32) * jnp.exp(a_col)).astype(BF16)
                    ys.append(jnp.dot(jnp.concatenate([m, c_e], axis=1), rhs, preferred_element_type=F32))
                y = jnp.where(lane < M2_HEADDIM, ys[0], ys[1])
                if has_skip:
                    y = y + dsk_ref[:, c0:c0 + LANES] * xs[:, c0:c0 + LANES]
                if has_prev:
                    y = y + yp_ref[:, c0:c0 + LANES]
                y_ref[:, c0:c0 + LANES] = y

        @pl.when(c == nc - 1)
        def _():
            hout_ref[...] = h_scr[...]

    def rb(bi, ci):
        return bi * nc + ((nc - 1 - ci) if reverse else ci)

    width = xbc.shape[1]
    st_shape = (M2_GROUPS, M2_STATE, gw)
    ins = [xbc, dt, par]
    in_specs = [pl.BlockSpec((q, width), lambda bi, ci: (rb(bi, ci), 0)),
                pl.BlockSpec((q, LANES), lambda bi, ci: (rb(bi, ci), 0)),
                pl.BlockSpec((SUBLANES, LANES), lambda bi, ci: (0, 0))]
    if has_skip:
        ins.append(dskip)
        in_specs.append(pl.BlockSpec((1, inner), lambda bi, ci: (0, 0)))
    ins.append(h0)
    in_specs.append(pl.BlockSpec((None,) + st_shape, lambda bi, ci: (bi, 0, 0, 0)))
    if has_prev:
        ins.append(yprev)
        in_specs.append(pl.BlockSpec((q, inner), lambda bi, ci: (rb(bi, ci), 0)))
    ins += [tri, e_head]
    in_specs += [pl.BlockSpec(tri.shape, lambda bi, ci: (0, 0)),
                 pl.BlockSpec(e_head.shape, lambda bi, ci: (0, 0))]
    est = (q * width * 4 * 2 + q * inner * 4 * 4 + 3 * math.prod(st_shape) * 4 * 2
           + e_head.size * 2 * 2 + 12 * q * inner * 4 + q * heads * LANES * 4)
    y, h_last = pl.pallas_call(
        body,
        out_shape=[jax.ShapeDtypeStruct((nb * seq, inner), F32),
                   jax.ShapeDtypeStruct((nb,) + st_shape, F32)],
        grid=(nb, nc),
        in_specs=in_specs,
        out_specs=[pl.BlockSpec((q, inner), lambda bi, ci: (rb(bi, ci), 0)),
                   pl.BlockSpec((None,) + st_shape, lambda bi, ci: (bi, 0, 0, 0))],
        scratch_shapes=[pltpu.VMEM(st_shape, F32)],
        compiler_params=_params(("parallel", "arbitrary"), est),
        name="ssd_bwd" if reverse else "ssd_fwd",
    )(*ins)
    return y, h_last


def _ssd_consts(inner, reverse):
    heads = inner // M2_HEADDIM
    r = jnp.arange(SSD_CHUNK)
    tri = (r[None, :] >= r[:, None]) if reverse else (r[None, :] <= r[:, None])
    hk = jnp.arange(LANES)[:, None]
    e_head = (hk == (jnp.arange(inner)[None, :] // M2_HEADDIM))
    return tri.astype(BF16), e_head.astype(BF16)


def _tile_scan(a, u, reverse):
    row = lax.broadcasted_iota(jnp.int32, a.shape, 0)
    for d in (1, 2, 4):
        shift = (SUBLANES - d) if reverse else d
        valid = (row < SUBLANES - d) if reverse else (row >= d)
        a_s = jnp.where(valid, pltpu.roll(a, shift, 0), 1.0)
        u_s = jnp.where(valid, pltpu.roll(u, shift, 0), 0.0)
        u = a * u_s + u
        a = a * a_s
    return a, u


def _rglru_direction(xc, gate_w, gate_b, a_param, h0, rprev, *, nb, seq, reverse):
    width = xc.shape[1]
    nblk = width // LRU_BLOCK
    tb = _pick(seq, (512, 256, 128))
    nt = seq // tb
    has_prev = rprev is not None

    def body(*refs):
        it = iter(refs)
        x_ref, gw_ref, gb_ref, ap_ref, h0_ref = next(it), next(it), next(it), next(it), next(it)
        rp_ref = next(it) if has_prev else None
        r_ref, hout_ref, a_scr, u_scr, h_scr = next(it), next(it), next(it), next(it), next(it)
        tstep = pl.program_id(1)

        @pl.when(tstep == 0)
        def _():
            h_scr[...] = jnp.broadcast_to(h0_ref[...], (SUBLANES, width))

        log_base = -LRU_C * _softplus(-ap_ref[...])
        for nbk in range(nblk):
            cs = slice(nbk * LRU_BLOCK, (nbk + 1) * LRU_BLOCK)
            xb = x_ref[:, cs]
            g = jnp.dot(xb.astype(BF16), gw_ref[nbk], preferred_element_type=F32) + gb_ref[nbk]
            g = _sigmoid(g)
            log_a = g[:, :LRU_BLOCK] * log_base[:, cs]
            a = jnp.exp(log_a)
            gap = jnp.maximum(1.0 - a * a, 0.0)
            mult = jnp.where(gap > 0.0, gap * lax.rsqrt(gap), 0.0)
            a_scr[:, cs] = a
            u_scr[:, cs] = xb * g[:, LRU_BLOCK:] * mult

        def step(i, h):
            grp = (tb // SUBLANES - 1 - i) if reverse else i
            r0 = pl.multiple_of(grp * SUBLANES, SUBLANES)
            a_c, h_loc = _tile_scan(a_scr[pl.ds(r0, SUBLANES), :], u_scr[pl.ds(r0, SUBLANES), :], reverse)
            h_new = a_c * h + h_loc
            out = h_new
            if has_prev:
                out = out + rp_ref[pl.ds(r0, SUBLANES), :]
            r_ref[pl.ds(r0, SUBLANES), :] = out
            edge = h_new[0:1, :] if reverse else h_new[SUBLANES - 1:SUBLANES, :]
            return jnp.broadcast_to(edge, (SUBLANES, width))

        h_fin = lax.fori_loop(0, tb // SUBLANES, step, h_scr[...])
        h_scr[...] = h_fin

        @pl.when(tstep == nt - 1)
        def _():
            hout_ref[...] = h_fin[0:1, :]

    def rb(bi, ti):
        return bi * nt + ((nt - 1 - ti) if reverse else ti)

    ins = [xc, gate_w, gate_b, a_param, h0]
    in_specs = [pl.BlockSpec((tb, width), lambda bi, ti: (rb(bi, ti), 0)),
                pl.BlockSpec(gate_w.shape, lambda bi, ti: (0, 0, 0)),
                pl.BlockSpec(gate_b.shape, lambda bi, ti: (0, 0, 0)),
                pl.BlockSpec((1, width), lambda bi, ti: (0, 0)),
                pl.BlockSpec((None, 1, width), lambda bi, ti: (bi, 0, 0))]
    if has_prev:
        ins.append(rprev)
        in_specs.append(pl.BlockSpec((tb, width), lambda bi, ti: (rb(bi, ti), 0)))
    est = 10 * tb * width * 4 + gate_w.size * 2 * 2
    r, h_last = pl.pallas_call(
        body,
        out_shape=[jax.ShapeDtypeStruct((nb * seq, width), F32),
                   jax.ShapeDtypeStruct((nb, 1, width), F32)],
        grid=(nb, nt),
        in_specs=in_specs,
        out_specs=[pl.BlockSpec((tb, width), lambda bi, ti: (rb(bi, ti), 0)),
                   pl.BlockSpec((None, 1, width), lambda bi, ti: (bi, 0, 0))],
        scratch_shapes=[pltpu.VMEM((tb, width), F32), pltpu.VMEM((tb, width), F32),
                        pltpu.VMEM((SUBLANES, width), F32)],
        compiler_params=_params(("parallel", "arbitrary"), est),
        name="rglru_bwd" if reverse else "rglru_fwd",
    )(*ins)
    return r, h_last


def _s5_bidir(u, w, pw_f, pw_b, h_in, *, nb, seq):
    d_model = u.shape[1]
    lbs = d_model // LANES
    sw = pw_f.shape[3]
    m_t = seq // S5_TILE
    wide = S5_TILE * LANES

    def scan(s_scr, pw_ref, c_re, c_im, reverse):
        row = lax.broadcasted_iota(jnp.int32, (SUBLANES, sw), 0)

        def step(i, carry):
            hr, hi = carry
            grp = (m_t // SUBLANES - 1 - i) if reverse else i
            r0 = pl.multiple_of(grp * SUBLANES, SUBLANES)
            xr = s_scr[pl.ds(r0, SUBLANES), 0:sw]
            xi = s_scr[pl.ds(r0, SUBLANES), sw:2 * sw]
            for n_step, dist in enumerate((1, 2, 4)):
                shift = (SUBLANES - dist) if reverse else dist
                pr = pw_ref[2 * n_step]
                pi = pw_ref[2 * n_step + 1]
                sr = pltpu.roll(xr, shift, 0)
                si = pltpu.roll(xi, shift, 0)
                xr, xi = xr + (pr * sr - pi * si), xi + (pr * si + pi * sr)
            nr = xr + (pw_ref[6] * hr - pw_ref[7] * hi)
            ni = xi + (pw_ref[6] * hi + pw_ref[7] * hr)
            edge = (SUBLANES - 1) if reverse else 0
            back = 1 if not reverse else SUBLANES - 1
            s_scr[pl.ds(r0, SUBLANES), 0:sw] = jnp.where(row == edge, hr, pltpu.roll(nr, back, 0))
            s_scr[pl.ds(r0, SUBLANES), sw:2 * sw] = jnp.where(row == edge, hi, pltpu.roll(ni, back, 0))
            e = 0 if reverse else SUBLANES - 1
            return (jnp.broadcast_to(nr[e:e + 1, :], (SUBLANES, sw)),
                    jnp.broadcast_to(ni[e:e + 1, :], (SUBLANES, sw)))

        init = (jnp.broadcast_to(c_re, (SUBLANES, sw)), jnp.broadcast_to(c_im, (SUBLANES, sw)))
        return lax.fori_loop(0, m_t // SUBLANES, step, init)

    def body(u_ref, t_ref, pf_ref, pb_ref, of_ref, ob_ref, pwf_ref, pwb_ref, hin_ref,
             y_ref, hout_ref, x_scr, f_scr, g_scr):
        for l in range(S5_TILE):
            x_scr[:, l * LANES:(l + 1) * LANES] = u_ref[pl.ds(l, m_t, stride=S5_TILE), :].astype(BF16)
        x = x_scr[...]
        f_scr[...] = jnp.dot(x, pf_ref[...], preferred_element_type=F32)
        g_scr[...] = jnp.dot(x, pb_ref[...], preferred_element_type=F32)
        fr, fi = scan(f_scr, pwf_ref, hin_ref[0:1, :], hin_ref[1:2, :], False)
        gr, gi = scan(g_scr, pwb_ref, hin_ref[2:3, :], hin_ref[3:4, :], True)
        hout_ref[0:1, :] = fr[0:1, :]
        hout_ref[1:2, :] = fi[0:1, :]
        hout_ref[2:3, :] = gr[0:1, :]
        hout_ref[3:4, :] = gi[0:1, :]
        y = (jnp.dot(x, t_ref[...], preferred_element_type=F32)
             + jnp.dot(f_scr[...].astype(BF16), of_ref[...], preferred_element_type=F32)
             + jnp.dot(g_scr[...].astype(BF16), ob_ref[...], preferred_element_type=F32))
        for l in range(S5_TILE):
            y_ref[pl.ds(l, m_t, stride=S5_TILE), :] = y[:, l * LANES:(l + 1) * LANES]

    def wspec():
        return pl.BlockSpec((None, wide, wide), lambda li, bi: (li, 0, 0), pipeline_mode=pl.Buffered(1))

    est = (4 * seq * LANES * 4 + 5 * wide * wide * 2 + m_t * wide * 2 + 2 * m_t * 2 * sw * 4
           + 3 * m_t * wide * 4)
    y, h_out = pl.pallas_call(
        body,
        out_shape=[jax.ShapeDtypeStruct((nb * seq, d_model), F32),
                   jax.ShapeDtypeStruct((nb, lbs, 4, sw), F32)],
        grid=(lbs, nb),
        in_specs=[pl.BlockSpec((seq, LANES), lambda li, bi: (bi, li)),
                  wspec(), wspec(), wspec(), wspec(), wspec(),
                  pl.BlockSpec((None, 8, SUBLANES, sw), lambda li, bi: (li, 0, 0, 0)),
                  pl.BlockSpec((None, 8, SUBLANES, sw), lambda li, bi: (li, 0, 0, 0)),
                  pl.BlockSpec((None, None, 4, sw), lambda li, bi: (bi, li, 0, 0))],
        out_specs=[pl.BlockSpec((seq, LANES), lambda li, bi: (bi, li)),
                   pl.BlockSpec((None, None, 4, sw), lambda li, bi: (bi, li, 0, 0))],
        scratch_shapes=[pltpu.VMEM((m_t, wide), BF16), pltpu.VMEM((m_t, 2 * sw), F32),
                        pltpu.VMEM((m_t, 2 * sw), F32)],
        compiler_params=_params(("parallel", "parallel"), est),
        name="s5_bidir",
    )(u, w["t"], w["pf"], w["pb"], w["of"], w["ob"], pw_f, pw_b, h_in)
    return y, h_out


def _attention(q, kv_parts, *, nb, lq, scale):
    heads = q.shape[1] // HEAD_PAD
    tq = _pick(lq, (2048, 1024, 512, 256, 128))
    tsub = _pick(tq, (256, 128))
    nq = lq // tq
    c = scale * math.log2(math.e)
    chunks = [_pick(length, (2048, 1024, 512, 256, 128)) for (_, _, length) in kv_parts]
    n_parts = len(kv_parts)

    def body(*refs):
        q_ref = refs[0]
        kv_refs = refs[1:1 + 2 * n_parts]
        o_ref = refs[1 + 2 * n_parts]
        s_a, s_b = refs[2 + 2 * n_parts], refs[3 + 2 * n_parts]

        def q_tile(t, carry):
            r0 = pl.multiple_of(t * tsub, tsub)
            qt = q_ref[pl.ds(r0, tsub), :]

            def key_parts(n):
                parts = 4 if n >= 2048 else 2 if n >= 2 * MAX_FOLD_ROWS else 1
                return tuple((i * n // parts, (i + 1) * n // parts) for i in range(parts))

            def scores(k):
                return jnp.concatenate(
                    [lax.dot_general(k[a:b], qt, (((1,), (1,)), ((), ())), preferred_element_type=F32)
                     for a, b in key_parts(k.shape[0])], axis=0)

            def absorb(s_t, v_t, m, acc):
                part = jnp.max(s_t.reshape(-1, MAX_FOLD_ROWS, tsub), axis=0)
                m_new = jnp.maximum(m, jnp.max(part, axis=0, keepdims=True))
                p_t = jnp.exp2((s_t - m_new) * c).astype(BF16)
                alpha = jnp.exp2((m - m_new) * c)
                acc = alpha * acc
                for a, b in key_parts(s_t.shape[0]):
                    acc = acc + jnp.dot(v_t[:, a:b], p_t[a:b], preferred_element_type=F32)
                return m_new, acc

            m = jnp.full((1, tsub), -jnp.inf, F32)
            acc = jnp.zeros((HEAD_PAD, tsub), F32)
            sched = [(kv_refs[2 * pi], kv_refs[2 * pi + 1], ci, chunks[pi])
                     for pi, (_, _, length) in enumerate(kv_parts) for ci in range(length // chunks[pi])]
            bufs = (s_a, s_b)

            def issue(i):
                k_ref, _, ci, tk = sched[i]
                bufs[i % 2][0:tk, :] = scores(k_ref[ci * tk:(ci + 1) * tk, :])

            issue(0)
            for i, (_, v_ref, ci, tk) in enumerate(sched):
                if i + 1 < len(sched):
                    issue(i + 1)
                m, acc = absorb(bufs[i % 2][0:tk, :], v_ref[ci], m, acc)
            o_t = acc / acc[V_ONES_LANE:V_ONES_LANE + 1, :]
            o_ref[pl.ds(r0, tsub), :] = o_t.T.astype(o_ref.dtype)
            return carry

        lax.fori_loop(0, tq // tsub, q_tile, 0)

    ins = [q]
    in_specs = [pl.BlockSpec((tq, HEAD_PAD), lambda b, h, qi: (b * nq + qi, h))]
    est = tq * HEAD_PAD * 2 * 4 + 6 * tsub * max(chunks) * 4
    for (k, v, length), tk in zip(kv_parts, chunks):
        n_ch = length // tk
        v_t = v.reshape(nb, n_ch, tk, heads, HEAD_PAD).transpose(0, 3, 1, 4, 2)
        ins += [k, v_t]
        in_specs += [pl.BlockSpec((length, HEAD_PAD), lambda b, h, qi: (b, h)),
                     pl.BlockSpec((None, None, n_ch, HEAD_PAD, tk), lambda b, h, qi: (b, h, 0, 0, 0))]
        est += 2 * length * HEAD_PAD * 2 * 2
    return pl.pallas_call(
        body,
        out_shape=jax.ShapeDtypeStruct(q.shape, BF16),
        grid=(nb, heads, nq),
        in_specs=in_specs,
        out_specs=pl.BlockSpec((tq, HEAD_PAD), lambda b, h, qi: (b * nq + qi, h)),
        scratch_shapes=[pltpu.VMEM((max(chunks), tsub), F32), pltpu.VMEM((max(chunks), tsub), F32)],
        compiler_params=_params(("parallel", "parallel", "arbitrary"), est),
        name="attention",
    )(*ins)


class _Stream:
    def __init__(self, x, nb, seq, mods):
        self.x = x
        self.nb = nb
        self.seq = seq
        self.tm = _pick(seq, (512, 256, 128))
        self.tpb = seq // self.tm
        self.mods = mods
        self.rows = nb * seq


def _in_proj(st, norm_w, sc, sh, weights, n, tn, out_dtypes=None):
    d = st.x.shape[1]
    pro = [_rows_in(st.x, st.tm, d), _const_in(norm_w, d), _batch_in(sc, st.tpb, d), _batch_in(sh, st.tpb, d)]
    return _fused_matmul(pro, weights, [], _norm_mod, lambda accs: accs, rows=st.rows, tm=st.tm, k=d, n=n,
                         tn=tn, out_dtypes=out_dtypes or [F32] * len(weights))


def _out_proj(st, pro_ins, prologue, weight, gate, k):
    d = st.x.shape[1]
    epi = [_tile_in(st.x, st.tm, d), _batch_tile_in(gate, st.tpb, d)]
    (out,) = _fused_matmul(pro_ins, [weight], epi, prologue, lambda accs, x, g: [x + g * accs[0]],
                           rows=st.rows, tm=st.tm, k=k, n=d, tn=d, out_dtypes=[F32])
    return out


def _ffn(st, norm_w, sc, sh, gate, w1, w3, w2):
    d = st.x.shape[1]
    hidden = w1.shape[1]
    resident = 3 * d * hidden * 2 <= FFN_RESIDENT_WEIGHT_BYTES
    th = hidden if resident else _pick(hidden, (1408, 1024, 512, 256, 128))
    wmode = dict(pipeline_mode=pl.Buffered(1)) if resident else {}
    tm, tpb = st.tm, st.tpb
    nh = hidden // th

    def body(x_ref, nw_ref, sc_ref, sh_ref, w1_ref, w3_ref, w2_ref, g_ref, o_ref, a_scr, acc_scr):
        j = pl.program_id(1)

        @pl.when(j == 0)
        def _():
            a_scr[...] = _norm_mod(x_ref[...], nw_ref[...], sc_ref[...], sh_ref[...]).astype(BF16)
            acc_scr[...] = jnp.zeros((tm, d), F32)

        a = a_scr[...]
        up = (_silu(jnp.dot(a, w1_ref[...], preferred_element_type=F32))
              * jnp.dot(a, w3_ref[...], preferred_element_type=F32))
        acc_scr[...] += jnp.dot(up.astype(BF16), w2_ref[...], preferred_element_type=F32)

        @pl.when(j == nh - 1)
        def _():
            o_ref[...] = x_ref[...] + g_ref[...] * acc_scr[...]

    def bidx(a):
        if a.shape[0] == 1:
            return lambda i, j: (0, 0, 0)
        return lambda i, j: (i // tpb, 0, 0)

    est = 4 * tm * d * 4 + 3 * d * th * 2 * (1 if resident else 2) + tm * d * 6 + 4 * tm * th * 4
    return pl.pallas_call(
        body,
        out_shape=jax.ShapeDtypeStruct((st.rows, d), F32),
        grid=(st.rows // tm, nh),
        in_specs=[pl.BlockSpec((tm, d), lambda i, j: (i, 0)),
                  pl.BlockSpec((1, d), lambda i, j: (0, 0)),
                  pl.BlockSpec((None, 1, d), bidx(sc)),
                  pl.BlockSpec((None, 1, d), bidx(sh)),
                  pl.BlockSpec((d, th), lambda i, j: (0, j), **wmode),
                  pl.BlockSpec((d, th), lambda i, j: (0, j), **wmode),
                  pl.BlockSpec((th, d), lambda i, j: (j, 0), **wmode),
                  pl.BlockSpec((None, 1, d), bidx(gate))],
        out_specs=pl.BlockSpec((tm, d), lambda i, j: (i, 0)),
        scratch_shapes=[pltpu.VMEM((tm, d), BF16), pltpu.VMEM((tm, d), F32)],
        compiler_params=_params(("parallel", "arbitrary"), est),
        name="ffn",
    )(st.x, norm_w, sc, sh, w1, w3, w2, gate)


def _mamba_layer(lat, ctx, p, ctx_out):
    inner = p["inner"]
    conv_ch = p["conv_ch"]
    heads = inner // M2_HEADDIM
    gw = inner // M2_GROUPS
    outs = {}
    acts = {}
    for name, st in (("ctx", ctx), ("lat", lat)):
        m = st.mods
        (zx,) = _in_proj(st, p["norm1"], m["sc1"], m["sh1"], [p["w_zx"]], inner + conv_ch,
                         _pick(inner + conv_ch, (2560, 1024, 512, 256, 128)), out_dtypes=[BF16])
        dt0, dt1 = _in_proj(st, p["norm1"], m["sc1"], m["sh1"], [p["w_dt0"], p["w_dt1"]], LANES, LANES)
        xbc = _dwconv(zx, p["conv_w"], p["conv_b"], nb=st.nb, seq=st.seq, col_off=inner // LANES,
                      channels=conv_ch, act=True, out_dtype=BF16)
        acts[name] = (zx, (dt0, dt1), xbc)
    ys = {}
    for dirn, reverse in ((0, False), (1, True)):
        consts = _ssd_consts(inner, reverse)
        h = jnp.zeros((ctx.nb, M2_GROUPS, M2_STATE, gw), F32)
        for name, st in (("ctx", ctx), ("lat", lat)):
            zx, dts, xbc = acts[name]
            y, h = _ssd_direction(xbc, dts[dirn], p["par"][dirn], p["dskip"] if dirn == 0 else None, h,
                                  ys.get(name), consts, nb=st.nb, seq=st.seq, reverse=reverse)
            ys[name] = y

    def gated_norm(y, z, w):
        return _rms(y * _silu(z.astype(F32)), w)

    for name, st in (("ctx", ctx), ("lat", lat)):
        if name == "ctx" and not ctx_out:
            continue
        zx = acts[name][0]
        pro = [_rows_in(ys[name], st.tm, inner), _rows_in(zx, st.tm, inner), _const_in(p["norm_w"], inner)]
        outs[name] = _out_proj(st, pro, gated_norm, p["w_out"], st.mods["g1"], inner)
    return outs


def _rglru_layer(lat, ctx, p, ctx_out):
    width = p["width"]
    outs = {}
    acts = {}
    for name, st in (("ctx", ctx), ("lat", lat)):
        m = st.mods
        (yx,) = _in_proj(st, p["norm1"], m["sc1"], m["sh1"], [p["w_in"]], 2 * width,
                         _pick(2 * width, (2560, 1280, 1024, 512, 256, 128)), out_dtypes=[BF16])
        xc = _dwconv(yx, p["conv_w"], p["conv_b"], nb=st.nb, seq=st.seq, col_off=width // LANES,
                     channels=width, act=False)
        acts[name] = (yx, xc)
    rs = {}
    for dirn, reverse in ((0, False), (1, True)):
        h = jnp.zeros((ctx.nb, 1, width), F32)
        for name, st in (("ctx", ctx), ("lat", lat)):
            r, h = _rglru_direction(acts[name][1], p["gate_w"][dirn], p["gate_b"][dirn], p["a_param"][dirn], h,
                                    rs.get(name), nb=st.nb, seq=st.seq, reverse=reverse)
            rs[name] = r
    for name, st in (("ctx", ctx), ("lat", lat)):
        if name == "ctx" and not ctx_out:
            continue
        pro = [_rows_in(acts[name][0], st.tm, width), _rows_in(rs[name], st.tm, width)]
        outs[name] = _out_proj(st, pro, lambda y, r: _gelu_tanh(y.astype(F32)) * r, p["w_out"], st.mods["g1"], width)
    return outs


def _s5_layer(lat, ctx, p, ctx_out):
    d = lat.x.shape[1]
    lbs = d // LANES
    sw = p["pw"][0].shape[3]
    us = {}
    for name, st in (("ctx", ctx), ("lat", lat)):
        m = st.mods
        us[name] = _norm_mod_rows(st.x, p["norm1"], m["sc1"], m["sh1"], tm=st.tm, tpb=st.tpb)
    ys = {}
    h = jnp.zeros((ctx.nb, lbs, 4, sw), F32)
    for name, st in (("ctx", ctx), ("lat", lat)):
        ys[name], h = _s5_bidir(us[name], p["w"], p["pw"][0], p["pw"][1], h, nb=st.nb, seq=st.seq)
    outs = {}
    for name, st in (("ctx", ctx), ("lat", lat)):
        if name == "ctx" and not ctx_out:
            continue
        pro = [_rows_in(ys[name], st.tm, d), _rows_in(us[name], st.tm, d), _const_in(p["dskip"], d)]
        epi = [_const_tile_in(p["glu_ba"], d), _const_tile_in(p["glu_bg"], d),
               _tile_in(st.x, st.tm, d), _batch_tile_in(st.mods["g1"], st.tpb, d)]
        (out,) = _fused_matmul(
            pro, [p["glu_wa"], p["glu_wg"]], epi,
            lambda y, u, dsk: _gelu_tanh(y + dsk * u),
            lambda accs, ba, bg, x, g: [x + g * ((accs[0] + ba) * _sigmoid(accs[1] + bg))],
            rows=st.rows, tm=st.tm, k=d, n=d, tn=d, out_dtypes=[F32])
        outs[name] = out
    return outs


def _mla_layer(lat, ctx, p, ctx_out):
    d = lat.x.shape[1]
    qr, kvr = p["q_rank"], p["kv_rank"]
    hp = MLA_HEADS * HEAD_PAD
    tn_h = _pick(hp, (1024, 512, 256, 128))
    reps = tn_h // HEAD_PAD
    qkv = {}
    for name, st in (("ctx", ctx), ("lat", lat)):
        m = st.mods
        n_in = p["w_in"].shape[1]
        (lat_all,) = _in_proj(st, p["norm1"], m["sc1"], m["sh1"], [p["w_in"]], n_in, n_in)
        cos_t, sin_t = p["tables"][name]
        tab_spec = lambda a, st=st: (a, (st.tm, HEAD_PAD), lambda i, j: (i % st.tpb, 0))

        def rope_epi(accs, cos, sin, reps=reps):
            cos_r = jnp.concatenate([cos] * reps, axis=1)
            sin_r = jnp.concatenate([sin] * reps, axis=1)
            return [accs[0] * cos_r + accs[1] * sin_r]

        want_q = name == "lat" or ctx_out
        q = None
        if want_q:
            (q,) = _fused_matmul([_rows_in(lat_all, st.tm, qr, 0), _const_in(p["q_norm"], qr)],
                                 [p["w_qa"], p["w_qb"]], [tab_spec(cos_t), tab_spec(sin_t)],
                                 _rms, rope_epi, rows=st.rows, tm=st.tm, k=qr, n=hp, tn=tn_h, out_dtypes=[BF16])

        def kv_epi(accs, kra, krb, cos, sin, ones, reps=reps):
            kr = kra * cos + krb * sin
            return [accs[0] + jnp.concatenate([kr] * reps, axis=1), accs[1] + ones]

        kra_blk = (qr // LANES)
        kv_blk = (qr + LANES) // kvr
        krb_blk = (qr + LANES + kvr) // LANES
        k, v = _fused_matmul(
            [_rows_in(lat_all, st.tm, kvr, kv_blk), _const_in(p["kv_norm"], kvr)],
            [p["w_k"], p["w_v"]],
            [(lat_all, (st.tm, LANES), lambda i, j: (i, kra_blk)),
             (lat_all, (st.tm, LANES), lambda i, j: (i, krb_blk)),
             tab_spec(cos_t), tab_spec(sin_t), _const_tile_in(p["v_ones"], tn_h)],
            _rms, kv_epi, rows=st.rows, tm=st.tm, k=kvr, n=hp, tn=tn_h, out_dtypes=[BF16, BF16])
        qkv[name] = (q, k, v)
    scale = (MLA_NOPE + MLA_ROPE) ** -0.5
    nb = lat.nb
    kv_c = (qkv["ctx"][1], qkv["ctx"][2], ctx.seq)
    kv_l = (qkv["lat"][1], qkv["lat"][2], lat.seq)
    outs = {}
    o_l = _attention(qkv["lat"][0], [kv_c, kv_l], nb=nb, lq=lat.seq, scale=scale)
    outs["lat"] = _out_proj(lat, [_rows_in(o_l, lat.tm, hp)], lambda a: a, p["w_out"], lat.mods["g1"], hp)
    if ctx_out:
        o_c = _attention(qkv["ctx"][0], [kv_c], nb=nb, lq=ctx.seq, scale=scale)
        outs["ctx"] = _out_proj(ctx, [_rows_in(o_c, ctx.tm, hp)], lambda a: a, p["w_out"], ctx.mods["g1"], hp)
    return outs


def _pad_cols(w, n):
    return jnp.pad(w, ((0, 0), (0, n - w.shape[1])))


def _mamba_params(in_w, conv_w, conv_b, dt_bias, a_log, d_skip, norm_w, out_w):
    d, proj = in_w.shape
    heads = dt_bias.shape[1]
    inner = heads * M2_HEADDIM
    conv_ch = inner + 2 * M2_GROUPS * M2_STATE
    par = []
    for dirn in range(2):
        rows = jnp.zeros((SUBLANES, LANES), F32)
        rows = rows.at[0, :heads].set(dt_bias[dirn].astype(F32)).at[1, :heads].set(a_log[dirn].astype(F32))
        par.append(rows)
    off = inner + conv_ch
    return dict(
        inner=inner, conv_ch=conv_ch,
        w_zx=in_w[:, :off].astype(BF16),
        w_dt0=_pad_cols(in_w[:, off:off + heads], LANES).astype(BF16),
        w_dt1=_pad_cols(in_w[:, off + heads:off + 2 * heads], LANES).astype(BF16),
        conv_w=conv_w.astype(F32), conv_b=conv_b.astype(F32)[None, :],
        par=par, dskip=jnp.repeat(d_skip.astype(F32), M2_HEADDIM)[None, :],
        norm_w=norm_w.astype(F32)[None, :], w_out=out_w.astype(BF16))


def _rglru_params(in_w, conv_w, conv_b, gate_w, gate_b, a_param, out_w):
    width = conv_w.shape[1]
    return dict(
        width=width, w_in=in_w.astype(BF16), conv_w=conv_w.astype(F32), conv_b=conv_b.astype(F32)[None, :],
        gate_w=[gate_w[dirn].astype(BF16) for dirn in range(2)],
        gate_b=[gate_b[dirn].astype(F32)[:, None, :] for dirn in range(2)],
        a_param=[a_param[dirn].astype(F32)[None, :] for dirn in range(2)],
        w_out=out_w.astype(BF16))


def _block_diag(m, per):
    g, r, c = m.shape
    m = m.reshape(g // per, per, r, 1, c)
    on_diag = jnp.arange(per)[:, None, None, None] == jnp.arange(per)[None, None, :, None]
    return jnp.where(on_diag, m, jnp.zeros((), m.dtype)).reshape(g // per, per * r, per * c)


def _complex_powers(ar, ai, n):
    pr, pi = [jnp.ones_like(ar)], [jnp.zeros_like(ai)]
    for _ in range(n):
        pr, pi = pr + [pr[-1] * ar - pi[-1] * ai], pi + [pr[-1] * ai + pi[-1] * ar]
    return pr, pi


def _s5_params(lam_re, lam_im, log_step, b_re, b_im, c_re, c_im, d_skip, glu_w, glu_b):
    per = LANES // S5_GROUP
    hp = lax.Precision.HIGHEST
    br, bi = b_re.astype(F32), b_im.astype(F32)
    out = dict(pw=[])
    w = {}
    for dirn in range(2):
        reverse = dirn == 1
        lr = jnp.minimum(lam_re[dirn].astype(F32), -1e-4)
        li = lam_im[dirn].astype(F32)
        step = jnp.exp(log_step[dirn].astype(F32))[:, None]
        mag = jnp.exp(lr * step)
        abr, abi = mag * jnp.cos(li * step), mag * jnp.sin(li * step)
        den = lr * lr + li * li
        zr = ((abr - 1.0) * lr + abi * li) / den
        zi = (abi * lr - (abr - 1.0) * li) / den
        bbr = zr[..., None] * br - zi[..., None] * bi
        bbi = zr[..., None] * bi + zi[..., None] * br
        cr, ci = c_re[dirn].astype(F32), c_im[dirn].astype(F32)
        pr, pi = _complex_powers(abr, abi, S5_TILE)

        pra, pia = jnp.stack(pr), jnp.stack(pi)
        g = abr.shape[0]
        lbs, wide = g // per, S5_TILE * LANES
        wr = pra[..., None] * bbr - pia[..., None] * bbi
        wi = pra[..., None] * bbi + pia[..., None] * bbr

        def bdiag(m):
            e = m.shape[0]
            return _block_diag(m.reshape((e * g,) + m.shape[2:]), per).reshape(
                (e, lbs, per * m.shape[2], per * m.shape[3]))

        taps = (jnp.einsum("gjp,egpi->egij", cr, wr[:S5_TILE], precision=hp)
                - jnp.einsum("gjp,egpi->egij", ci, wi[:S5_TILE], precision=hp))
        taps = bdiag(jnp.concatenate([taps, jnp.zeros_like(taps[:1])], axis=0))
        pos = jnp.arange(S5_TILE)
        dist = (pos[:, None] - pos[None, :]) if reverse else (pos[None, :] - pos[:, None])
        t_dir = taps[jnp.where(dist >= 0, dist, S5_TILE)]
        t_dir = t_dir.transpose(2, 0, 3, 1, 4).reshape(lbs, wide, wide)
        w["t"] = t_dir if dirn == 0 else w["t"] + t_dir
        e_in = pos if reverse else S5_TILE - 1 - pos
        p_mat = jnp.concatenate([bdiag(jnp.swapaxes(wr[e_in], 2, 3).astype(BF16)),
                                 bdiag(jnp.swapaxes(wi[e_in], 2, 3).astype(BF16))], axis=3)
        w["pb" if reverse else "pf"] = p_mat.transpose(1, 0, 2, 3).reshape(lbs, wide, -1)
        e_out = (S5_TILE - pos) if reverse else (pos + 1)
        o_re = jnp.swapaxes(cr * pra[e_out][:, :, None, :] - ci * pia[e_out][:, :, None, :], 2, 3)
        o_im = jnp.swapaxes(-(cr * pia[e_out][:, :, None, :] + ci * pra[e_out][:, :, None, :]), 2, 3)
        o_mat = jnp.concatenate([bdiag(o_re.astype(BF16)), bdiag(o_im.astype(BF16))], axis=2)
        w["ob" if reverse else "of"] = o_mat.transpose(1, 2, 0, 3).reshape(lbs, -1, wide)
        qr, qi = _complex_powers(pr[S5_TILE], pi[S5_TILE], SUBLANES)
        g = abr.shape[0]
        flat = lambda a: a.reshape(g // per, per * a.shape[1])
        t = jnp.arange(SUBLANES)[None, :, None]
        tiles = []
        for dist in (1, 2, 4):
            valid = (t < SUBLANES - dist) if reverse else (t >= dist)
            tiles.append(jnp.where(valid, flat(qr[dist])[:, None, :], 0.0))
            tiles.append(jnp.where(valid, flat(qi[dist])[:, None, :], 0.0))
        order = list(range(SUBLANES, 0, -1)) if reverse else list(range(1, SUBLANES + 1))
        tiles.append(jnp.stack([flat(qr[o]) for o in order], axis=1))
        tiles.append(jnp.stack([flat(qi[o]) for o in order], axis=1))
        out["pw"].append(jnp.stack(tiles, axis=1).astype(F32))
    w["t"] = w["t"].astype(BF16)
    out["w"] = w
    d = glu_w.shape[0]
    out.update(dskip=d_skip.astype(F32)[None, :],
               glu_wa=glu_w[:, :d].astype(BF16), glu_wg=glu_w[:, d:].astype(BF16),
               glu_ba=glu_b[:d].astype(F32)[None, :], glu_bg=glu_b[d:].astype(F32)[None, :])
    return out


def _head_pad_cols(w, per_head, take):
    kdim = w.shape[0]
    w = w.reshape(kdim, MLA_HEADS, per_head)[:, :, take]
    w = jnp.pad(w, ((0, 0), (0, 0), (0, HEAD_PAD - w.shape[2])))
    return w.reshape(kdim, MLA_HEADS * HEAD_PAD)


def _rope_swap(w):
    idx = jnp.arange(MLA_ROPE)
    blk, pos = idx // (2 * ROPE_FREQ), idx % (2 * ROPE_FREQ)
    return w[..., blk * 2 * ROPE_FREQ + (pos + ROPE_FREQ) % (2 * ROPE_FREQ)]


def _mla_params(in_w, q_norm_w, kv_norm_w, qb_w, kvb_w, out_w, seq, ctx_len):
    d = in_w.shape[0]
    qr, kvr = q_norm_w.shape[0], kv_norm_w.shape[0]
    dq = MLA_NOPE + MLA_ROPE
    w_q, w_kv, w_kr = in_w[:, :qr], in_w[:, qr:qr + kvr], in_w[:, qr + kvr:]

    def place(w):
        return jnp.pad(w, ((0, 0), (MLA_NOPE, HEAD_PAD - dq)))

    w_in = jnp.concatenate([w_q, place(w_kr), w_kv, place(_rope_swap(w_kr))], axis=1).astype(BF16)
    qb = qb_w.reshape(qr, MLA_HEADS, dq)
    qa = jnp.pad(qb, ((0, 0), (0, 0), (0, HEAD_PAD - dq))).reshape(qr, -1)
    qb_sw = jnp.pad(_rope_swap(qb[:, :, MLA_NOPE:]), ((0, 0), (0, 0), (MLA_NOPE, HEAD_PAD - dq))).reshape(qr, -1)
    per = MLA_NOPE + MLA_V
    w_k = _head_pad_cols(kvb_w, per, slice(0, MLA_NOPE))
    w_v = _head_pad_cols(kvb_w, per, slice(MLA_NOPE, per))
    w_out = jnp.pad(out_w.reshape(MLA_HEADS, MLA_V, d), ((0, 0), (0, HEAD_PAD - MLA_V), (0, 0))).reshape(-1, d)
    rows = seq // GRID_W
    row = jnp.repeat(jnp.arange(rows, dtype=F32), GRID_W)
    col = jnp.tile(jnp.arange(GRID_W, dtype=F32), rows)
    inv_freq = ROPE_BASE ** (-jnp.arange(ROPE_FREQ, dtype=F32) / ROPE_FREQ)
    ang = jnp.stack([row[:, None] * inv_freq, col[:, None] * inv_freq], axis=1)
    cos, sin = jnp.cos(ang), jnp.sin(ang)
    cos32 = jnp.concatenate([cos, cos], axis=2).reshape(seq, MLA_ROPE)
    sin32 = jnp.concatenate([-sin, sin], axis=2).reshape(seq, MLA_ROPE)
    ones = jnp.ones((seq, MLA_NOPE), F32)
    cos_l = jnp.pad(jnp.concatenate([ones, cos32], axis=1), ((0, 0), (0, HEAD_PAD - dq)))
    sin_l = jnp.pad(sin32, ((0, 0), (MLA_NOPE, HEAD_PAD - dq)))
    cos_c = jnp.pad(jnp.ones((ctx_len, dq), F32), ((0, 0), (0, HEAD_PAD - dq)))
    sin_c = jnp.zeros((ctx_len, HEAD_PAD), F32)
    return dict(q_rank=qr, kv_rank=kvr, w_in=w_in, q_norm=q_norm_w.astype(F32)[None, :],
                kv_norm=kv_norm_w.astype(F32)[None, :], w_qa=qa.astype(BF16), w_qb=qb_sw.astype(BF16),
                w_k=w_k.astype(BF16), w_v=w_v.astype(BF16), w_out=w_out.astype(BF16),
                v_ones=jnp.tile((jnp.arange(HEAD_PAD) == V_ONES_LANE).astype(F32), MLA_HEADS)[None, :],
                tables=dict(lat=(cos_l, sin_l), ctx=(cos_c, sin_c)))


def _modulation(c, c_ctx, ada_w, ada_b):
    nb, d = c.shape
    rows = jnp.zeros((SUBLANES, d), F32).at[:nb].set(c).at[nb].set(c_ctx)
    n = ada_w.shape[1]
    (mod,) = _fused_matmul([(rows, (SUBLANES, d), lambda i, j: (0, 0))], [ada_w.astype(BF16)],
                           [_const_tile_in(ada_b.astype(F32)[None, :], _pick(n, (1024, 512, 256, 128)))],
                           _silu, lambda accs, b: [accs[0] + b], rows=SUBLANES, tm=SUBLANES, k=d, n=n,
                           tn=_pick(n, (1024, 512, 256, 128)), out_dtypes=[F32])
    names = ("sh1", "sc1", "g1", "sh2", "sc2", "g2")
    parts = jnp.split(mod, 6, axis=1)
    lat = {nm: pt[:nb][:, None, :] for nm, pt in zip(names, parts)}
    ctx = {nm: pt[nb:nb + 1][:, None, :] for nm, pt in zip(names, parts)}
    return lat, ctx


def kernel(x, c, ctx, c_ctx, ada_w, ada_b, norm1_w, norm2_w, ffn_w13, ffn_w2, m2_in_w, m2_conv_w, m2_conv_b, m2_dt_bias, m2_a_log, m2_d, m2_norm_w, m2_out_w, lru_in_w, lru_conv_w, lru_conv_b, lru_gate_w, lru_gate_b, lru_a_param, lru_out_w, s5_lambda_re, s5_lambda_im, s5_log_step, s5_b_re, s5_b_im, s5_c_re, s5_c_im, s5_d, s5_glu_w, s5_glu_b, mla_in_w, mla_q_norm_w, mla_kv_norm_w, mla_qb_w, mla_kvb_w, mla_out_w, final_norm_w):
    nb, seq, d = x.shape
    ctx_len = ctx.shape[1]
    depth = ada_w.shape[0]
    hidden = ffn_w2.shape[1]
    xl = x.reshape(nb * seq, d).astype(F32)
    xc = ctx.reshape(nb * ctx_len, d).astype(F32)
    for i in range(depth):
        kind, j = i % N_MIXERS, i // N_MIXERS
        ctx_out = i < depth - 1
        mods_l, mods_c = _modulation(c.astype(F32), c_ctx.astype(F32), ada_w[i], ada_b[i])
        lat_s = _Stream(xl, nb, seq, mods_l)
        ctx_s = _Stream(xc, nb, ctx_len, mods_c)
        n1 = norm1_w[i].astype(F32)[None, :]
        if kind == 0:
            p = _mamba_params(m2_in_w[j], m2_conv_w[j], m2_conv_b[j], m2_dt_bias[j], m2_a_log[j], m2_d[j],
                              m2_norm_w[j], m2_out_w[j])
            p["norm1"] = n1
            outs = _mamba_layer(lat_s, ctx_s, p, ctx_out)
        elif kind == 1:
            p = _rglru_params(lru_in_w[j], lru_conv_w[j], lru_conv_b[j], lru_gate_w[j], lru_gate_b[j],
                              lru_a_param[j], lru_out_w[j])
            p["norm1"] = n1
            outs = _rglru_layer(lat_s, ctx_s, p, ctx_out)
        elif kind == 2:
            p = _s5_params(s5_lambda_re[j], s5_lambda_im[j], s5_log_step[j], s5_b_re[j], s5_b_im[j],
                           s5_c_re[j], s5_c_im[j], s5_d[j], s5_glu_w[j], s5_glu_b[j])
            p["norm1"] = n1
            outs = _s5_layer(lat_s, ctx_s, p, ctx_out)
        else:
            p = _mla_params(mla_in_w[j], mla_q_norm_w[j], mla_kv_norm_w[j], mla_qb_w[j], mla_kvb_w[j],
                            mla_out_w[j], seq, ctx_len)
            p["norm1"] = n1
            outs = _mla_layer(lat_s, ctx_s, p, ctx_out)
        n2 = norm2_w[i].astype(F32)[None, :]
        w1 = ffn_w13[i][:, :hidden].astype(BF16)
        w3 = ffn_w13[i][:, hidden:].astype(BF16)
        w2 = ffn_w2[i].astype(BF16)
        lat_s = _Stream(outs["lat"], nb, seq, mods_l)
        xl = _ffn(lat_s, n2, mods_l["sc2"], mods_l["sh2"], mods_l["g2"], w1, w3, w2)
        if ctx_out:
            ctx_s = _Stream(outs["ctx"], nb, ctx_len, mods_c)
            xc = _ffn(ctx_s, n2, mods_c["sc2"], mods_c["sh2"], mods_c["g2"], w1, w3, w2)
    zero = jnp.zeros((1, 1, d), F32)
    lat_s = _Stream(xl, nb, seq, None)
    out = _norm_mod_rows(xl, final_norm_w.astype(F32)[None, :], zero, zero, tm=lat_s.tm, tpb=lat_s.tpb)
    return out.reshape(nb, seq, d).astype(x.dtype)
```
